```python
import math
import jax, jax.numpy as jnp
from jax import lax
import numpy as np

D_MODEL = 1024
BATCH = 2
SEQ = 8192
DEPTH = 2
DEC_BATCH = 16
DEC_SEQ = 32
PAST_LEN = 4096

CHUNK = 64
N_A_LAYERS = DEPTH // 2
N_B_LAYERS = DEPTH - N_A_LAYERS
RET_HEADS = 4
RET_DK = D_MODEL // RET_HEADS
RET_DV = 2 * RET_DK
RET_THETA = 10000.0
DIFF_HEADS = 8
DIFF_DH = D_MODEL // (2 * DIFF_HEADS)
DIFF_DV = 2 * DIFF_DH
ROPE_THETA = 500000.0
ROPE_DIM = DIFF_DH // 4
Q_BLOCK = 128
N_GROUPS = 4
EXPERTS_PER_GROUP = 4
N_EXPERTS = N_GROUPS * EXPERTS_PER_GROUP
TOP_K_IN_GROUP = 2
D_EXPERT = D_MODEL // 4
EPS = 1e-6

kernel_name = 'yoco_retention_diffattn_hmoe_stream_step'


def rmsnorm(x, g):
    xf = x.astype(jnp.float32)
    y = xf * lax.rsqrt(jnp.mean(xf * xf, axis=-1, keepdims=True) + EPS)
    return (y * g.astype(jnp.float32)).astype(x.dtype)


def head_rms(x):
    return x * lax.rsqrt(jnp.mean(x * x, axis=-1, keepdims=True) + EPS)


def rotary(x, pos, rot_dim, theta):
    half = rot_dim // 2
    inv_freq = jnp.power(jnp.float32(theta), -jnp.arange(half, dtype=jnp.float32) / half)
    ang = pos.astype(jnp.float32)[:, None] * inv_freq[None, :]
    shape = (1, pos.shape[0]) + (1,) * (x.ndim - 3) + (half,)
    cos = jnp.cos(ang).reshape(shape)
    sin = jnp.sin(ang).reshape(shape)
    xr = x[..., :rot_dim].astype(jnp.float32)
    x1, x2 = xr[..., :half], xr[..., half:]
    rot = jnp.concatenate([x1 * cos - x2 * sin, x2 * cos + x1 * sin], axis=-1).astype(x.dtype)
    if rot_dim == x.shape[-1]:
        return rot
    return jnp.concatenate([rot, x[..., rot_dim:]], axis=-1)


def retention_chunk(R, q, k, v):
    L = q.shape[1]
    lg = jnp.log1p(-jnp.exp2(-5.0 - jnp.arange(RET_HEADS, dtype=jnp.float32)))
    i = jnp.arange(L, dtype=jnp.float32)
    diff = i[:, None] - i[None, :]
    dmask = jnp.where(diff >= 0, jnp.exp(lg[:, None, None] * jnp.maximum(diff, 0.0)), 0.0)
    xi = jnp.exp(lg[:, None] * (i[None, :] + 1.0)).T[None, :, :, None]
    zeta = jnp.exp(lg[:, None] * (L - 1.0 - i[None, :])).T[None, :, :, None]
    qf = q.astype(jnp.float32)
    kf = k.astype(jnp.float32)
    vf = v.astype(jnp.float32)
    s = jnp.einsum('bihk,bjhk->bhij', qf, kf) * dmask[None]
    intra = jnp.einsum('bhij,bjhv->bihv', s, vf)
    inter = jnp.einsum('bihk,bhkv->bihv', qf * xi, R)
    R_new = jnp.exp(lg * L)[None, :, None, None] * R + jnp.einsum('bjhk,bjhv->bhkv', kf * zeta, vf)
    return intra + inter, R_new


def retention_seq(q, k, v, R0):
    B, T = q.shape[0], q.shape[1]
    R0 = R0.astype(jnp.float32)
    if T <= CHUNK:
        return retention_chunk(R0, q, k, v)
    n = T // CHUNK

    def split(a):
        return a.reshape((B, n, CHUNK) + a.shape[2:]).swapaxes(0, 1)

    def step(R, xs):
        o, R = retention_chunk(R, xs[0], xs[1], xs[2])
        return R, o

    R, o = lax.scan(step, R0, (split(q), split(k), split(v)))
    o = o.swapaxes(0, 1).reshape(B, T, RET_HEADS, RET_DV)
    return o, R


def retention_mixer(u, pos, R0, w_in, w_out):
    B, T, _ = u.shape
    proj = jnp.dot(u, w_in)
    c1 = RET_HEADS * RET_DK
    c2 = 2 * c1
    c3 = c2 + RET_HEADS * RET_DV
    q = rotary(proj[..., :c1].reshape(B, T, RET_HEADS, RET_DK), pos, RET_DK, RET_THETA)
    k = rotary(proj[..., c1:c2].reshape(B, T, RET_HEADS, RET_DK), pos, RET_DK, RET_THETA) * (RET_DK ** -0.5)
    v = proj[..., c2:c3].reshape(B, T, RET_HEADS, RET_DV)
    g = proj[..., c3:]
    o, R = retention_seq(q, k, v, R0)
    o = head_rms(o).reshape(B, T, RET_HEADS * RET_DV).astype(u.dtype)
    return jnp.dot(jax.nn.silu(g) * o, w_out), R


def shared_kv(h, pos, kv_norm, w_k, w_v):
    B, T, _ = h.shape
    u = rmsnorm(h, kv_norm)
    k = jnp.dot(u, w_k).reshape(B, T, DIFF_HEADS, 2, DIFF_DH)
    k = rotary(k, pos, ROPE_DIM, ROPE_THETA)
    v = jnp.dot(u, w_v).reshape(B, T, DIFF_HEADS, DIFF_DV)
    return k, v


def diff_attend_block(qb, qpos, k, v, kpos, lam):
    s = jnp.einsum('bqhcd,bshcd->bhcqs', qb, k).astype(jnp.float32) * (DIFF_DH ** -0.5)
    limit = (qpos // CHUNK + 1) * CHUNK
    mask = kpos[None, :] < limit[:, None]
    s = jnp.where(mask[None, None, None], s, -jnp.inf)
    p = jax.nn.softmax(s, axis=-1)
    a = p[:, :, 0] - lam * p[:, :, 1]
    return jnp.einsum('bhqs,bshv->bqhv', a.astype(v.dtype), v)


def diff_mixer(u, pos, k, v, kpos, w_q, lq1, lk1, lq2, lk2, subln, w_o, lam_init):
    B, T, _ = u.shape
    q = jnp.dot(u, w_q).reshape(B, T, DIFF_HEADS, 2, DIFF_DH)
    q = rotary(q, pos, ROPE_DIM, ROPE_THETA)
    f32 = jnp.float32
    lam = (jnp.exp(jnp.sum(lq1.astype(f32) * lk1.astype(f32)))
           - jnp.exp(jnp.sum(lq2.astype(f32) * lk2.astype(f32))) + lam_init)
    if T <= Q_BLOCK:
        o = diff_attend_block(q, pos, k, v, kpos, lam)
    else:
        n = T // Q_BLOCK
        qb = q.reshape(B, n, Q_BLOCK, DIFF_HEADS, 2, DIFF_DH).swapaxes(0, 1)
        pb = pos.reshape(n, Q_BLOCK)
        o = lax.map(lambda xs: diff_attend_block(xs[0], xs[1], k, v, kpos, lam), (qb, pb))
        o = o.swapaxes(0, 1).reshape(B, T, DIFF_HEADS, DIFF_DV)
    o = rmsnorm(o, subln) * (1.0 - lam_init)
    return jnp.dot(o.reshape(B, T, DIFF_HEADS * DIFF_DV), w_o)


def hier_moe(u, w_group, b_group, w_route, b_route, w_gate, w_up, w_down):
    f32 = jnp.float32
    g_prob = jax.nn.softmax(jnp.dot(u, w_group).astype(f32) + b_group.astype(f32), axis=-1)
    p_top, g_top = lax.top_k(g_prob, 1)
    e_logits = jnp.einsum('nd,gde->nge', u, w_route).astype(f32) + b_route.astype(f32)
    e_logits = jnp.einsum('nge,ng->ne', e_logits, jax.nn.one_hot(g_top[:, 0], N_GROUPS, dtype=f32))
    e_prob = jax.nn.softmax(e_logits, axis=-1)
    w_top, e_top = lax.top_k(e_prob, TOP_K_IN_GROUP)
    w_top = w_top / jnp.sum(w_top, axis=-1, keepdims=True) * p_top
    expert_id = g_top * EXPERTS_PER_GROUP + e_top
    gate = jnp.einsum('nk,nke->ne', w_top, jax.nn.one_hot(expert_id, N_EXPERTS, dtype=f32))
    hid = jax.nn.silu(jnp.einsum('nd,edf->nef', u, w_gate)) * jnp.einsum('nd,edf->nef', u, w_up)
    hid = hid * gate[:, :, None].astype(hid.dtype)
    return jnp.einsum('nef,efd->nd', hid, w_down)


def trunk(x, pos, R0, past_k, past_v, p):
    B, T, D = x.shape
    h = x
    new_R = []
    k_new = v_new = k_all = v_all = kpos = None
    for layer in range(DEPTH):
        u = rmsnorm(h, p['norm_mix'][layer])
        if layer < N_A_LAYERS:
            y, R = retention_mixer(u, pos, R0[layer], p['ret_w_in'][layer], p['ret_w_out'][layer])
            new_R.append(R)
        else:
            if layer == N_A_LAYERS:
                k_new, v_new = shared_kv(h, pos, p['kv_norm'], p['kv_w_k'], p['kv_w_v'])
                if past_k is None:
                    k_all, v_all, kpos = k_new, v_new, pos
                else:
                    k_all = jnp.concatenate([past_k.astype(k_new.dtype), k_new], axis=1)
                    v_all = jnp.concatenate([past_v.astype(v_new.dtype), v_new], axis=1)
                    kpos = jnp.arange(past_k.shape[1] + T)
            j = layer - N_A_LAYERS
            lam_init = 0.8 - 0.6 * math.exp(-0.3 * layer)
            y = diff_mixer(u, pos, k_all, v_all, kpos, p['diff_w_q'][j], p['diff_lam_q1'][j],
                           p['diff_lam_k1'][j], p['diff_lam_q2'][j], p['diff_lam_k2'][j],
                           p['diff_subln'][j], p['diff_w_o'][j], lam_init)
        h = h + y
        u = rmsnorm(h, p['norm_ffn'][layer]).reshape(B * T, D)
        h = h + hier_moe(u, p['moe_w_group'][layer], p['moe_b_group'][layer], p['moe_w_route'][layer],
                         p['moe_b_route'][layer], p['moe_w_gate'][layer], p['moe_w_up'][layer],
                         p['moe_w_down'][layer]).reshape(B, T, D)
    return rmsnorm(h, p['norm_final']), jnp.stack(new_R), k_new, v_new


def setup_inputs(seed: int = 0) -> dict:
    key = jax.random.key(seed)
    ks = jax.random.split(key, 32)
    f32 = jnp.float32
    D = D_MODEL

    def nrm(k, shape, scale):
        return jax.random.normal(k, shape, f32) * scale

    n_in = 2 * RET_HEADS * RET_DK + 2 * RET_HEADS * RET_DV
    return {
        'x_prompt': nrm(ks[0], (BATCH, SEQ, D), 1.0),
        'x_sample': nrm(ks[1], (DEC_BATCH, DEC_SEQ, D), 1.0),
        'state_ret': nrm(ks[2], (N_A_LAYERS, DEC_BATCH, RET_HEADS, RET_DK, RET_DV), 0.05),
        'cache_k': nrm(ks[3], (DEC_BATCH, PAST_LEN, DIFF_HEADS, 2, DIFF_DH), 1.0),
        'cache_v': nrm(ks[4], (DEC_BATCH, PAST_LEN, DIFF_HEADS, DIFF_DV), 1.0),
        'norm_mix': 1.0 + nrm(ks[5], (DEPTH, D), 0.01),
        'norm_ffn': 1.0 + nrm(ks[6], (DEPTH, D), 0.01),
        'norm_final': 1.0 + nrm(ks[7], (D,), 0.01),
        'ret_w_in': nrm(ks[8], (N_A_LAYERS, D, n_in), D ** -0.5),
        'ret_w_out': nrm(ks[9], (N_A_LAYERS, RET_HEADS * RET_DV, D), (RET_HEADS * RET_DV) ** -0.5),
        'kv_norm': 1.0 + nrm(ks[10], (D,), 0.01),
        'kv_w_k': nrm(ks[11], (D, DIFF_HEADS * 2 * DIFF_DH), D ** -0.5),
        'kv_w_v': nrm(ks[12], (D, DIFF_HEADS * DIFF_DV), D ** -0.5),
        'diff_w_q': nrm(ks[13], (N_B_LAYERS, D, DIFF_HEADS * 2 * DIFF_DH), D ** -0.5),
        'diff_lam_q1': nrm(ks[14], (N_B_LAYERS, DIFF_DH), 0.1),
        'diff_lam_k1': nrm(ks[15], (N_B_LAYERS, DIFF_DH), 0.1),
        'diff_lam_q2': nrm(ks[16], (N_B_LAYERS, DIFF_DH), 0.1),
        'diff_lam_k2': nrm(ks[17], (N_B_LAYERS, DIFF_DH), 0.1),
        'diff_subln': 1.0 + nrm(ks[18], (N_B_LAYERS, DIFF_DV), 0.01),
        'diff_w_o': nrm(ks[19], (N_B_LAYERS, DIFF_HEADS * DIFF_DV, D), (DIFF_HEADS * DIFF_DV) ** -0.5),
        'moe_w_group': nrm(ks[20], (DEPTH, D, N_GROUPS), D ** -0.5),
        'moe_b_group': nrm(ks[21], (DEPTH, N_GROUPS), 0.01),
        'moe_w_route': nrm(ks[22], (DEPTH, N_GROUPS, D, EXPERTS_PER_GROUP), D ** -0.5),
        'moe_b_route': nrm(ks[23], (DEPTH, N_GROUPS, EXPERTS_PER_GROUP), 0.01),
        'moe_w_gate': nrm(ks[24], (DEPTH, N_EXPERTS, D, D_EXPERT), D ** -0.5),
        'moe_w_up': nrm(ks[25], (DEPTH, N_EXPERTS, D, D_EXPERT), D ** -0.5),
        'moe_w_down': nrm(ks[26], (DEPTH, N_EXPERTS, D_EXPERT, D), D_EXPERT ** -0.5),
    }


def reference(x_prompt, x_sample, state_ret, cache_k, cache_v, norm_mix, norm_ffn, norm_final,
              ret_w_in, ret_w_out, kv_norm, kv_w_k, kv_w_v, diff_w_q, diff_lam_q1, diff_lam_k1,
              diff_lam_q2, diff_lam_k2, diff_subln, diff_w_o, moe_w_group, moe_b_group, moe_w_route,
              moe_b_route, moe_w_gate, moe_w_up, moe_w_down):
    params = {
        'norm_mix': norm_mix, 'norm_ffn': norm_ffn, 'norm_final': norm_final,
        'ret_w_in': ret_w_in, 'ret_w_out': ret_w_out,
        'kv_norm': kv_norm, 'kv_w_k': kv_w_k, 'kv_w_v': kv_w_v,
        'diff_w_q': diff_w_q, 'diff_lam_q1': diff_lam_q1, 'diff_lam_k1': diff_lam_k1,
        'diff_lam_q2': diff_lam_q2, 'diff_lam_k2': diff_lam_k2, 'diff_subln': diff_subln,
        'diff_w_o': diff_w_o,
        'moe_w_group': moe_w_group, 'moe_b_group': moe_b_group, 'moe_w_route': moe_w_route,
        'moe_b_route': moe_b_route, 'moe_w_gate': moe_w_gate, 'moe_w_up': moe_w_up,
        'moe_w_down': moe_w_down,
    }
    bp, tp = x_prompt.shape[0], x_prompt.shape[1]
    pos_p = jnp.arange(tp)
    R0_p = jnp.zeros((N_A_LAYERS, bp, RET_HEADS, RET_DK, RET_DV), jnp.float32)
    y_prompt, state_ret_prompt, k_prompt, v_prompt = trunk(x_prompt, pos_p, R0_p, None, None, params)
    past = cache_k.shape[1]
    pos_s = past + jnp.arange(x_sample.shape[1])
    y_sample, state_ret_sample, k_sample, v_sample = trunk(x_sample, pos_s, state_ret, cache_k, cache_v, params)
    return (y_prompt, y_sample, state_ret_prompt, k_prompt, v_prompt, state_ret_sample, k_sample, v_sample)
```

```python
import functools
import math

import numpy as np
import jax
import jax.numpy as jnp
from jax import lax
from jax.experimental import pallas as pl
from jax.experimental.pallas import tpu as pltpu

F32 = jnp.float32
BF16 = jnp.bfloat16

D_MODEL = 1024
CHUNK = 64
RET_HEADS = 4
RET_DK = 256
RET_DV = 512
RET_THETA = 10000.0
DIFF_HEADS = 8
DIFF_DH = 64
DIFF_DV = 128
ROPE_THETA = 500000.0
ROPE_DIM = 16
N_GROUPS = 4
EXPERTS_PER_GROUP = 4
N_EXPERTS = 16
D_EXPERT = 256
EPS = 1e-6

LANES = 128
VMEM_LIMIT = 56 * 1024 * 1024
ROUTE_LANES = 128
GROUP_LANE0 = N_EXPERTS
NEG_BIG = -1e30


def _cparams(sem):
    return pltpu.CompilerParams(dimension_semantics=sem, vmem_limit_bytes=VMEM_LIMIT)


def _rms(x, g):
    ms = jnp.mean(x * x, axis=-1, keepdims=True)
    return x * lax.rsqrt(ms + EPS) * g


def _silu(x):
    return x * (1.0 / (1.0 + jnp.exp(-x)))


def _ret_inproj_kernel(x_ref, g_ref, w_ref, cos_ref, sin_ref, o_ref, xn_ref, *, tn):
    j = pl.program_id(1)
    n_q = (RET_HEADS * RET_DK) // tn
    n_qk = 2 * n_q
    n_qkv = n_qk + (RET_HEADS * RET_DV) // tn

    @pl.when(j == 0)
    def _():
        xn_ref[...] = _rms(x_ref[...], g_ref[...]).astype(BF16)

    y = jnp.dot(xn_ref[...], w_ref[...], preferred_element_type=F32)

    @pl.when(j < n_qk)
    def _():
        cos = cos_ref[...]
        sin = sin_ref[...]
        scale = jnp.where(j >= n_q, RET_DK ** -0.5, 1.0).astype(F32)
        half = RET_DK // 2
        for h in range(tn // RET_DK):
            x1 = y[:, h * RET_DK:h * RET_DK + half]
            x2 = y[:, h * RET_DK + half:(h + 1) * RET_DK]
            o_ref[:, h * RET_DK:h * RET_DK + half] = ((x1 * cos - x2 * sin) * scale).astype(BF16)
            o_ref[:, h * RET_DK + half:(h + 1) * RET_DK] = ((x2 * cos + x1 * sin) * scale).astype(BF16)

    @pl.when(jnp.logical_and(j >= n_qk, j < n_qkv))
    def _():
        o_ref[...] = y.astype(BF16)

    @pl.when(j >= n_qkv)
    def _():
        o_ref[...] = _silu(y).astype(BF16)


def ret_inproj(x, g, w, cos, sin, *, tm, tn=512):
    n, d = x.shape
    m = w.shape[1]
    n_pos_tiles = cos.shape[0] // tm
    return pl.pallas_call(
        functools.partial(_ret_inproj_kernel, tn=tn),
        grid=(n // tm, m // tn),
        in_specs=[
            pl.BlockSpec((tm, d), lambda i, j: (i, 0)),
            pl.BlockSpec((1, d), lambda i, j: (0, 0)),
            pl.BlockSpec((d, tn), lambda i, j: (0, j)),
            pl.BlockSpec((tm, RET_DK // 2), lambda i, j: (i % n_pos_tiles, 0)),
            pl.BlockSpec((tm, RET_DK // 2), lambda i, j: (i % n_pos_tiles, 0)),
        ],
        out_specs=pl.BlockSpec((tm, tn), lambda i, j: (i, j)),
        out_shape=jax.ShapeDtypeStruct((n, m), BF16),
        scratch_shapes=[pltpu.VMEM((tm, d), BF16)],
        compiler_params=_cparams(("parallel", "arbitrary")),
        name="ret_inproj",
    )(x, g, w, cos, sin)


def _retention_kernel(*refs, L, n_chunks, has_r0):
    if has_r0:
        (q_ref, k_ref, v_ref, g_ref, dm_ref, xi_ref, zt_ref, r0_ref, o_ref, rout_ref, r_ref) = refs
    else:
        (q_ref, k_ref, v_ref, g_ref, dm_ref, xi_ref, zt_ref, o_ref, rout_ref, r_ref) = refs
    t = pl.program_id(1)

    @pl.when(t == 0)
    def _():
        if has_r0:
            r_ref[...] = r0_ref[0]
        else:
            r_ref[...] = jnp.zeros_like(r_ref)

    def chunk(c, carry):
        r0 = pl.multiple_of(c * L, L)
        for h in range(RET_HEADS):
            lg = math.log1p(-2.0 ** (-5.0 - h))
            q = q_ref[pl.ds(r0, L), h * RET_DK:(h + 1) * RET_DK]
            k = k_ref[pl.ds(r0, L), h * RET_DK:(h + 1) * RET_DK]
            v = v_ref[pl.ds(r0, L), h * RET_DV:(h + 1) * RET_DV]
            gate = g_ref[pl.ds(r0, L), h * RET_DV:(h + 1) * RET_DV]
            rh = r_ref[h]
            s = lax.dot_general(q, k, (((1,), (1,)), ((), ())), preferred_element_type=F32)
            s = s * dm_ref[h]
            intra = jnp.dot(s.astype(BF16), v, preferred_element_type=F32)
            inter = jnp.dot(q, rh.astype(BF16), preferred_element_type=F32) * xi_ref[h]
            o = intra + inter
            kz = (k.astype(F32) * zt_ref[h]).astype(BF16)
            upd = lax.dot_general(kz, v, (((0,), (0,)), ((), ())), preferred_element_type=F32)
            r_ref[h] = math.exp(lg * L) * rh + upd
            on = o * lax.rsqrt(jnp.mean(o * o, axis=-1, keepdims=True) + EPS)
            o_ref[pl.ds(r0, L), h * RET_DV:(h + 1) * RET_DV] = (on * gate.astype(F32)).astype(BF16)
        return carry

    lax.fori_loop(0, n_chunks, chunk, 0)

    @pl.when(t == pl.num_programs(1) - 1)
    def _():
        rout_ref[0] = r_ref[...]


def _retention_tables(L):
    lg = np.log1p(-np.exp2(-5.0 - np.arange(RET_HEADS, dtype=np.float64)))
    i = np.arange(L, dtype=np.float64)
    diff = i[:, None] - i[None, :]
    dmask = np.where(diff >= 0, np.exp(lg[:, None, None] * np.maximum(diff, 0.0)), 0.0)
    xi = np.exp(lg[:, None] * (i[None, :] + 1.0))
    zeta = np.exp(lg[:, None] * (L - 1.0 - i[None, :]))
    xi = np.broadcast_to(xi[:, :, None], (RET_HEADS, L, RET_DV))
    zeta = np.broadcast_to(zeta[:, :, None], (RET_HEADS, L, RET_DK))
    return (jnp.asarray(dmask, F32), jnp.asarray(xi, F32), jnp.asarray(zeta, F32))


def retention(proj, r0, *, batch, seq, L, tb):
    n = batch * seq
    nt = seq // tb
    dmask, xi, zeta = _retention_tables(L)
    dqk = RET_HEADS * RET_DK
    dv = RET_HEADS * RET_DV
    has_r0 = r0 is not None
    const3 = lambda b, t: (0, 0, 0)
    in_specs = [
        pl.BlockSpec((tb, dqk), lambda b, t: (b * nt + t, 0)),
        pl.BlockSpec((tb, dqk), lambda b, t: (b * nt + t, 1)),
        pl.BlockSpec((tb, dv), lambda b, t: (b * nt + t, 1)),
        pl.BlockSpec((tb, dv), lambda b, t: (b * nt + t, 2)),
        pl.BlockSpec((RET_HEADS, L, L), const3),
        pl.BlockSpec((RET_HEADS, L, RET_DV), const3),
        pl.BlockSpec((RET_HEADS, L, RET_DK), const3),
    ]
    args = [proj, proj, proj, proj, dmask, xi, zeta]
    state_spec = pl.BlockSpec((1, RET_HEADS, RET_DK, RET_DV), lambda b, t: (b, 0, 0, 0))
    if has_r0:
        in_specs.append(state_spec)
        args.append(r0)
    return pl.pallas_call(
        functools.partial(_retention_kernel, L=L, n_chunks=tb // L, has_r0=has_r0),
        grid=(batch, nt),
        in_specs=in_specs,
        out_specs=[pl.BlockSpec((tb, dv), lambda b, t: (b * nt + t, 0)), state_spec],
        out_shape=[jax.ShapeDtypeStruct((n, dv), BF16),
                   jax.ShapeDtypeStruct((batch, RET_HEADS, RET_DK, RET_DV), F32)],
        scratch_shapes=[pltpu.VMEM((RET_HEADS, RET_DK, RET_DV), F32)],
        compiler_params=_cparams(("parallel", "arbitrary")),
        name="retention",
    )(*args)


def _route(u, wr, br):
    logits = jnp.dot(u, wr, preferred_element_type=F32, precision=lax.Precision.HIGHEST) + br
    lane_i = lax.broadcasted_iota(jnp.int32, logits.shape, 1)
    lane = lane_i.astype(F32)
    lane_grp = (lane_i // EXPERTS_PER_GROUP).astype(F32)
    big = jnp.float32(1 << 20)
    is_grp = jnp.logical_and(lane_i >= GROUP_LANE0, lane_i < GROUP_LANE0 + N_GROUPS)
    gl = jnp.where(is_grp, logits, -jnp.inf)
    gmax = jnp.max(gl, axis=-1, keepdims=True)
    gsum = jnp.sum(jnp.exp(gl - gmax), axis=-1, keepdims=True)
    p_top = 1.0 / gsum
    g_top = jnp.min(jnp.where(gl == gmax, lane, big), axis=-1, keepdims=True) - GROUP_LANE0
    sel = jnp.logical_and(lane_i < N_EXPERTS, lane_grp == g_top)
    el = jnp.where(sel, logits, -jnp.inf)
    emax = jnp.max(el, axis=-1, keepdims=True)
    ee = jnp.exp(el - emax)
    ep = ee / jnp.sum(ee, axis=-1, keepdims=True)
    epm = jnp.where(sel, ep, -1.0)
    m1 = jnp.max(epm, axis=-1, keepdims=True)
    i1 = jnp.min(jnp.where(epm == m1, lane, big), axis=-1, keepdims=True)
    epm2 = jnp.where(lane == i1, -1.0, epm)
    m2 = jnp.max(epm2, axis=-1, keepdims=True)
    i2 = jnp.min(jnp.where(epm2 == m2, lane, big), axis=-1, keepdims=True)
    denom = m1 + m2
    w1 = m1 / denom * p_top
    w2 = m2 / denom * p_top
    return jnp.where(lane == i1, w1, jnp.where(lane == i2, w2, 0.0))


def _outproj_kernel(a_ref, w_ref, res_ref, g_ref, wr_ref, br_ref, h_ref, u_ref, gate_ref):
    h = res_ref[...] + jnp.dot(a_ref[...], w_ref[...], preferred_element_type=F32)
    h_ref[...] = h
    u = _rms(h, g_ref[...])
    u_ref[...] = u.astype(BF16)
    gate_ref[...] = _route(u, wr_ref[...], br_ref[...])


def outproj_route(a, w, res, g, wr, br, *, tm):
    n, kdim = a.shape
    d = w.shape[1]
    return pl.pallas_call(
        _outproj_kernel,
        grid=(n // tm,),
        in_specs=[
            pl.BlockSpec((tm, kdim), lambda i: (i, 0)),
            pl.BlockSpec((kdim, d), lambda i: (0, 0)),
            pl.BlockSpec((tm, d), lambda i: (i, 0)),
            pl.BlockSpec((1, d), lambda i: (0, 0)),
            pl.BlockSpec((d, ROUTE_LANES), lambda i: (0, 0)),
            pl.BlockSpec((1, ROUTE_LANES), lambda i: (0, 0)),
        ],
        out_specs=[
            pl.BlockSpec((tm, d), lambda i: (i, 0)),
            pl.BlockSpec((tm, d), lambda i: (i, 0)),
            pl.BlockSpec((tm, ROUTE_LANES), lambda i: (i, 0)),
        ],
        out_shape=[jax.ShapeDtypeStruct((n, d), F32),
                   jax.ShapeDtypeStruct((n, d), BF16),
                   jax.ShapeDtypeStruct((n, ROUTE_LANES), F32)],
        compiler_params=_cparams(("parallel",)),
        name="outproj_route",
    )(a, w, res, g, wr, br)


def _moe_kernel(u_ref, gate_ref, wg_ref, wu_ref, wd_ref, res_ref, gn_ref, *out_and_scratch, n_norms, emit_h):
    acc_ref = out_and_scratch[-1]
    outs = out_and_scratch[:-1]
    grp = pl.program_id(1)

    @pl.when(grp == 0)
    def _():
        acc_ref[...] = jnp.zeros_like(acc_ref)

    u = u_ref[...]
    gate = gate_ref[...]
    lane = lax.broadcasted_iota(jnp.int32, gate.shape, 1)
    contrib = None
    for e in range(EXPERTS_PER_GROUP):
        ge = jnp.sum(jnp.where(lane == grp * EXPERTS_PER_GROUP + e, gate, 0.0), axis=-1, keepdims=True)
        hg = jnp.dot(u, wg_ref[e], preferred_element_type=F32)
        hu = jnp.dot(u, wu_ref[e], preferred_element_type=F32)
        hid = (_silu(hg) * hu * ge).astype(BF16)
        d = jnp.dot(hid, wd_ref[e], preferred_element_type=F32)
        contrib = d if contrib is None else contrib + d
    acc_ref[...] += contrib

    @pl.when(grp == pl.num_programs(1) - 1)
    def _():
        h = res_ref[...] + acc_ref[...]
        k = 0
        if emit_h:
            outs[0][...] = h
            k = 1
        for t in range(n_norms):
            o = outs[k + t]
            o[...] = _rms(h, gn_ref[t:t + 1, :]).astype(o.dtype)


def moe(u, gate, wg, wu, wd, res, gnorms, *, tm, emit_h, norm_dtypes):
    n, d = u.shape
    n_norms = len(norm_dtypes)
    tok = lambda i, g: (i, 0)
    out_specs = []
    out_shape = []
    if emit_h:
        out_specs.append(pl.BlockSpec((tm, d), tok))
        out_shape.append(jax.ShapeDtypeStruct((n, d), F32))
    for dt in norm_dtypes:
        out_specs.append(pl.BlockSpec((tm, d), tok))
        out_shape.append(jax.ShapeDtypeStruct((n, d), dt))
    return pl.pallas_call(
        functools.partial(_moe_kernel, n_norms=n_norms, emit_h=emit_h),
        grid=(n // tm, N_GROUPS),
        in_specs=[
            pl.BlockSpec((tm, d), tok),
            pl.BlockSpec((tm, ROUTE_LANES), tok),
            pl.BlockSpec((EXPERTS_PER_GROUP, d, D_EXPERT), lambda i, g: (g, 0, 0)),
            pl.BlockSpec((EXPERTS_PER_GROUP, d, D_EXPERT), lambda i, g: (g, 0, 0)),
            pl.BlockSpec((EXPERTS_PER_GROUP, D_EXPERT, d), lambda i, g: (g, 0, 0)),
            pl.BlockSpec((tm, d), tok),
            pl.BlockSpec((n_norms, d), lambda i, g: (0, 0)),
        ],
        out_specs=out_specs,
        out_shape=out_shape,
        scratch_shapes=[pltpu.VMEM((tm, d), F32)],
        compiler_params=_cparams(("parallel", "arbitrary")),
        name="moe",
    )(u, gate, wg, wu, wd, res, gnorms)


def _rope_cols(y, cosf, sina, sinb):
    blocks = []
    for c in range(y.shape[1] // LANES):
        xb = y[:, c * LANES:(c + 1) * LANES]
        blocks.append(xb * cosf + pltpu.roll(xb, LANES - ROPE_DIM // 2, 1) * sina
                      + pltpu.roll(xb, ROPE_DIM // 2, 1) * sinb)
    return blocks


def _qkv_kernel(ukv_ref, umix_ref, wk_ref, wv_ref, wq_ref, cos_ref, sa_ref, sb_ref,
                k32_ref, v32_ref, kb_ref, vb_ref, qb_ref):
    cosf = cos_ref[...]
    sina = sa_ref[...]
    sinb = sb_ref[...]
    ukv = ukv_ref[...]
    k = jnp.dot(ukv, wk_ref[...], preferred_element_type=F32)
    for c, blk in enumerate(_rope_cols(k, cosf, sina, sinb)):
        k32_ref[:, c * LANES:(c + 1) * LANES] = blk
        kb_ref[:, c * LANES:(c + 1) * LANES] = blk.astype(BF16)
    v = jnp.dot(ukv, wv_ref[...], preferred_element_type=F32)
    v32_ref[...] = v
    vb_ref[...] = v.astype(BF16)
    q = jnp.dot(umix_ref[...], wq_ref[...], preferred_element_type=F32)
    for c, blk in enumerate(_rope_cols(q, cosf, sina, sinb)):
        qb_ref[:, c * LANES:(c + 1) * LANES] = (blk * DIFF_DH ** -0.5).astype(BF16)


def qkv_proj(ukv, umix, wk, wv, wq, cosf, sina, sinb, *, tm):
    n, d = ukv.shape
    n_pos_tiles = cosf.shape[0] // tm
    tok = lambda i: (i, 0)
    wspec = pl.BlockSpec((d, d), lambda i: (0, 0))
    tspec = pl.BlockSpec((tm, LANES), lambda i: (i % n_pos_tiles, 0))
    return pl.pallas_call(
        _qkv_kernel,
        grid=(n // tm,),
        in_specs=[pl.BlockSpec((tm, d), tok), pl.BlockSpec((tm, d), tok), wspec, wspec, wspec,
                  tspec, tspec, tspec],
        out_specs=[pl.BlockSpec((tm, d), tok)] * 5,
        out_shape=[jax.ShapeDtypeStruct((n, d), F32), jax.ShapeDtypeStruct((n, d), F32),
                   jax.ShapeDtypeStruct((n, d), BF16), jax.ShapeDtypeStruct((n, d), BF16),
                   jax.ShapeDtypeStruct((n, d), BF16)],
        compiler_params=_cparams(("parallel",)),
        name="qkv_proj",
    )(ukv, umix, wk, wv, wq, cosf, sina, sinb)


def _lambda(lam_ref, lam_init):
    lv = lam_ref[...]
    a = jnp.sum(lv[0:1, :] * lv[1:2, :], axis=-1, keepdims=True)
    b = jnp.sum(lv[2:3, :] * lv[3:4, :], axis=-1, keepdims=True)
    return jnp.exp(a) - jnp.exp(b) + lam_init


def _split_q(q):
    qf = q.astype(F32)
    lane = lax.broadcasted_iota(jnp.int32, qf.shape, 1)
    q0 = jnp.where(lane < DIFF_DH, qf, 0.0).astype(BF16)
    q1 = jnp.where(lane >= DIFF_DH, qf, 0.0).astype(BF16)
    return jnp.concatenate([q0, q1], axis=0)


def _online_update(s, v, m_ref, l_ref, acc_ref):
    m_old = m_ref[...]
    m_new = jnp.maximum(m_old, jnp.max(s, axis=-1, keepdims=True))
    alpha = jnp.exp(m_old - m_new)
    p = jnp.exp(s - m_new)
    l_ref[...] = alpha * l_ref[...] + jnp.sum(p, axis=-1, keepdims=True)
    acc_ref[...] = alpha * acc_ref[...] + jnp.dot(p.astype(BF16), v, preferred_element_type=F32)
    m_ref[...] = m_new


def _diff_finish(m_ref, l_ref, acc_ref, lam, subln, lam_init, tq):
    o = acc_ref[...] / l_ref[...]
    a = o[:tq] - lam * o[tq:]
    return _rms(a, subln) * (1.0 - lam_init)


def _attn_prompt_kernel(q_ref, k_ref, v_ref, lam_ref, sub_ref, o_ref, q2_ref, m_ref, l_ref, acc_ref,
                        *, tq, lam_init):
    i = pl.program_id(2)
    q2_ref[...] = _split_q(q_ref[...])
    m_ref[...] = jnp.full_like(m_ref, NEG_BIG)
    l_ref[...] = jnp.zeros_like(l_ref)
    acc_ref[...] = jnp.zeros_like(acc_ref)

    def scores(j):
        r0 = pl.multiple_of(j * tq, tq)
        k = k_ref[pl.ds(r0, tq), :]
        v = v_ref[pl.ds(r0, tq), :]
        s = lax.dot_general(q2_ref[...], k, (((1,), (1,)), ((), ())), preferred_element_type=F32)
        return s, v

    def body(j, carry):
        s, v = scores(j)
        _online_update(s, v, m_ref, l_ref, acc_ref)
        return carry

    lax.fori_loop(0, i, body, 0)

    s, v = scores(i)
    row = lax.broadcasted_iota(jnp.int32, s.shape, 0) % tq
    col = lax.broadcasted_iota(jnp.int32, s.shape, 1)
    s = jnp.where(col < (row // CHUNK + 1) * CHUNK, s, NEG_BIG)
    _online_update(s, v, m_ref, l_ref, acc_ref)

    lam = _lambda(lam_ref, lam_init)
    o_ref[...] = _diff_finish(m_ref, l_ref, acc_ref, lam, sub_ref[...], lam_init, tq).astype(BF16)


def attn_prompt(qb, kb, vb, lamv, subln, *, batch, seq, tq, lam_init):
    n = batch * seq
    nq = seq // tq
    return pl.pallas_call(
        functools.partial(_attn_prompt_kernel, tq=tq, lam_init=lam_init),
        grid=(batch, DIFF_HEADS, nq),
        in_specs=[
            pl.BlockSpec((tq, LANES), lambda b, h, i: (b * nq + i, h)),
            pl.BlockSpec((seq, LANES), lambda b, h, i: (b, h)),
            pl.BlockSpec((seq, LANES), lambda b, h, i: (b, h)),
            pl.BlockSpec((8, LANES), lambda b, h, i: (0, 0)),
            pl.BlockSpec((1, LANES), lambda b, h, i: (0, 0)),
        ],
        out_specs=pl.BlockSpec((tq, LANES), lambda b, h, i: (b * nq + i, h)),
        out_shape=jax.ShapeDtypeStruct((n, DIFF_HEADS * DIFF_DV), BF16),
        scratch_shapes=[pltpu.VMEM((2 * tq, LANES), BF16), pltpu.VMEM((2 * tq, 1), F32),
                        pltpu.VMEM((2 * tq, 1), F32), pltpu.VMEM((2 * tq, LANES), F32)],
        compiler_params=_cparams(("parallel", "parallel", "arbitrary")),
        name="attn_prompt",
    )(qb, kb, vb, lamv, subln)


def _attn_sample_kernel(q_ref, ck_ref, cv_ref, kn_ref, vn_ref, lam_ref, sub_ref, o_ref,
                        q2_ref, m_ref, l_ref, acc_ref, *, tq, past, lam_init):
    j = pl.program_id(1)
    last = pl.num_programs(1) - 1

    @pl.when(j == 0)
    def _():
        for h in range(DIFF_HEADS):
            q2_ref[h] = _split_q(q_ref[:, h * LANES:(h + 1) * LANES])
        m_ref[...] = jnp.full_like(m_ref, NEG_BIG)
        l_ref[...] = jnp.zeros_like(l_ref)
        acc_ref[...] = jnp.zeros_like(acc_ref)

    @pl.when(j < last)
    def _():
        for h in range(DIFF_HEADS):
            k = ck_ref[0, :, h * LANES:(h + 1) * LANES].astype(BF16)
            v = cv_ref[0, :, h * LANES:(h + 1) * LANES].astype(BF16)
            s = lax.dot_general(q2_ref[h], k, (((1,), (1,)), ((), ())), preferred_element_type=F32)
            _online_update(s, v, m_ref.at[h], l_ref.at[h], acc_ref.at[h])

    @pl.when(j == last)
    def _():
        lam = _lambda(lam_ref, lam_init)
        for h in range(DIFF_HEADS):
            k = kn_ref[:, h * LANES:(h + 1) * LANES]
            v = vn_ref[:, h * LANES:(h + 1) * LANES]
            s = lax.dot_general(q2_ref[h], k, (((1,), (1,)), ((), ())), preferred_element_type=F32)
            qpos = past + lax.broadcasted_iota(jnp.int32, s.shape, 0) % tq
            kpos = past + lax.broadcasted_iota(jnp.int32, s.shape, 1)
            s = jnp.where(kpos < (qpos // CHUNK + 1) * CHUNK, s, NEG_BIG)
            _online_update(s, v, m_ref.at[h], l_ref.at[h], acc_ref.at[h])
            o_ref[:, h * LANES:(h + 1) * LANES] = _diff_finish(
                m_ref.at[h], l_ref.at[h], acc_ref.at[h], lam, sub_ref[...], lam_init, tq).astype(BF16)


def attn_sample(qb, cache_k, cache_v, kb, vb, lamv, subln, *, batch, seq, tk, lam_init):
    past = cache_k.shape[1]
    d = DIFF_HEADS * DIFF_DV
    nk = past // tk
    cache_spec = pl.BlockSpec((1, tk, d), lambda b, j: (b, jnp.minimum(j, nk - 1), 0))
    row_spec = pl.BlockSpec((seq, d), lambda b, j: (b, 0))
    return pl.pallas_call(
        functools.partial(_attn_sample_kernel, tq=seq, past=past, lam_init=lam_init),
        grid=(batch, nk + 1),
        in_specs=[row_spec, cache_spec, cache_spec, row_spec, row_spec,
                  pl.BlockSpec((8, LANES), lambda b, j: (0, 0)),
                  pl.BlockSpec((1, LANES), lambda b, j: (0, 0))],
        out_specs=row_spec,
        out_shape=jax.ShapeDtypeStruct((batch * seq, d), BF16),
        scratch_shapes=[pltpu.VMEM((DIFF_HEADS, 2 * seq, LANES), BF16),
                        pltpu.VMEM((DIFF_HEADS, 2 * seq, 1), F32),
                        pltpu.VMEM((DIFF_HEADS, 2 * seq, 1), F32),
                        pltpu.VMEM((DIFF_HEADS, 2 * seq, LANES), F32)],
        compiler_params=_cparams(("parallel", "arbitrary")),
        name="attn_sample",
    )(qb, cache_k, cache_v, kb, vb, lamv, subln)


def _ret_rope_tables(pos):
    half = RET_DK // 2
    inv_freq = jnp.power(jnp.float32(RET_THETA), -jnp.arange(half, dtype=F32) / half)
    ang = pos.astype(F32)[:, None] * inv_freq[None, :]
    return jnp.cos(ang), jnp.sin(ang)


def _diff_rope_tables(pos):
    half = ROPE_DIM // 2
    inv_freq = jnp.power(jnp.float32(ROPE_THETA), -jnp.arange(half, dtype=F32) / half)
    ang = pos.astype(F32)[:, None] * inv_freq[None, :]
    cos, sin = jnp.cos(ang), jnp.sin(ang)
    t = pos.shape[0]
    pad = DIFF_DH - ROPE_DIM
    cos64 = jnp.concatenate([cos, cos, jnp.ones((t, pad), F32)], axis=1)
    sina64 = jnp.concatenate([-sin, jnp.zeros((t, DIFF_DH - half), F32)], axis=1)
    sinb64 = jnp.concatenate([jnp.zeros((t, half), F32), sin, jnp.zeros((t, pad), F32)], axis=1)
    rep = LANES // DIFF_DH
    return jnp.tile(cos64, (1, rep)), jnp.tile(sina64, (1, rep)), jnp.tile(sinb64, (1, rep))


def _prep_params(p):
    w = {}
    w['ret_w_in'] = p['ret_w_in'][0].astype(BF16)
    w['ret_w_out'] = p['ret_w_out'][0].astype(BF16)
    w['kv_w_k'] = p['kv_w_k'].astype(BF16)
    w['kv_w_v'] = p['kv_w_v'].astype(BF16)
    w['diff_w_q'] = p['diff_w_q'][0].astype(BF16)
    w['diff_w_o'] = p['diff_w_o'][0].astype(BF16)
    w['moe_w_gate'] = p['moe_w_gate'].astype(BF16)
    w['moe_w_up'] = p['moe_w_up'].astype(BF16)
    w['moe_w_down'] = p['moe_w_down'].astype(BF16)
    wr, br = [], []
    for layer in range(2):
        route = jnp.transpose(p['moe_w_route'][layer], (1, 0, 2)).reshape(D_MODEL, N_EXPERTS)
        cols = jnp.concatenate([route, p['moe_w_group'][layer]], axis=1)
        wr.append(jnp.pad(cols, ((0, 0), (0, ROUTE_LANES - cols.shape[1]))))
        bias = jnp.concatenate([p['moe_b_route'][layer].reshape(-1), p['moe_b_group'][layer]])
        br.append(jnp.pad(bias, (0, ROUTE_LANES - bias.shape[0]))[None, :])
    w['route_w'] = wr
    w['route_b'] = br
    lamv = jnp.concatenate([p['diff_lam_q1'][0][None], p['diff_lam_k1'][0][None],
                            p['diff_lam_q2'][0][None], p['diff_lam_k2'][0][None]], axis=0)
    w['lamv'] = jnp.pad(lamv, ((0, 4), (0, LANES - DIFF_DH)))
    return w


def _trunk(x, pos, r0, past_k, past_v, p, w, *, tm, ret_chunk, ret_tb, attn_tq=None, attn_tk=None):
    batch, seq, d = x.shape
    n = batch * seq
    xf = x.reshape(n, d)
    row = lambda v: v.reshape(1, -1)

    cos, sin = _ret_rope_tables(pos)
    if seq < tm:
        cos, sin = jnp.tile(cos, (tm // seq, 1)), jnp.tile(sin, (tm // seq, 1))
    proj = ret_inproj(xf, row(p['norm_mix'][0]), w['ret_w_in'], cos, sin, tm=tm)
    o_gated, r_new = retention(proj, r0, batch=batch, seq=seq, L=ret_chunk, tb=ret_tb)
    h1, u1, gate1 = outproj_route(o_gated, w['ret_w_out'], xf, row(p['norm_ffn'][0]),
                                  w['route_w'][0], w['route_b'][0], tm=tm)
    gn = jnp.stack([p['kv_norm'], p['norm_mix'][1]])
    h2, ukv, umix = moe(u1, gate1, w['moe_w_gate'][0], w['moe_w_up'][0], w['moe_w_down'][0], h1, gn,
                        tm=tm, emit_h=True, norm_dtypes=(BF16, BF16))

    cosf, sina, sinb = _diff_rope_tables(pos)
    if seq < tm:
        rep = (tm // seq, 1)
        cosf, sina, sinb = jnp.tile(cosf, rep), jnp.tile(sina, rep), jnp.tile(sinb, rep)
    k32, v32, kb, vb, qb = qkv_proj(ukv, umix, w['kv_w_k'], w['kv_w_v'], w['diff_w_q'],
                                    cosf, sina, sinb, tm=tm)
    lam_init = 0.8 - 0.6 * math.exp(-0.3 * 1)
    subln = row(p['diff_subln'][0])
    if past_k is None:
        attn = attn_prompt(qb, kb, vb, w['lamv'], subln, batch=batch, seq=seq, tq=attn_tq,
                           lam_init=lam_init)
    else:
        past = past_k.shape[1]
        attn = attn_sample(qb, past_k.reshape(batch, past, d), past_v.reshape(batch, past, d), kb, vb,
                           w['lamv'], subln, batch=batch, seq=seq, tk=attn_tk, lam_init=lam_init)
    h3, u3, gate3 = outproj_route(attn, w['diff_w_o'], h2, row(p['norm_ffn'][1]),
                                  w['route_w'][1], w['route_b'][1], tm=tm)
    (y,) = moe(u3, gate3, w['moe_w_gate'][1], w['moe_w_up'][1], w['moe_w_down'][1], h3,
               row(p['norm_final']), tm=tm, emit_h=False, norm_dtypes=(F32,))

    return (y.reshape(batch, seq, d), r_new[None],
            k32.reshape(batch, seq, DIFF_HEADS, 2, DIFF_DH), v32.reshape(batch, seq, DIFF_HEADS, DIFF_DV))


def kernel(x_prompt, x_sample, state_ret, cache_k, cache_v, norm_mix, norm_ffn, norm_final, ret_w_in, ret_w_out, kv_norm, kv_w_k, kv_w_v, diff_w_q, diff_lam_q1, diff_lam_k1, diff_lam_q2, diff_lam_k2, diff_subln, diff_w_o, moe_w_group, moe_b_group, moe_w_route, moe_b_route, moe_w_gate, moe_w_up, moe_w_down):
    p = {
        'norm_mix': norm_mix, 'norm_ffn': norm_ffn, 'norm_final': norm_final,
        'ret_w_in': ret_w_in, 'ret_w_out': ret_w_out,
        'kv_norm': kv_norm, 'kv_w_k': kv_w_k, 'kv_w_v': kv_w_v,
        'diff_w_q': diff_w_q, 'diff_lam_q1': diff_lam_q1, 'diff_lam_k1': diff_lam_k1,
        'diff_lam_q2': diff_lam_q2, 'diff_lam_k2': diff_lam_k2, 'diff_subln': diff_subln,
        'diff_w_o': diff_w_o,
        'moe_w_group': moe_w_group, 'moe_b_group': moe_b_group, 'moe_w_route': moe_w_route,
        'moe_b_route': moe_b_route, 'moe_w_gate': moe_w_gate, 'moe_w_up': moe_w_up,
        'moe_w_down': moe_w_down,
    }
    w = _prep_params(p)
    tp = x_prompt.shape[1]
    ts = x_sample.shape[1]
    past = cache_k.shape[1]
    y_p, r_p, k_p, v_p = _trunk(x_prompt, jnp.arange(tp), None, None, None, p, w,
                                tm=512, ret_chunk=128, ret_tb=512, attn_tq=512)
    y_s, r_s, k_s, v_s = _trunk(x_sample, past + jnp.arange(ts), state_ret[0], cache_k, cache_v, p, w,
                                tm=x_sample.shape[0] * ts, ret_chunk=ts, ret_tb=ts, attn_tk=512)
    return (y_p, y_s, r_p, k_p, v_p, r_s, k_s, v_s)
```

```python
import functools
import math

import numpy as np
import jax
import jax.numpy as jnp
from jax import lax
from jax.experimental import pallas as pl
from jax.experimental.pallas import tpu as pltpu

F32 = jnp.float32
BF16 = jnp.bfloat16

D_MODEL = 1024
CHUNK = 64
RET_HEADS = 4
RET_DK = 256
RET_DV = 512
RET_THETA = 10000.0
DIFF_HEADS = 8
DIFF_DH = 64
DIFF_DV = 128
ROPE_THETA = 500000.0
ROPE_DIM = 16
N_GROUPS = 4
EXPERTS_PER_GROUP = 4
N_EXPERTS = 16
D_EXPERT = 256
EPS = 1e-6

LANES = 128
VMEM_LIMIT = 56 * 1024 * 1024
ROUTE_LANES = 128
GROUP_LANE0 = N_EXPERTS
NEG_BIG = -1e30
QK_SCALE_LOG2 = DIFF_DH ** -0.5 * math.log2(math.e)


def _cparams(sem):
    return pltpu.CompilerParams(dimension_semantics=sem, vmem_limit_bytes=VMEM_LIMIT)


def _rms(x, g):
    ms = jnp.mean(x * x, axis=-1, keepdims=True)
    return x * lax.rsqrt(ms + EPS) * g


def _silu(x):
    return x * (1.0 / (1.0 + jnp.exp(-x)))


def _ret_inproj_kernel(x_ref, g_ref, w_ref, cos_ref, sin_ref, o_ref, xn_ref, *, tn):
    j = pl.program_id(1)
    n_q = (RET_HEADS * RET_DK) // tn
    n_qk = 2 * n_q
    n_qkv = n_qk + (RET_HEADS * RET_DV) // tn

    @pl.when(j == 0)
    def _():
        xn_ref[...] = _rms(x_ref[...], g_ref[...]).astype(BF16)

    y = jnp.dot(xn_ref[...], w_ref[...], preferred_element_type=F32)

    @pl.when(j < n_qk)
    def _():
        cos = cos_ref[...]
        sin = sin_ref[...]
        scale = jnp.where(j >= n_q, RET_DK ** -0.5, 1.0).astype(F32)
        half = RET_DK // 2
        for h in range(tn // RET_DK):
            x1 = y[:, h * RET_DK:h * RET_DK + half]
            x2 = y[:, h * RET_DK + half:(h + 1) * RET_DK]
            o_ref[:, h * RET_DK:h * RET_DK + half] = ((x1 * cos - x2 * sin) * scale).astype(BF16)
            o_ref[:, h * RET_DK + half:(h + 1) * RET_DK] = ((x2 * cos + x1 * sin) * scale).astype(BF16)

    @pl.when(jnp.logical_and(j >= n_qk, j < n_qkv))
    def _():
        o_ref[...] = y.astype(BF16)

    @pl.when(j >= n_qkv)
    def _():
        o_ref[...] = _silu(y).astype(BF16)


def ret_inproj(x, g, w, cos, sin, *, tm, tn=1024):
    n, d = x.shape
    m = w.shape[1]
    n_pos_tiles = cos.shape[0] // tm
    return pl.pallas_call(
        functools.partial(_ret_inproj_kernel, tn=tn),
        grid=(n // tm, m // tn),
        in_specs=[
            pl.BlockSpec((tm, d), lambda i, j: (i, 0)),
            pl.BlockSpec((1, d), lambda i, j: (0, 0)),
            pl.BlockSpec((d, tn), lambda i, j: (0, j)),
            pl.BlockSpec((tm, RET_DK // 2), lambda i, j: (i % n_pos_tiles, 0)),
            pl.BlockSpec((tm, RET_DK // 2), lambda i, j: (i % n_pos_tiles, 0)),
        ],
        out_specs=pl.BlockSpec((tm, tn), lambda i, j: (i, j)),
        out_shape=jax.ShapeDtypeStruct((n, m), BF16),
        scratch_shapes=[pltpu.VMEM((tm, d), BF16)],
        compiler_params=_cparams(("parallel", "arbitrary")),
        name="ret_inproj",
    )(x, g, w, cos, sin)


def _retention_kernel(*refs, L, n_chunks, has_r0):
    if has_r0:
        (q_ref, k_ref, v_ref, g_ref, dm_ref, xi_ref, zt_ref, r0_ref, o_ref, rout_ref, r_ref) = refs
    else:
        (q_ref, k_ref, v_ref, g_ref, dm_ref, xi_ref, zt_ref, o_ref, rout_ref, r_ref) = refs
    t = pl.program_id(1)

    @pl.when(t == 0)
    def _():
        if has_r0:
            r_ref[...] = r0_ref[0]
        else:
            r_ref[...] = jnp.zeros_like(r_ref)

    def chunk(c, carry):
        r0 = pl.multiple_of(c * L, L)
        for h in range(RET_HEADS):
            lg = math.log1p(-2.0 ** (-5.0 - h))
            q = q_ref[pl.ds(r0, L), h * RET_DK:(h + 1) * RET_DK]
            k = k_ref[pl.ds(r0, L), h * RET_DK:(h + 1) * RET_DK]
            v = v_ref[pl.ds(r0, L), h * RET_DV:(h + 1) * RET_DV]
            gate = g_ref[pl.ds(r0, L), h * RET_DV:(h + 1) * RET_DV]
            rh = r_ref[h]
            s = lax.dot_general(q, k, (((1,), (1,)), ((), ())), preferred_element_type=F32)
            s = s * dm_ref[h]
            intra = jnp.dot(s.astype(BF16), v, preferred_element_type=F32)
            inter = jnp.dot(q, rh.astype(BF16), preferred_element_type=F32) * xi_ref[h]
            o = intra + inter
            kz = (k.astype(F32) * zt_ref[h]).astype(BF16)
            upd = lax.dot_general(kz, v, (((0,), (0,)), ((), ())), preferred_element_type=F32)
            r_ref[h] = math.exp(lg * L) * rh + upd
            on = o * lax.rsqrt(jnp.mean(o * o, axis=-1, keepdims=True) + EPS)
            o_ref[pl.ds(r0, L), h * RET_DV:(h + 1) * RET_DV] = (on * gate.astype(F32)).astype(BF16)
        return carry

    lax.fori_loop(0, n_chunks, chunk, 0)

    @pl.when(t == pl.num_programs(1) - 1)
    def _():
        rout_ref[0] = r_ref[...]


def _retention_tables(L):
    lg = np.log1p(-np.exp2(-5.0 - np.arange(RET_HEADS, dtype=np.float64)))
    i = np.arange(L, dtype=np.float64)
    diff = i[:, None] - i[None, :]
    dmask = np.where(diff >= 0, np.exp(lg[:, None, None] * np.maximum(diff, 0.0)), 0.0)
    xi = np.exp(lg[:, None] * (i[None, :] + 1.0))
    zeta = np.exp(lg[:, None] * (L - 1.0 - i[None, :]))
    xi = np.broadcast_to(xi[:, :, None], (RET_HEADS, L, RET_DV))
    zeta = np.broadcast_to(zeta[:, :, None], (RET_HEADS, L, RET_DK))
    return (jnp.asarray(dmask, F32), jnp.asarray(xi, F32), jnp.asarray(zeta, F32))


def retention(proj, r0, *, batch, seq, L, tb):
    n = batch * seq
    nt = seq // tb
    dmask, xi, zeta = _retention_tables(L)
    dqk = RET_HEADS * RET_DK
    dv = RET_HEADS * RET_DV
    has_r0 = r0 is not None
    const3 = lambda b, t: (0, 0, 0)
    in_specs = [
        pl.BlockSpec((tb, dqk), lambda b, t: (b * nt + t, 0)),
        pl.BlockSpec((tb, dqk), lambda b, t: (b * nt + t, 1)),
        pl.BlockSpec((tb, dv), lambda b, t: (b * nt + t, 1)),
        pl.BlockSpec((tb, dv), lambda b, t: (b * nt + t, 2)),
        pl.BlockSpec((RET_HEADS, L, L), const3),
        pl.BlockSpec((RET_HEADS, L, RET_DV), const3),
        pl.BlockSpec((RET_HEADS, L, RET_DK), const3),
    ]
    args = [proj, proj, proj, proj, dmask, xi, zeta]
    state_spec = pl.BlockSpec((1, RET_HEADS, RET_DK, RET_DV), lambda b, t: (b, 0, 0, 0))
    if has_r0:
        in_specs.append(state_spec)
        args.append(r0)
    return pl.pallas_call(
        functools.partial(_retention_kernel, L=L, n_chunks=tb // L, has_r0=has_r0),
        grid=(batch, nt),
        in_specs=in_specs,
        out_specs=[pl.BlockSpec((tb, dv), lambda b, t: (b * nt + t, 0)), state_spec],
        out_shape=[jax.ShapeDtypeStruct((n, dv), BF16),
                   jax.ShapeDtypeStruct((batch, RET_HEADS, RET_DK, RET_DV), F32)],
        scratch_shapes=[pltpu.VMEM((RET_HEADS, RET_DK, RET_DV), F32)],
        compiler_params=_cparams(("parallel", "arbitrary")),
        name="retention",
    )(*args)


def _route(u, wr, br):
    logits = jnp.dot(u, wr, preferred_element_type=F32, precision=lax.Precision.HIGHEST) + br
    lane_i = lax.broadcasted_iota(jnp.int32, logits.shape, 1)
    lane = lane_i.astype(F32)
    lane_grp = (lane_i // EXPERTS_PER_GROUP).astype(F32)
    big = jnp.float32(1 << 20)
    is_grp = jnp.logical_and(lane_i >= GROUP_LANE0, lane_i < GROUP_LANE0 + N_GROUPS)
    gl = jnp.where(is_grp, logits, -jnp.inf)
    gmax = jnp.max(gl, axis=-1, keepdims=True)
    gsum = jnp.sum(jnp.exp(gl - gmax), axis=-1, keepdims=True)
    p_top = 1.0 / gsum
    g_top = jnp.min(jnp.where(gl == gmax, lane, big), axis=-1, keepdims=True) - GROUP_LANE0
    sel = jnp.logical_and(lane_i < N_EXPERTS, lane_grp == g_top)
    el = jnp.where(sel, logits, -jnp.inf)
    emax = jnp.max(el, axis=-1, keepdims=True)
    ee = jnp.exp(el - emax)
    ep = ee / jnp.sum(ee, axis=-1, keepdims=True)
    epm = jnp.where(sel, ep, -1.0)
    m1 = jnp.max(epm, axis=-1, keepdims=True)
    i1 = jnp.min(jnp.where(epm == m1, lane, big), axis=-1, keepdims=True)
    epm2 = jnp.where(lane == i1, -1.0, epm)
    m2 = jnp.max(epm2, axis=-1, keepdims=True)
    i2 = jnp.min(jnp.where(epm2 == m2, lane, big), axis=-1, keepdims=True)
    denom = m1 + m2
    w1 = m1 / denom * p_top
    w2 = m2 / denom * p_top
    return jnp.where(lane == i1, w1, jnp.where(lane == i2, w2, 0.0))


def _outproj_kernel(a_ref, w_ref, res_ref, g_ref, wr_ref, br_ref, h_ref, u_ref, gate_ref):
    h = res_ref[...] + jnp.dot(a_ref[...], w_ref[...], preferred_element_type=F32)
    h_ref[...] = h
    u = _rms(h, g_ref[...])
    u_ref[...] = u.astype(BF16)
    gate_ref[...] = _route(u, wr_ref[...], br_ref[...])


def outproj_route(a, w, res, g, wr, br, *, tm):
    n, kdim = a.shape
    d = w.shape[1]
    return pl.pallas_call(
        _outproj_kernel,
        grid=(n // tm,),
        in_specs=[
            pl.BlockSpec((tm, kdim), lambda i: (i, 0)),
            pl.BlockSpec((kdim, d), lambda i: (0, 0)),
            pl.BlockSpec((tm, d), lambda i: (i, 0)),
            pl.BlockSpec((1, d), lambda i: (0, 0)),
            pl.BlockSpec((d, ROUTE_LANES), lambda i: (0, 0)),
            pl.BlockSpec((1, ROUTE_LANES), lambda i: (0, 0)),
        ],
        out_specs=[
            pl.BlockSpec((tm, d), lambda i: (i, 0)),
            pl.BlockSpec((tm, d), lambda i: (i, 0)),
            pl.BlockSpec((tm, ROUTE_LANES), lambda i: (i, 0)),
        ],
        out_shape=[jax.ShapeDtypeStruct((n, d), F32),
                   jax.ShapeDtypeStruct((n, d), BF16),
                   jax.ShapeDtypeStruct((n, ROUTE_LANES), F32)],
        compiler_params=_cparams(("parallel",)),
        name="outproj_route",
    )(a, w, res, g, wr, br)


def _moe_kernel(u_ref, gate_ref, wg_ref, wu_ref, wd_ref, res_ref, gn_ref, *out_and_scratch, n_norms, emit_h):
    acc_ref = out_and_scratch[-1]
    outs = out_and_scratch[:-1]
    grp = pl.program_id(1)

    @pl.when(grp == 0)
    def _():
        acc_ref[...] = jnp.zeros_like(acc_ref)

    u = u_ref[...]
    gate = gate_ref[...]
    lane = lax.broadcasted_iota(jnp.int32, gate.shape, 1)
    contrib = None
    for e in range(EXPERTS_PER_GROUP):
        ge = jnp.sum(jnp.where(lane == grp * EXPERTS_PER_GROUP + e, gate, 0.0), axis=-1, keepdims=True)
        hg = jnp.dot(u, wg_ref[e], preferred_element_type=F32)
        hu = jnp.dot(u, wu_ref[e], preferred_element_type=F32)
        hid = (_silu(hg) * hu * ge).astype(BF16)
        d = jnp.dot(hid, wd_ref[e], preferred_element_type=F32)
        contrib = d if contrib is None else contrib + d
    acc_ref[...] += contrib

    @pl.when(grp == pl.num_programs(1) - 1)
    def _():
        h = res_ref[...] + acc_ref[...]
        k = 0
        if emit_h:
            outs[0][...] = h
            k = 1
        for t in range(n_norms):
            o = outs[k + t]
            o[...] = _rms(h, gn_ref[t:t + 1, :]).astype(o.dtype)


def moe(u, gate, wg, wu, wd, res, gnorms, *, tm, emit_h, norm_dtypes):
    n, d = u.shape
    n_norms = len(norm_dtypes)
    tok = lambda i, g: (i, 0)
    out_specs = []
    out_shape = []
    if emit_h:
        out_specs.append(pl.BlockSpec((tm, d), tok))
        out_shape.append(jax.ShapeDtypeStruct((n, d), F32))
    for dt in norm_dtypes:
        out_specs.append(pl.BlockSpec((tm, d), tok))
        out_shape.append(jax.ShapeDtypeStruct((n, d), dt))
    return pl.pallas_call(
        functools.partial(_moe_kernel, n_norms=n_norms, emit_h=emit_h),
        grid=(n // tm, N_GROUPS),
        in_specs=[
            pl.BlockSpec((tm, d), tok),
            pl.BlockSpec((tm, ROUTE_LANES), tok),
            pl.BlockSpec((EXPERTS_PER_GROUP, d, D_EXPERT), lambda i, g: (g, 0, 0)),
            pl.BlockSpec((EXPERTS_PER_GROUP, d, D_EXPERT), lambda i, g: (g, 0, 0)),
            pl.BlockSpec((EXPERTS_PER_GROUP, D_EXPERT, d), lambda i, g: (g, 0, 0)),
            pl.BlockSpec((tm, d), tok),
            pl.BlockSpec((n_norms, d), lambda i, g: (0, 0)),
        ],
        out_specs=out_specs,
        out_shape=out_shape,
        scratch_shapes=[pltpu.VMEM((tm, d), F32)],
        compiler_params=_cparams(("parallel", "arbitrary")),
        name="moe",
    )(u, gate, wg, wu, wd, res, gnorms)


def _rope_cols(y, cosf, sina, sinb):
    blocks = []
    for c in range(y.shape[1] // LANES):
        xb = y[:, c * LANES:(c + 1) * LANES]
        blocks.append(xb * cosf + pltpu.roll(xb, LANES - ROPE_DIM // 2, 1) * sina
                      + pltpu.roll(xb, ROPE_DIM // 2, 1) * sinb)
    return blocks


def _rope_rows(kt, cost, sint):
    half = ROPE_DIM // 2
    parts = []
    for g in range(kt.shape[0] // DIFF_DH):
        b0 = g * DIFF_DH
        x1 = kt[b0:b0 + half]
        x2 = kt[b0 + half:b0 + ROPE_DIM]
        parts += [x1 * cost - x2 * sint, x2 * cost + x1 * sint, kt[b0 + ROPE_DIM:b0 + DIFF_DH]]
    return jnp.concatenate(parts, axis=0)


def _qkv_kernel(*refs, tm, k_feature_major):
    if k_feature_major:
        (ukv_ref, umix_ref, wk_ref, wv_ref, wq_ref, cos_ref, sa_ref, sb_ref, cost_ref, sint_ref,
         k32_ref, v32_ref, kb_ref, vb_ref, qb_ref) = refs
    else:
        (ukv_ref, umix_ref, wk_ref, wv_ref, wq_ref, cos_ref, sa_ref, sb_ref,
         k32_ref, v32_ref, kb_ref, vb_ref, qb_ref) = refs
    cosf = cos_ref[...]
    sina = sa_ref[...]
    sinb = sb_ref[...]
    ukv = ukv_ref[...]
    if k_feature_major:
        kt = lax.dot_general(wk_ref[...], ukv, (((1,), (1,)), ((), ())), preferred_element_type=F32)
        kt = _rope_rows(kt, cost_ref[...], sint_ref[...])
        k32_ref[0] = kt
        kb_ref[0, 0] = kt.astype(BF16)
    else:
        k = jnp.dot(ukv, wk_ref[...], preferred_element_type=F32)
        for c, blk in enumerate(_rope_cols(k, cosf, sina, sinb)):
            k32_ref[:, c * LANES:(c + 1) * LANES] = blk
            kb_ref[:, c * LANES:(c + 1) * LANES] = blk.astype(BF16)
    v = jnp.dot(ukv, wv_ref[...], preferred_element_type=F32)
    vb_ref[...] = v.astype(BF16)
    for h in range(DIFF_HEADS):
        v32_ref[pl.ds(h, tm, stride=DIFF_HEADS), :] = v[:, h * DIFF_DV:(h + 1) * DIFF_DV]
    q = jnp.dot(umix_ref[...], wq_ref[...], preferred_element_type=F32)
    for c, blk in enumerate(_rope_cols(q, cosf, sina, sinb)):
        qb_ref[:, c * LANES:(c + 1) * LANES] = (blk * QK_SCALE_LOG2).astype(BF16)


def qkv_proj(ukv, umix, wk, wv, wq, cosf, sina, sinb, cost=None, sint=None, *, tm, batch, seq):
    n, d = ukv.shape
    k_feature_major = cost is not None
    nt = max(seq // tm, 1)
    n_pos_tiles = cosf.shape[0] // tm
    tok = lambda i: (i, 0)
    wspec = pl.BlockSpec((d, d), lambda i: (0, 0))
    tspec = pl.BlockSpec((tm, LANES), lambda i: (i % n_pos_tiles, 0))
    in_specs = [pl.BlockSpec((tm, d), tok), pl.BlockSpec((tm, d), tok), wspec, wspec, wspec,
                tspec, tspec, tspec]
    args = [ukv, umix, wk, wv, wq, cosf, sina, sinb]
    rows_f32 = (pl.BlockSpec((tm, d), tok), jax.ShapeDtypeStruct((n, d), F32))
    rows_bf16 = (pl.BlockSpec((tm, d), tok), jax.ShapeDtypeStruct((n, d), BF16))
    v32 = (pl.BlockSpec((tm * DIFF_HEADS, DIFF_DV), tok), jax.ShapeDtypeStruct((n * DIFF_HEADS, DIFF_DV), F32))
    if k_feature_major:
        half = ROPE_DIM // 2
        rspec = pl.BlockSpec((half, tm), lambda i: (0, i % nt))
        in_specs += [rspec, rspec]
        args += [cost, sint]
        k32 = (pl.BlockSpec((1, d, tm), lambda i: (i // nt, 0, i % nt)),
               jax.ShapeDtypeStruct((batch, d, seq), F32))
        kb = (pl.BlockSpec((1, 1, d, tm), lambda i: (i // nt, i % nt, 0, 0)),
              jax.ShapeDtypeStruct((batch, nt, d, tm), BF16))
    else:
        k32, kb = rows_f32, rows_bf16
    outs = [k32, v32, kb, rows_bf16, rows_bf16]
    return pl.pallas_call(
        functools.partial(_qkv_kernel, tm=tm, k_feature_major=k_feature_major),
        grid=(n // tm,),
        in_specs=in_specs,
        out_specs=[o[0] for o in outs],
        out_shape=[o[1] for o in outs],
        compiler_params=_cparams(("parallel",)),
        name="qkv_proj",
    )(*args)


def _lambda(lam_ref, lam_init):
    lv = lam_ref[...]
    a = jnp.sum(lv[0:1, :] * lv[1:2, :], axis=-1, keepdims=True)
    b = jnp.sum(lv[2:3, :] * lv[3:4, :], axis=-1, keepdims=True)
    return jnp.exp(a) - jnp.exp(b) + lam_init


def _split_q(q):
    qf = q.astype(F32)
    lane = lax.broadcasted_iota(jnp.int32, qf.shape, 1)
    q0 = jnp.where(lane < DIFF_DH, qf, 0.0).astype(BF16)
    q1 = jnp.where(lane >= DIFF_DH, qf, 0.0).astype(BF16)
    return jnp.concatenate([q0, q1], axis=0)


def _online_update(s, v, m_ref, l_ref, acc_ref):
    m_old = m_ref[...]
    if s.shape[1] < LANES:
        m_new = jnp.maximum(m_old, jnp.max(s, axis=-1, keepdims=True))
        alpha = jnp.exp2(m_old - m_new)
        p = jnp.exp2(s - m_new[:, :s.shape[1]])
        lane = lax.broadcasted_iota(jnp.int32, m_old.shape, 1)
        lsum = jnp.where(lane == 0, jnp.sum(p, axis=-1, keepdims=True), 0.0)
        l_ref[...] = alpha * l_ref[...] + lsum
        acc_ref[...] = alpha * acc_ref[...] + jnp.dot(p.astype(BF16), v, preferred_element_type=F32)
        m_ref[...] = m_new
        return
    ncol = s.shape[1] // LANES
    cols = [s[:, c * LANES:(c + 1) * LANES] for c in range(ncol)]
    mx = cols[0]
    for c in range(1, ncol):
        mx = jnp.maximum(mx, cols[c])
    m_new = jnp.maximum(m_old, jnp.max(mx, axis=-1, keepdims=True))
    alpha = jnp.exp2(m_old - m_new)
    ps = [jnp.exp2(col - m_new) for col in cols]
    lsum = ps[0]
    for c in range(1, ncol):
        lsum = lsum + ps[c]
    p = jnp.concatenate([pc.astype(BF16) for pc in ps], axis=1) if ncol > 1 else ps[0].astype(BF16)
    l_ref[...] = alpha * l_ref[...] + lsum
    acc_ref[...] = alpha * acc_ref[...] + jnp.dot(p, v, preferred_element_type=F32)
    m_ref[...] = m_new


def _diff_finish(l_ref, acc_ref, lam, subln, lam_init, tq):
    o = acc_ref[...] / jnp.sum(l_ref[...], axis=-1, keepdims=True)
    a = o[:tq] - lam * o[tq:]
    return _rms(a, subln) * (1.0 - lam_init)


def _attn_prompt_kernel(q_ref, k_ref, v_ref, lam_ref, sub_ref, o_ref, q2_ref, m_ref, l_ref, acc_ref,
                        *, tq, rc, lam_init):
    i = pl.program_id(2)
    q2_ref[...] = _split_q(q_ref[...])
    m_ref[...] = jnp.full_like(m_ref, NEG_BIG)
    l_ref[...] = jnp.zeros_like(l_ref)
    acc_ref[...] = jnp.zeros_like(acc_ref)

    def step(j, masked):
        r0 = pl.multiple_of(j * tq, tq)
        kt = k_ref[0, j]
        v = v_ref[pl.ds(r0, tq), :]
        for c in range(2 * tq // rc):
            rows = slice(c * rc, (c + 1) * rc)
            s = jnp.dot(q2_ref[rows, :], kt, preferred_element_type=F32)
            if masked:
                row = (c * rc + lax.broadcasted_iota(jnp.int32, s.shape, 0)) % tq
                col = lax.broadcasted_iota(jnp.int32, s.shape, 1)
                s = jnp.where(col < (row // CHUNK + 1) * CHUNK, s, NEG_BIG)
            _online_update(s, v, m_ref.at[rows, :], l_ref.at[rows, :], acc_ref.at[rows, :])

    def body(j, carry):
        step(j, False)
        return carry

    lax.fori_loop(0, i, body, 0)
    step(i, True)

    lam = _lambda(lam_ref, lam_init)
    o_ref[...] = _diff_finish(l_ref, acc_ref, lam, sub_ref[...], lam_init, tq).astype(BF16)


def attn_prompt(qb, kb, vb, lamv, subln, *, batch, seq, tq, lam_init, rc=1024):
    n = batch * seq
    nq = seq // tq
    rc = min(rc, 2 * tq)
    return pl.pallas_call(
        functools.partial(_attn_prompt_kernel, tq=tq, rc=rc, lam_init=lam_init),
        grid=(batch, DIFF_HEADS, nq),
        in_specs=[
            pl.BlockSpec((tq, LANES), lambda b, h, i: (b * nq + i, h)),
            pl.BlockSpec((1, nq, LANES, tq), lambda b, h, i: (b, 0, h, 0)),
            pl.BlockSpec((seq, LANES), lambda b, h, i: (b, h)),
            pl.BlockSpec((8, LANES), lambda b, h, i: (0, 0)),
            pl.BlockSpec((1, LANES), lambda b, h, i: (0, 0)),
        ],
        out_specs=pl.BlockSpec((tq, LANES), lambda b, h, i: (b * nq + i, h)),
        out_shape=jax.ShapeDtypeStruct((n, DIFF_HEADS * DIFF_DV), BF16),
        scratch_shapes=[pltpu.VMEM((2 * tq, LANES), BF16), pltpu.VMEM((2 * tq, LANES), F32),
                        pltpu.VMEM((2 * tq, LANES), F32), pltpu.VMEM((2 * tq, LANES), F32)],
        compiler_params=_cparams(("parallel", "parallel", "arbitrary")),
        name="attn_prompt",
    )(qb, kb, vb, lamv, subln)


def _attn_sample_kernel(q_ref, ck_ref, cv_ref, kn_ref, vn_ref, lam_ref, sub_ref, o_ref,
                        q2_ref, m_ref, l_ref, acc_ref, *, tq, past, lam_init):
    j = pl.program_id(1)
    last = pl.num_programs(1) - 1

    @pl.when(j == 0)
    def _():
        for h in range(DIFF_HEADS):
            q2_ref[h] = _split_q(q_ref[:, h * LANES:(h + 1) * LANES])
        m_ref[...] = jnp.full_like(m_ref, NEG_BIG)
        l_ref[...] = jnp.zeros_like(l_ref)
        acc_ref[...] = jnp.zeros_like(acc_ref)

    @pl.when(j < last)
    def _():
        tk = ck_ref.shape[-1]
        for h in range(DIFF_HEADS):
            kt = ck_ref[0, h].astype(BF16)
            v = cv_ref[pl.ds(h, tk, stride=DIFF_HEADS), :].astype(BF16)
            s = jnp.dot(q2_ref[h], kt, preferred_element_type=F32)
            _online_update(s, v, m_ref.at[h], l_ref.at[h], acc_ref.at[h])

    @pl.when(j == last)
    def _():
        lam = _lambda(lam_ref, lam_init)
        for h in range(DIFF_HEADS):
            k = kn_ref[:, h * LANES:(h + 1) * LANES]
            v = vn_ref[:, h * LANES:(h + 1) * LANES]
            s = lax.dot_general(q2_ref[h], k, (((1,), (1,)), ((), ())), preferred_element_type=F32)
            qpos = past + lax.broadcasted_iota(jnp.int32, s.shape, 0) % tq
            kpos = past + lax.broadcasted_iota(jnp.int32, s.shape, 1)
            s = jnp.where(kpos < (qpos // CHUNK + 1) * CHUNK, s, NEG_BIG)
            _online_update(s, v, m_ref.at[h], l_ref.at[h], acc_ref.at[h])
            o_ref[:, h * LANES:(h + 1) * LANES] = _diff_finish(
                l_ref.at[h], acc_ref.at[h], lam, sub_ref[...], lam_init, tq).astype(BF16)


def attn_sample(qb, cache_k, cache_v, kb, vb, lamv, subln, *, batch, seq, tk, lam_init):
    past = cache_k.shape[-1]
    d = DIFF_HEADS * DIFF_DV
    nk = past // tk
    ck_spec = pl.BlockSpec((1, DIFF_HEADS, LANES, tk), lambda b, j: (b, 0, 0, jnp.minimum(j, nk - 1)))
    cv_spec = pl.BlockSpec((tk * DIFF_HEADS, DIFF_DV), lambda b, j: (b * nk + jnp.minimum(j, nk - 1), 0))
    row_spec = pl.BlockSpec((seq, d), lambda b, j: (b, 0))
    return pl.pallas_call(
        functools.partial(_attn_sample_kernel, tq=seq, past=past, lam_init=lam_init),
        grid=(batch, nk + 1),
        in_specs=[row_spec, ck_spec, cv_spec, row_spec, row_spec,
                  pl.BlockSpec((8, LANES), lambda b, j: (0, 0)),
                  pl.BlockSpec((1, LANES), lambda b, j: (0, 0))],
        out_specs=row_spec,
        out_shape=jax.ShapeDtypeStruct((batch * seq, d), BF16),
        scratch_shapes=[pltpu.VMEM((DIFF_HEADS, 2 * seq, LANES), BF16),
                        pltpu.VMEM((DIFF_HEADS, 2 * seq, LANES), F32),
                        pltpu.VMEM((DIFF_HEADS, 2 * seq, LANES), F32),
                        pltpu.VMEM((DIFF_HEADS, 2 * seq, LANES), F32)],
        compiler_params=_cparams(("parallel", "arbitrary")),
        name="attn_sample",
    )(qb, cache_k, cache_v, kb, vb, lamv, subln)


def _ret_rope_tables(pos):
    half = RET_DK // 2
    inv_freq = jnp.power(jnp.float32(RET_THETA), -jnp.arange(half, dtype=F32) / half)
    ang = pos.astype(F32)[:, None] * inv_freq[None, :]
    return jnp.cos(ang), jnp.sin(ang)


def _diff_rope_tables(pos):
    half = ROPE_DIM // 2
    inv_freq = jnp.power(jnp.float32(ROPE_THETA), -jnp.arange(half, dtype=F32) / half)
    ang = pos.astype(F32)[:, None] * inv_freq[None, :]
    cos, sin = jnp.cos(ang), jnp.sin(ang)
    t = pos.shape[0]
    pad = DIFF_DH - ROPE_DIM
    cos64 = jnp.concatenate([cos, cos, jnp.ones((t, pad), F32)], axis=1)
    sina64 = jnp.concatenate([-sin, jnp.zeros((t, DIFF_DH - half), F32)], axis=1)
    sinb64 = jnp.concatenate([jnp.zeros((t, half), F32), sin, jnp.zeros((t, pad), F32)], axis=1)
    rep = LANES // DIFF_DH
    return (jnp.tile(cos64, (1, rep)), jnp.tile(sina64, (1, rep)), jnp.tile(sinb64, (1, rep)),
            cos.T, sin.T)


def _prep_params(p):
    w = {}
    w['ret_w_in'] = p['ret_w_in'][0].astype(BF16)
    w['ret_w_out'] = p['ret_w_out'][0].astype(BF16)
    w['kv_w_k'] = p['kv_w_k'].astype(BF16)
    w['kv_w_k_t'] = p['kv_w_k'].T.astype(BF16)
    w['kv_w_v'] = p['kv_w_v'].astype(BF16)
    w['diff_w_q'] = p['diff_w_q'][0].astype(BF16)
    w['diff_w_o'] = p['diff_w_o'][0].astype(BF16)
    w['moe_w_gate'] = p['moe_w_gate'].astype(BF16)
    w['moe_w_up'] = p['moe_w_up'].astype(BF16)
    w['moe_w_down'] = p['moe_w_down'].astype(BF16)
    wr, br = [], []
    for layer in range(2):
        route = jnp.transpose(p['moe_w_route'][layer], (1, 0, 2)).reshape(D_MODEL, N_EXPERTS)
        cols = jnp.concatenate([route, p['moe_w_group'][layer]], axis=1)
        wr.append(jnp.pad(cols, ((0, 0), (0, ROUTE_LANES - cols.shape[1]))))
        bias = jnp.concatenate([p['moe_b_route'][layer].reshape(-1), p['moe_b_group'][layer]])
        br.append(jnp.pad(bias, (0, ROUTE_LANES - bias.shape[0]))[None, :])
    w['route_w'] = wr
    w['route_b'] = br
    lamv = jnp.concatenate([p['diff_lam_q1'][0][None], p['diff_lam_k1'][0][None],
                            p['diff_lam_q2'][0][None], p['diff_lam_k2'][0][None]], axis=0)
    w['lamv'] = jnp.pad(lamv, ((0, 4), (0, LANES - DIFF_DH)))
    return w


def _trunk(x, pos, r0, past_k, past_v, p, w, *, tm, ret_chunk, ret_tb, attn_tq=None, attn_tk=None):
    batch, seq, d = x.shape
    n = batch * seq
    xf = x.reshape(n, d)
    row = lambda v: v.reshape(1, -1)

    cos, sin = _ret_rope_tables(pos)
    if seq < tm:
        cos, sin = jnp.tile(cos, (tm // seq, 1)), jnp.tile(sin, (tm // seq, 1))
    proj = ret_inproj(xf, row(p['norm_mix'][0]), w['ret_w_in'], cos, sin, tm=tm)
    o_gated, r_new = retention(proj, r0, batch=batch, seq=seq, L=ret_chunk, tb=ret_tb)
    h1, u1, gate1 = outproj_route(o_gated, w['ret_w_out'], xf, row(p['norm_ffn'][0]),
                                  w['route_w'][0], w['route_b'][0], tm=tm)
    gn = jnp.stack([p['kv_norm'], p['norm_mix'][1]])
    h2, ukv, umix = moe(u1, gate1, w['moe_w_gate'][0], w['moe_w_up'][0], w['moe_w_down'][0], h1, gn,
                        tm=tm, emit_h=True, norm_dtypes=(BF16, BF16))

    cosf, sina, sinb, cost, sint = _diff_rope_tables(pos)
    if seq < tm:
        rep = (tm // seq, 1)
        cosf, sina, sinb = jnp.tile(cosf, rep), jnp.tile(sina, rep), jnp.tile(sinb, rep)
    lam_init = 0.8 - 0.6 * math.exp(-0.3 * 1)
    subln = row(p['diff_subln'][0])
    if past_k is None:
        assert attn_tq == tm
        k32, v32, kb, vb, qb = qkv_proj(ukv, umix, w['kv_w_k_t'], w['kv_w_v'], w['diff_w_q'],
                                        cosf, sina, sinb, cost, sint, tm=tm, batch=batch, seq=seq)
        attn = attn_prompt(qb, kb, vb, w['lamv'], subln, batch=batch, seq=seq, tq=attn_tq,
                           lam_init=lam_init)
        k_out = jnp.transpose(k32.reshape(batch, DIFF_HEADS, 2, DIFF_DH, seq), (0, 4, 1, 2, 3))
    else:
        past = past_k.shape[1]
        k32, v32, kb, vb, qb = qkv_proj(ukv, umix, w['kv_w_k'], w['kv_w_v'], w['diff_w_q'],
                                        cosf, sina, sinb, tm=tm, batch=batch, seq=seq)
        ck = jnp.transpose(past_k, (0, 2, 3, 4, 1)).reshape(batch, DIFF_HEADS, 2 * DIFF_DH, past)
        cv = past_v.reshape(batch * past * DIFF_HEADS, DIFF_DV)
        attn = attn_sample(qb, ck, cv, kb, vb, w['lamv'], subln, batch=batch, seq=seq, tk=attn_tk,
                           lam_init=lam_init)
        k_out = k32.reshape(batch, seq, DIFF_HEADS, 2, DIFF_DH)
    h3, u3, gate3 = outproj_route(attn, w['diff_w_o'], h2, row(p['norm_ffn'][1]),
                                  w['route_w'][1], w['route_b'][1], tm=tm)
    (y,) = moe(u3, gate3, w['moe_w_gate'][1], w['moe_w_up'][1], w['moe_w_down'][1], h3,
               row(p['norm_final']), tm=tm, emit_h=False, norm_dtypes=(F32,))

    return (y.reshape(batch, seq, d), r_new[None], k_out, v32.reshape(batch, seq, DIFF_HEADS, DIFF_DV))


def kernel(x_prompt, x_sample, state_ret, cache_k, cache_v, norm_mix, norm_ffn, norm_final, ret_w_in, ret_w_out, kv_norm, kv_w_k, kv_w_v, diff_w_q, diff_lam_q1, diff_lam_k1, diff_lam_q2, diff_lam_k2, diff_subln, diff_w_o, moe_w_group, moe_b_group, moe_w_route, moe_b_route, moe_w_gate, moe_w_up, moe_w_down):
    p = {
        'norm_mix': norm_mix, 'norm_ffn': norm_ffn, 'norm_final': norm_final,
        'ret_w_in': ret_w_in, 'ret_w_out': ret_w_out,
        'kv_norm': kv_norm, 'kv_w_k': kv_w_k, 'kv_w_v': kv_w_v,
        'diff_w_q': diff_w_q, 'diff_lam_q1': diff_lam_q1, 'diff_lam_k1': diff_lam_k1,
        'diff_lam_q2': diff_lam_q2, 'diff_lam_k2': diff_lam_k2, 'diff_subln': diff_subln,
        'diff_w_o': diff_w_o,
        'moe_w_group': moe_w_group, 'moe_b_group': moe_b_group, 'moe_w_route': moe_w_route,
        'moe_b_route': moe_b_route, 'moe_w_gate': moe_w_gate, 'moe_w_up': moe_w_up,
        'moe_w_down': moe_w_down,
    }
    w = _prep_params(p)
    tp = x_prompt.shape[1]
    ts = x_sample.shape[1]
    past = cache_k.shape[1]
    y_p, r_p, k_p, v_p = _trunk(x_prompt, jnp.arange(tp), None, None, None, p, w,
                                tm=512, ret_chunk=128, ret_tb=512, attn_tq=512)
    y_s, r_s, k_s, v_s = _trunk(x_sample, past + jnp.arange(ts), state_ret[0], cache_k, cache_v, p, w,
                                tm=x_sample.shape[0] * ts, ret_chunk=ts, ret_tb=ts, attn_tk=512)
    return (y_p, y_s, r_p, k_p, v_p, r_s, k_s, v_s)
```

```python
import functools
import math

import numpy as np
import jax
import jax.numpy as jnp
from jax import lax
from jax.experimental import pallas as pl
from jax.experimental.pallas import tpu as pltpu

F32 = jnp.float32
BF16 = jnp.bfloat16

D_MODEL = 1024
CHUNK = 64
RET_HEADS = 4
RET_DK = 256
RET_DV = 512
RET_THETA = 10000.0
DIFF_HEADS = 8
DIFF_DH = 64
DIFF_DV = 128
ROPE_THETA = 500000.0
ROPE_DIM = 16
N_GROUPS = 4
EXPERTS_PER_GROUP = 4
N_EXPERTS = 16
D_EXPERT = 256
EPS = 1e-6

LANES = 128
VMEM_LIMIT = 56 * 1024 * 1024
ROUTE_LANES = 128
GROUP_LANE0 = N_EXPERTS
NEG_BIG = -1e30
QK_SCALE_LOG2 = DIFF_DH ** -0.5 * math.log2(math.e)


def _cparams(sem):
    return pltpu.CompilerParams(dimension_semantics=sem, vmem_limit_bytes=VMEM_LIMIT)


def _rms(x, g):
    ms = jnp.mean(x * x, axis=-1, keepdims=True)
    return x * lax.rsqrt(ms + EPS) * g


def _silu(x):
    return x * (1.0 / (1.0 + jnp.exp(-x)))


def _ret_inproj_kernel(x_ref, g_ref, w_ref, cos_ref, sin_ref, o_ref, xn_ref, *, tn):
    j = pl.program_id(1)
    n_q = (RET_HEADS * RET_DK) // tn
    n_qk = 2 * n_q
    n_qkv = n_qk + (RET_HEADS * RET_DV) // tn

    @pl.when(j == 0)
    def _():
        xn_ref[...] = _rms(x_ref[...], g_ref[...]).astype(BF16)

    def proj():
        return jnp.dot(xn_ref[...], w_ref[...], preferred_element_type=F32)

    @pl.when(j < n_qk)
    def _():
        y = proj()
        cos = cos_ref[...]
        sin = sin_ref[...]
        scale = jnp.where(j >= n_q, RET_DK ** -0.5, 1.0).astype(F32)
        half = RET_DK // 2
        for h in range(tn // RET_DK):
            x1 = y[:, h * RET_DK:h * RET_DK + half]
            x2 = y[:, h * RET_DK + half:(h + 1) * RET_DK]
            o_ref[:, h * RET_DK:h * RET_DK + half] = ((x1 * cos - x2 * sin) * scale).astype(BF16)
            o_ref[:, h * RET_DK + half:(h + 1) * RET_DK] = ((x2 * cos + x1 * sin) * scale).astype(BF16)

    @pl.when(jnp.logical_and(j >= n_qk, j < n_qkv))
    def _():
        o_ref[...] = proj().astype(BF16)

    @pl.when(j >= n_qkv)
    def _():
        o_ref[...] = _silu(proj()).astype(BF16)


def ret_inproj(x, g, w, cos, sin, *, tm, tn=1024):
    n, d = x.shape
    m = w.shape[1]
    n_pos_tiles = cos.shape[0] // tm
    return pl.pallas_call(
        functools.partial(_ret_inproj_kernel, tn=tn),
        grid=(n // tm, m // tn),
        in_specs=[
            pl.BlockSpec((tm, d), lambda i, j: (i, 0)),
            pl.BlockSpec((1, d), lambda i, j: (0, 0)),
            pl.BlockSpec((d, tn), lambda i, j: (0, j)),
            pl.BlockSpec((tm, RET_DK // 2), lambda i, j: (i % n_pos_tiles, 0)),
            pl.BlockSpec((tm, RET_DK // 2), lambda i, j: (i % n_pos_tiles, 0)),
        ],
        out_specs=pl.BlockSpec((tm, tn), lambda i, j: (i, j)),
        out_shape=jax.ShapeDtypeStruct((n, m), BF16),
        scratch_shapes=[pltpu.VMEM((tm, d), BF16)],
        compiler_params=_cparams(("parallel", "arbitrary")),
        name="ret_inproj",
    )(x, g, w, cos, sin)


def _retention_kernel(*refs, L, n_chunks, has_r0):
    if has_r0:
        (q_ref, k_ref, v_ref, g_ref, dm_ref, xi_ref, zt_ref, r0_ref, o_ref, rout_ref, r_ref) = refs
    else:
        (q_ref, k_ref, v_ref, g_ref, dm_ref, xi_ref, zt_ref, o_ref, rout_ref, r_ref) = refs
    t = pl.program_id(1)

    @pl.when(t == 0)
    def _():
        if has_r0:
            r_ref[...] = r0_ref[0]
        else:
            r_ref[...] = jnp.zeros_like(r_ref)

    def chunk(c, carry):
        r0 = pl.multiple_of(c * L, L)
        for h in range(RET_HEADS):
            lg = math.log1p(-2.0 ** (-5.0 - h))
            q = q_ref[pl.ds(r0, L), h * RET_DK:(h + 1) * RET_DK]
            k = k_ref[pl.ds(r0, L), h * RET_DK:(h + 1) * RET_DK]
            v = v_ref[pl.ds(r0, L), h * RET_DV:(h + 1) * RET_DV]
            gate = g_ref[pl.ds(r0, L), h * RET_DV:(h + 1) * RET_DV]
            rh = r_ref[h]
            s = lax.dot_general(q, k, (((1,), (1,)), ((), ())), preferred_element_type=F32)
            s = s * dm_ref[h]
            intra = jnp.dot(s.astype(BF16), v, preferred_element_type=F32)
            inter = jnp.dot(q, rh.astype(BF16), preferred_element_type=F32) * xi_ref[h]
            o = intra + inter
            kz = (k.astype(F32) * zt_ref[h]).astype(BF16)
            upd = lax.dot_general(kz, v, (((0,), (0,)), ((), ())), preferred_element_type=F32)
            r_ref[h] = math.exp(lg * L) * rh + upd
            on = o * lax.rsqrt(jnp.mean(o * o, axis=-1, keepdims=True) + EPS)
            o_ref[pl.ds(r0, L), h * RET_DV:(h + 1) * RET_DV] = (on * gate.astype(F32)).astype(BF16)
        return carry

    lax.fori_loop(0, n_chunks, chunk, 0)

    @pl.when(t == pl.num_programs(1) - 1)
    def _():
        rout_ref[0] = r_ref[...]


def _retention_tables(L):
    lg = np.log1p(-np.exp2(-5.0 - np.arange(RET_HEADS, dtype=np.float64)))
    i = np.arange(L, dtype=np.float64)
    diff = i[:, None] - i[None, :]
    dmask = np.where(diff >= 0, np.exp(lg[:, None, None] * np.maximum(diff, 0.0)), 0.0)
    xi = np.exp(lg[:, None] * (i[None, :] + 1.0))
    zeta = np.exp(lg[:, None] * (L - 1.0 - i[None, :]))
    xi = np.broadcast_to(xi[:, :, None], (RET_HEADS, L, RET_DV))
    zeta = np.broadcast_to(zeta[:, :, None], (RET_HEADS, L, RET_DK))
    return (jnp.asarray(dmask, F32), jnp.asarray(xi, F32), jnp.asarray(zeta, F32))


def retention(proj, r0, *, batch, seq, L, tb):
    n = batch * seq
    nt = seq // tb
    dmask, xi, zeta = _retention_tables(L)
    dqk = RET_HEADS * RET_DK
    dv = RET_HEADS * RET_DV
    has_r0 = r0 is not None
    const3 = lambda b, t: (0, 0, 0)
    in_specs = [
        pl.BlockSpec((tb, dqk), lambda b, t: (b * nt + t, 0)),
        pl.BlockSpec((tb, dqk), lambda b, t: (b * nt + t, 1)),
        pl.BlockSpec((tb, dv), lambda b, t: (b * nt + t, 1)),
        pl.BlockSpec((tb, dv), lambda b, t: (b * nt + t, 2)),
        pl.BlockSpec((RET_HEADS, L, L), const3),
        pl.BlockSpec((RET_HEADS, L, RET_DV), const3),
        pl.BlockSpec((RET_HEADS, L, RET_DK), const3),
    ]
    args = [proj, proj, proj, proj, dmask, xi, zeta]
    state_spec = pl.BlockSpec((1, RET_HEADS, RET_DK, RET_DV), lambda b, t: (b, 0, 0, 0))
    if has_r0:
        in_specs.append(state_spec)
        args.append(r0)
    return pl.pallas_call(
        functools.partial(_retention_kernel, L=L, n_chunks=tb // L, has_r0=has_r0),
        grid=(batch, nt),
        in_specs=in_specs,
        out_specs=[pl.BlockSpec((tb, dv), lambda b, t: (b * nt + t, 0)), state_spec],
        out_shape=[jax.ShapeDtypeStruct((n, dv), BF16),
                   jax.ShapeDtypeStruct((batch, RET_HEADS, RET_DK, RET_DV), F32)],
        scratch_shapes=[pltpu.VMEM((RET_HEADS, RET_DK, RET_DV), F32)],
        compiler_params=_cparams(("parallel", "arbitrary")),
        name="retention",
    )(*args)


def _route(u, wr_hi, wr_lo, br):
    u_hi = u.astype(BF16)
    u_lo = (u - u_hi.astype(F32)).astype(BF16)
    logits = (jnp.dot(u_hi, wr_hi, preferred_element_type=F32)
              + jnp.dot(u_lo, wr_hi, preferred_element_type=F32)
              + jnp.dot(u_hi, wr_lo, preferred_element_type=F32)) + br
    lane_i = lax.broadcasted_iota(jnp.int32, logits.shape, 1)
    lane = lane_i.astype(F32)
    lane_grp = (lane_i // EXPERTS_PER_GROUP).astype(F32)
    big = jnp.float32(1 << 20)
    is_grp = jnp.logical_and(lane_i >= GROUP_LANE0, lane_i < GROUP_LANE0 + N_GROUPS)
    gl = jnp.where(is_grp, logits, -jnp.inf)
    gmax = jnp.max(gl, axis=-1, keepdims=True)
    gsum = jnp.sum(jnp.exp(gl - gmax), axis=-1, keepdims=True)
    p_top = 1.0 / gsum
    g_top = jnp.min(jnp.where(gl == gmax, lane, big), axis=-1, keepdims=True) - GROUP_LANE0
    sel = jnp.logical_and(lane_i < N_EXPERTS, lane_grp == g_top)
    el = jnp.where(sel, logits, -jnp.inf)
    emax = jnp.max(el, axis=-1, keepdims=True)
    ee = jnp.exp(el - emax)
    ep = ee / jnp.sum(ee, axis=-1, keepdims=True)
    epm = jnp.where(sel, ep, -1.0)
    m1 = jnp.max(epm, axis=-1, keepdims=True)
    i1 = jnp.min(jnp.where(epm == m1, lane, big), axis=-1, keepdims=True)
    epm2 = jnp.where(lane == i1, -1.0, epm)
    m2 = jnp.max(epm2, axis=-1, keepdims=True)
    i2 = jnp.min(jnp.where(epm2 == m2, lane, big), axis=-1, keepdims=True)
    denom = m1 + m2
    w1 = m1 / denom * p_top
    w2 = m2 / denom * p_top
    return jnp.where(lane == i1, w1, jnp.where(lane == i2, w2, 0.0))


def _outproj_kernel(a_ref, w_ref, res_ref, g_ref, wrh_ref, wrl_ref, br_ref, h_ref, u_ref, gate_ref, *, rc):
    for c in range(a_ref.shape[0] // rc):
        rows = slice(c * rc, (c + 1) * rc)
        h = res_ref[rows, :] + jnp.dot(a_ref[rows, :], w_ref[...], preferred_element_type=F32)
        h_ref[rows, :] = h
        u = _rms(h, g_ref[...])
        u_ref[rows, :] = u.astype(BF16)
        gate_ref[rows, :] = _route(u, wrh_ref[...], wrl_ref[...], br_ref[...])


def outproj_route(a, w, res, g, wr, br, *, tm):
    n, kdim = a.shape
    d = w.shape[1]
    wr_hi, wr_lo = wr
    return pl.pallas_call(
        functools.partial(_outproj_kernel, rc=min(tm, 256)),
        grid=(n // tm,),
        in_specs=[
            pl.BlockSpec((tm, kdim), lambda i: (i, 0)),
            pl.BlockSpec((kdim, d), lambda i: (0, 0)),
            pl.BlockSpec((tm, d), lambda i: (i, 0)),
            pl.BlockSpec((1, d), lambda i: (0, 0)),
            pl.BlockSpec((d, ROUTE_LANES), lambda i: (0, 0)),
            pl.BlockSpec((d, ROUTE_LANES), lambda i: (0, 0)),
            pl.BlockSpec((1, ROUTE_LANES), lambda i: (0, 0)),
        ],
        out_specs=[
            pl.BlockSpec((tm, d), lambda i: (i, 0)),
            pl.BlockSpec((tm, d), lambda i: (i, 0)),
            pl.BlockSpec((tm, ROUTE_LANES), lambda i: (i, 0)),
        ],
        out_shape=[jax.ShapeDtypeStruct((n, d), F32),
                   jax.ShapeDtypeStruct((n, d), BF16),
                   jax.ShapeDtypeStruct((n, ROUTE_LANES), F32)],
        compiler_params=_cparams(("parallel",)),
        name="outproj_route",
    )(a, w, res, g, wr_hi, wr_lo, br)


def _moe_kernel(u_ref, gate_ref, wg_ref, wu_ref, wd_ref, res_ref, gn_ref, *out_and_scratch, n_norms, emit_h):
    acc_ref = out_and_scratch[-1]
    outs = out_and_scratch[:-1]
    grp = pl.program_id(1)

    @pl.when(grp == 0)
    def _():
        acc_ref[...] = jnp.zeros_like(acc_ref)

    u = u_ref[...]
    gate = gate_ref[...]
    lane = lax.broadcasted_iota(jnp.int32, gate.shape, 1)
    contrib = None
    for e in range(EXPERTS_PER_GROUP):
        ge = jnp.sum(jnp.where(lane == grp * EXPERTS_PER_GROUP + e, gate, 0.0), axis=-1, keepdims=True)
        hg = jnp.dot(u, wg_ref[e], preferred_element_type=F32)
        hu = jnp.dot(u, wu_ref[e], preferred_element_type=F32)
        hid = (_silu(hg) * hu * ge).astype(BF16)
        d = jnp.dot(hid, wd_ref[e], preferred_element_type=F32)
        contrib = d if contrib is None else contrib + d
    acc_ref[...] += contrib

    @pl.when(grp == pl.num_programs(1) - 1)
    def _():
        h = res_ref[...] + acc_ref[...]
        k = 0
        if emit_h:
            outs[0][...] = h
            k = 1
        for t in range(n_norms):
            o = outs[k + t]
            o[...] = _rms(h, gn_ref[t:t + 1, :]).astype(o.dtype)


def moe(u, gate, wg, wu, wd, res, gnorms, *, tm, emit_h, norm_dtypes):
    n, d = u.shape
    n_norms = len(norm_dtypes)
    tok = lambda i, g: (i, 0)
    out_specs = []
    out_shape = []
    if emit_h:
        out_specs.append(pl.BlockSpec((tm, d), tok))
        out_shape.append(jax.ShapeDtypeStruct((n, d), F32))
    for dt in norm_dtypes:
        out_specs.append(pl.BlockSpec((tm, d), tok))
        out_shape.append(jax.ShapeDtypeStruct((n, d), dt))
    return pl.pallas_call(
        functools.partial(_moe_kernel, n_norms=n_norms, emit_h=emit_h),
        grid=(n // tm, N_GROUPS),
        in_specs=[
            pl.BlockSpec((tm, d), tok),
            pl.BlockSpec((tm, ROUTE_LANES), tok),
            pl.BlockSpec((EXPERTS_PER_GROUP, d, D_EXPERT), lambda i, g: (g, 0, 0)),
            pl.BlockSpec((EXPERTS_PER_GROUP, d, D_EXPERT), lambda i, g: (g, 0, 0)),
            pl.BlockSpec((EXPERTS_PER_GROUP, D_EXPERT, d), lambda i, g: (g, 0, 0)),
            pl.BlockSpec((tm, d), tok),
            pl.BlockSpec((n_norms, d), lambda i, g: (0, 0)),
        ],
        out_specs=out_specs,
        out_shape=out_shape,
        scratch_shapes=[pltpu.VMEM((tm, d), F32)],
        compiler_params=_cparams(("parallel", "arbitrary")),
        name="moe",
    )(u, gate, wg, wu, wd, res, gnorms)


def _rope_cols(y, cosf, sina, sinb):
    blocks = []
    for c in range(y.shape[1] // LANES):
        xb = y[:, c * LANES:(c + 1) * LANES]
        blocks.append(xb * cosf + pltpu.roll(xb, LANES - ROPE_DIM // 2, 1) * sina
                      + pltpu.roll(xb, ROPE_DIM // 2, 1) * sinb)
    return blocks


def _rope_rows(kt, cost, sint):
    half = ROPE_DIM // 2
    parts = []
    for g in range(kt.shape[0] // DIFF_DH):
        b0 = g * DIFF_DH
        x1 = kt[b0:b0 + half]
        x2 = kt[b0 + half:b0 + ROPE_DIM]
        parts += [x1 * cost - x2 * sint, x2 * cost + x1 * sint, kt[b0 + ROPE_DIM:b0 + DIFF_DH]]
    return jnp.concatenate(parts, axis=0)


def _qkv_kernel(*refs, tm, k_feature_major):
    if k_feature_major:
        (ukv_ref, umix_ref, wk_ref, wv_ref, wq_ref, cos_ref, sa_ref, sb_ref, cost_ref, sint_ref,
         k32_ref, v32_ref, kb_ref, vb_ref, qb_ref) = refs
    else:
        (ukv_ref, umix_ref, wk_ref, wv_ref, wq_ref, cos_ref, sa_ref, sb_ref,
         k32_ref, v32_ref, kb_ref, vb_ref, qb_ref) = refs
    cosf = cos_ref[...]
    sina = sa_ref[...]
    sinb = sb_ref[...]
    ukv = ukv_ref[...]
    if k_feature_major:
        kt = lax.dot_general(wk_ref[...], ukv, (((1,), (1,)), ((), ())), preferred_element_type=F32)
        kt = _rope_rows(kt, cost_ref[...], sint_ref[...])
        k32_ref[0] = kt
        kb_ref[0, 0] = kt.astype(BF16)
    else:
        k = jnp.dot(ukv, wk_ref[...], preferred_element_type=F32)
        for c, blk in enumerate(_rope_cols(k, cosf, sina, sinb)):
            k32_ref[:, c * LANES:(c + 1) * LANES] = blk
            kb_ref[:, c * LANES:(c + 1) * LANES] = blk.astype(BF16)
    v = jnp.dot(ukv, wv_ref[...], preferred_element_type=F32)
    vb_ref[...] = v.astype(BF16)
    for h in range(DIFF_HEADS):
        v32_ref[pl.ds(h, tm, stride=DIFF_HEADS), :] = v[:, h * DIFF_DV:(h + 1) * DIFF_DV]
    q = jnp.dot(umix_ref[...], wq_ref[...], preferred_element_type=F32)
    for c, blk in enumerate(_rope_cols(q, cosf, sina, sinb)):
        qb_ref[:, c * LANES:(c + 1) * LANES] = (blk * QK_SCALE_LOG2).astype(BF16)


def qkv_proj(ukv, umix, wk, wv, wq, cosf, sina, sinb, cost=None, sint=None, *, tm, batch, seq):
    n, d = ukv.shape
    k_feature_major = cost is not None
    nt = max(seq // tm, 1)
    n_pos_tiles = cosf.shape[0] // tm
    tok = lambda i: (i, 0)
    wspec = pl.BlockSpec((d, d), lambda i: (0, 0))
    tspec = pl.BlockSpec((tm, LANES), lambda i: (i % n_pos_tiles, 0))
    in_specs = [pl.BlockSpec((tm, d), tok), pl.BlockSpec((tm, d), tok), wspec, wspec, wspec,
                tspec, tspec, tspec]
    args = [ukv, umix, wk, wv, wq, cosf, sina, sinb]
    rows_f32 = (pl.BlockSpec((tm, d), tok), jax.ShapeDtypeStruct((n, d), F32))
    rows_bf16 = (pl.BlockSpec((tm, d), tok), jax.ShapeDtypeStruct((n, d), BF16))
    v32 = (pl.BlockSpec((tm * DIFF_HEADS, DIFF_DV), tok), jax.ShapeDtypeStruct((n * DIFF_HEADS, DIFF_DV), F32))
    if k_feature_major:
        half = ROPE_DIM // 2
        rspec = pl.BlockSpec((half, tm), lambda i: (0, i % nt))
        in_specs += [rspec, rspec]
        args += [cost, sint]
        k32 = (pl.BlockSpec((1, d, tm), lambda i: (i // nt, 0, i % nt)),
               jax.ShapeDtypeStruct((batch, d, seq), F32))
        kb = (pl.BlockSpec((1, 1, d, tm), lambda i: (i // nt, i % nt, 0, 0)),
              jax.ShapeDtypeStruct((batch, nt, d, tm), BF16))
    else:
        k32, kb = rows_f32, rows_bf16
    outs = [k32, v32, kb, rows_bf16, rows_bf16]
    return pl.pallas_call(
        functools.partial(_qkv_kernel, tm=tm, k_feature_major=k_feature_major),
        grid=(n // tm,),
        in_specs=in_specs,
        out_specs=[o[0] for o in outs],
        out_shape=[o[1] for o in outs],
        compiler_params=_cparams(("parallel",)),
        name="qkv_proj",
    )(*args)


def _lambda(lam_ref, lam_init):
    lv = lam_ref[...]
    a = jnp.sum(lv[0:1, :] * lv[1:2, :], axis=-1, keepdims=True)
    b = jnp.sum(lv[2:3, :] * lv[3:4, :], axis=-1, keepdims=True)
    return jnp.exp(a) - jnp.exp(b) + lam_init


def _split_q(q):
    qf = q.astype(F32)
    lane = lax.broadcasted_iota(jnp.int32, qf.shape, 1)
    q0 = jnp.where(lane < DIFF_DH, qf, 0.0).astype(BF16)
    q1 = jnp.where(lane >= DIFF_DH, qf, 0.0).astype(BF16)
    return jnp.concatenate([q0, q1], axis=0)


def _online_update(s, v, m_ref, l_ref, acc_ref):
    m_old = m_ref[...]
    if s.shape[1] < LANES:
        m_new = jnp.maximum(m_old, jnp.max(s, axis=-1, keepdims=True))
        alpha = jnp.exp2(m_old - m_new)
        p = jnp.exp2(s - m_new[:, :s.shape[1]])
        lane = lax.broadcasted_iota(jnp.int32, m_old.shape, 1)
        lsum = jnp.where(lane == 0, jnp.sum(p, axis=-1, keepdims=True), 0.0)
        l_ref[...] = alpha * l_ref[...] + lsum
        acc_ref[...] = alpha * acc_ref[...] + jnp.dot(p.astype(BF16), v, preferred_element_type=F32)
        m_ref[...] = m_new
        return
    ncol = s.shape[1] // LANES
    cols = [s[:, c * LANES:(c + 1) * LANES] for c in range(ncol)]
    mx = cols[0]
    for c in range(1, ncol):
        mx = jnp.maximum(mx, cols[c])
    m_new = jnp.maximum(m_old, jnp.max(mx, axis=-1, keepdims=True))
    alpha = jnp.exp2(m_old - m_new)
    ps = [jnp.exp2(col - m_new) for col in cols]
    lsum = ps[0]
    for c in range(1, ncol):
        lsum = lsum + ps[c]
    p = jnp.concatenate([pc.astype(BF16) for pc in ps], axis=1) if ncol > 1 else ps[0].astype(BF16)
    l_ref[...] = alpha * l_ref[...] + lsum
    acc_ref[...] = alpha * acc_ref[...] + jnp.dot(p, v, preferred_element_type=F32)
    m_ref[...] = m_new


def _diff_finish(l_ref, acc_ref, lam, subln, lam_init, tq):
    o = acc_ref[...] / jnp.sum(l_ref[...], axis=-1, keepdims=True)
    a = o[:tq] - lam * o[tq:]
    return _rms(a, subln) * (1.0 - lam_init)


def _attn_prompt_kernel(q_ref, k_ref, v_ref, lam_ref, sub_ref, o_ref, q2_ref, m_ref, l_ref, acc_ref,
                        *, tq, rc, lam_init):
    i = pl.program_id(2)
    q2_ref[...] = _split_q(q_ref[...])
    m_ref[...] = jnp.full_like(m_ref, NEG_BIG)
    l_ref[...] = jnp.zeros_like(l_ref)
    acc_ref[...] = jnp.zeros_like(acc_ref)

    def step(j, nblk, masked):
        r0 = pl.multiple_of(j * tq, tq)
        v = v_ref[pl.ds(r0, nblk * tq), :]
        for c in range(2 * tq // rc):
            rows = slice(c * rc, (c + 1) * rc)
            q2 = q2_ref[rows, :]
            parts = [jnp.dot(q2, k_ref[0, j + t], preferred_element_type=F32) for t in range(nblk)]
            s = parts[0] if nblk == 1 else jnp.concatenate(parts, axis=1)
            if masked:
                row = (c * rc + lax.broadcasted_iota(jnp.int32, s.shape, 0)) % tq
                col = lax.broadcasted_iota(jnp.int32, s.shape, 1)
                s = jnp.where(col < (row // CHUNK + 1) * CHUNK, s, NEG_BIG)
            _online_update(s, v, m_ref.at[rows, :], l_ref.at[rows, :], acc_ref.at[rows, :])

    def body(jj, carry):
        step(2 * jj, 2, False)
        return carry

    lax.fori_loop(0, i // 2, body, 0)

    @pl.when(i % 2 == 1)
    def _():
        step(i - 1, 1, False)

    step(i, 1, True)

    lam = _lambda(lam_ref, lam_init)
    o_ref[...] = _diff_finish(l_ref, acc_ref, lam, sub_ref[...], lam_init, tq).astype(BF16)


def attn_prompt(qb, kb, vb, lamv, subln, *, batch, seq, tq, lam_init, rc=1024):
    n = batch * seq
    nq = seq // tq
    rc = min(rc, 2 * tq)
    return pl.pallas_call(
        functools.partial(_attn_prompt_kernel, tq=tq, rc=rc, lam_init=lam_init),
        grid=(batch, DIFF_HEADS, nq),
        in_specs=[
            pl.BlockSpec((tq, LANES), lambda b, h, i: (b * nq + i, h)),
            pl.BlockSpec((1, nq, LANES, tq), lambda b, h, i: (b, 0, h, 0)),
            pl.BlockSpec((seq, LANES), lambda b, h, i: (b, h)),
            pl.BlockSpec((8, LANES), lambda b, h, i: (0, 0)),
            pl.BlockSpec((1, LANES), lambda b, h, i: (0, 0)),
        ],
        out_specs=pl.BlockSpec((tq, LANES), lambda b, h, i: (b * nq + i, h)),
        out_shape=jax.ShapeDtypeStruct((n, DIFF_HEADS * DIFF_DV), BF16),
        scratch_shapes=[pltpu.VMEM((2 * tq, LANES), BF16), pltpu.VMEM((2 * tq, LANES), F32),
                        pltpu.VMEM((2 * tq, LANES), F32), pltpu.VMEM((2 * tq, LANES), F32)],
        compiler_params=_cparams(("parallel", "parallel", "arbitrary")),
        name="attn_prompt",
    )(qb, kb, vb, lamv, subln)


def _attn_sample_kernel(q_ref, ck_ref, cv_ref, kn_ref, vn_ref, lam_ref, sub_ref, o_ref,
                        q2_ref, m_ref, l_ref, acc_ref, *, tq, past, lam_init):
    j = pl.program_id(1)
    last = pl.num_programs(1) - 1

    @pl.when(j == 0)
    def _():
        for h in range(DIFF_HEADS):
            q2_ref[h] = _split_q(q_ref[:, h * LANES:(h + 1) * LANES])
        m_ref[...] = jnp.full_like(m_ref, NEG_BIG)
        l_ref[...] = jnp.zeros_like(l_ref)
        acc_ref[...] = jnp.zeros_like(acc_ref)

    @pl.when(j < last)
    def _():
        tk = ck_ref.shape[-1]
        for h in range(DIFF_HEADS):
            kt = ck_ref[0, h].astype(BF16)
            v = cv_ref[pl.ds(h, tk, stride=DIFF_HEADS), :].astype(BF16)
            s = jnp.dot(q2_ref[h], kt, preferred_element_type=F32)
            _online_update(s, v, m_ref.at[h], l_ref.at[h], acc_ref.at[h])

    @pl.when(j == last)
    def _():
        lam = _lambda(lam_ref, lam_init)
        for h in range(DIFF_HEADS):
            k = kn_ref[:, h * LANES:(h + 1) * LANES]
            v = vn_ref[:, h * LANES:(h + 1) * LANES]
            s = lax.dot_general(q2_ref[h], k, (((1,), (1,)), ((), ())), preferred_element_type=F32)
            qpos = past + lax.broadcasted_iota(jnp.int32, s.shape, 0) % tq
            kpos = past + lax.broadcasted_iota(jnp.int32, s.shape, 1)
            s = jnp.where(kpos < (qpos // CHUNK + 1) * CHUNK, s, NEG_BIG)
            _online_update(s, v, m_ref.at[h], l_ref.at[h], acc_ref.at[h])
            o_ref[:, h * LANES:(h + 1) * LANES] = _diff_finish(
                l_ref.at[h], acc_ref.at[h], lam, sub_ref[...], lam_init, tq).astype(BF16)


def attn_sample(qb, cache_k, cache_v, kb, vb, lamv, subln, *, batch, seq, tk, lam_init):
    past = cache_k.shape[-1]
    d = DIFF_HEADS * DIFF_DV
    nk = past // tk
    ck_spec = pl.BlockSpec((1, DIFF_HEADS, LANES, tk), lambda b, j: (b, 0, 0, jnp.minimum(j, nk - 1)))
    cv_spec = pl.BlockSpec((tk * DIFF_HEADS, DIFF_DV), lambda b, j: (b * nk + jnp.minimum(j, nk - 1), 0))
    row_spec = pl.BlockSpec((seq, d), lambda b, j: (b, 0))
    return pl.pallas_call(
        functools.partial(_attn_sample_kernel, tq=seq, past=past, lam_init=lam_init),
        grid=(batch, nk + 1),
        in_specs=[row_spec, ck_spec, cv_spec, row_spec, row_spec,
                  pl.BlockSpec((8, LANES), lambda b, j: (0, 0)),
                  pl.BlockSpec((1, LANES), lambda b, j: (0, 0))],
        out_specs=row_spec,
        out_shape=jax.ShapeDtypeStruct((batch * seq, d), BF16),
        scratch_shapes=[pltpu.VMEM((DIFF_HEADS, 2 * seq, LANES), BF16),
                        pltpu.VMEM((DIFF_HEADS, 2 * seq, LANES), F32),
                        pltpu.VMEM((DIFF_HEADS, 2 * seq, LANES), F32),
                        pltpu.VMEM((DIFF_HEADS, 2 * seq, LANES), F32)],
        compiler_params=_cparams(("parallel", "arbitrary")),
        name="attn_sample",
    )(qb, cache_k, cache_v, kb, vb, lamv, subln)


def _ret_rope_tables(pos):
    half = RET_DK // 2
    inv_freq = jnp.power(jnp.float32(RET_THETA), -jnp.arange(half, dtype=F32) / half)
    ang = pos.astype(F32)[:, None] * inv_freq[None, :]
    return jnp.cos(ang), jnp.sin(ang)


def _diff_rope_tables(pos):
    half = ROPE_DIM // 2
    inv_freq = jnp.power(jnp.float32(ROPE_THETA), -jnp.arange(half, dtype=F32) / half)
    ang = pos.astype(F32)[:, None] * inv_freq[None, :]
    cos, sin = jnp.cos(ang), jnp.sin(ang)
    t = pos.shape[0]
    pad = DIFF_DH - ROPE_DIM
    cos64 = jnp.concatenate([cos, cos, jnp.ones((t, pad), F32)], axis=1)
    sina64 = jnp.concatenate([-sin, jnp.zeros((t, DIFF_DH - half), F32)], axis=1)
    sinb64 = jnp.concatenate([jnp.zeros((t, half), F32), sin, jnp.zeros((t, pad), F32)], axis=1)
    rep = LANES // DIFF_DH
    return (jnp.tile(cos64, (1, rep)), jnp.tile(sina64, (1, rep)), jnp.tile(sinb64, (1, rep)),
            cos.T, sin.T)


def _prep_params(p):
    w = {}
    w['ret_w_in'] = p['ret_w_in'][0].astype(BF16)
    w['ret_w_out'] = p['ret_w_out'][0].astype(BF16)
    w['kv_w_k'] = p['kv_w_k'].astype(BF16)
    w['kv_w_k_t'] = p['kv_w_k'].T.astype(BF16)
    w['kv_w_v'] = p['kv_w_v'].astype(BF16)
    w['diff_w_q'] = p['diff_w_q'][0].astype(BF16)
    w['diff_w_o'] = p['diff_w_o'][0].astype(BF16)
    w['moe_w_gate'] = p['moe_w_gate'].astype(BF16)
    w['moe_w_up'] = p['moe_w_up'].astype(BF16)
    w['moe_w_down'] = p['moe_w_down'].astype(BF16)
    wr, br = [], []
    for layer in range(2):
        route = jnp.transpose(p['moe_w_route'][layer], (1, 0, 2)).reshape(D_MODEL, N_EXPERTS)
        cols = jnp.concatenate([route, p['moe_w_group'][layer]], axis=1)
        full = jnp.pad(cols, ((0, 0), (0, ROUTE_LANES - cols.shape[1])))
        hi = full.astype(BF16)
        wr.append((hi, (full - hi.astype(F32)).astype(BF16)))
        bias = jnp.concatenate([p['moe_b_route'][layer].reshape(-1), p['moe_b_group'][layer]])
        br.append(jnp.pad(bias, (0, ROUTE_LANES - bias.shape[0]))[None, :])
    w['route_w'] = wr
    w['route_b'] = br
    lamv = jnp.concatenate([p['diff_lam_q1'][0][None], p['diff_lam_k1'][0][None],
                            p['diff_lam_q2'][0][None], p['diff_lam_k2'][0][None]], axis=0)
    w['lamv'] = jnp.pad(lamv, ((0, 4), (0, LANES - DIFF_DH)))
    return w


def _trunk(x, pos, r0, past_k, past_v, p, w, *, tm, ret_chunk, ret_tb, attn_tq=None, attn_tk=None):
    batch, seq, d = x.shape
    n = batch * seq
    xf = x.reshape(n, d)
    row = lambda v: v.reshape(1, -1)

    cos, sin = _ret_rope_tables(pos)
    if seq < tm:
        cos, sin = jnp.tile(cos, (tm // seq, 1)), jnp.tile(sin, (tm // seq, 1))
    proj = ret_inproj(xf, row(p['norm_mix'][0]), w['ret_w_in'], cos, sin, tm=tm)
    o_gated, r_new = retention(proj, r0, batch=batch, seq=seq, L=ret_chunk, tb=ret_tb)
    h1, u1, gate1 = outproj_route(o_gated, w['ret_w_out'], xf, row(p['norm_ffn'][0]),
                                  w['route_w'][0], w['route_b'][0], tm=tm)
    gn = jnp.stack([p['kv_norm'], p['norm_mix'][1]])
    h2, ukv, umix = moe(u1, gate1, w['moe_w_gate'][0], w['moe_w_up'][0], w['moe_w_down'][0], h1, gn,
                        tm=tm, emit_h=True, norm_dtypes=(BF16, BF16))

    cosf, sina, sinb, cost, sint = _diff_rope_tables(pos)
    if seq < tm:
        rep = (tm // seq, 1)
        cosf, sina, sinb = jnp.tile(cosf, rep), jnp.tile(sina, rep), jnp.tile(sinb, rep)
    lam_init = 0.8 - 0.6 * math.exp(-0.3 * 1)
    subln = row(p['diff_subln'][0])
    if past_k is None:
        assert attn_tq == tm
        k32, v32, kb, vb, qb = qkv_proj(ukv, umix, w['kv_w_k_t'], w['kv_w_v'], w['diff_w_q'],
                                        cosf, sina, sinb, cost, sint, tm=tm, batch=batch, seq=seq)
        attn = attn_prompt(qb, kb, vb, w['lamv'], subln, batch=batch, seq=seq, tq=attn_tq,
                           lam_init=lam_init)
        k_out = jnp.transpose(k32.reshape(batch, DIFF_HEADS, 2, DIFF_DH, seq), (0, 4, 1, 2, 3))
    else:
        past = past_k.shape[1]
        k32, v32, kb, vb, qb = qkv_proj(ukv, umix, w['kv_w_k'], w['kv_w_v'], w['diff_w_q'],
                                        cosf, sina, sinb, tm=tm, batch=batch, seq=seq)
        ck = jnp.transpose(past_k, (0, 2, 3, 4, 1)).reshape(batch, DIFF_HEADS, 2 * DIFF_DH, past)
        cv = past_v.reshape(batch * past * DIFF_HEADS, DIFF_DV)
        attn = attn_sample(qb, ck, cv, kb, vb, w['lamv'], subln, batch=batch, seq=seq, tk=attn_tk,
                           lam_init=lam_init)
        k_out = k32.reshape(batch, seq, DIFF_HEADS, 2, DIFF_DH)
    h3, u3, gate3 = outproj_route(attn, w['diff_w_o'], h2, row(p['norm_ffn'][1]),
                                  w['route_w'][1], w['route_b'][1], tm=tm)
    (y,) = moe(u3, gate3, w['moe_w_gate'][1], w['moe_w_up'][1], w['moe_w_down'][1], h3,
               row(p['norm_final']), tm=tm, emit_h=False, norm_dtypes=(F32,))

    return (y.reshape(batch, seq, d), r_new[None], k_out, v32.reshape(batch, seq, DIFF_HEADS, DIFF_DV))


def kernel(x_prompt, x_sample, state_ret, cache_k, cache_v, norm_mix, norm_ffn, norm_final, ret_w_in, ret_w_out, kv_norm, kv_w_k, kv_w_v, diff_w_q, diff_lam_q1, diff_lam_k1, diff_lam_q2, diff_lam_k2, diff_subln, diff_w_o, moe_w_group, moe_b_group, moe_w_route, moe_b_route, moe_w_gate, moe_w_up, moe_w_down):
    p = {
        'norm_mix': norm_mix, 'norm_ffn': norm_ffn, 'norm_final': norm_final,
        'ret_w_in': ret_w_in, 'ret_w_out': ret_w_out,
        'kv_norm': kv_norm, 'kv_w_k': kv_w_k, 'kv_w_v': kv_w_v,
        'diff_w_q': diff_w_q, 'diff_lam_q1': diff_lam_q1, 'diff_lam_k1': diff_lam_k1,
        'diff_lam_q2': diff_lam_q2, 'diff_lam_k2': diff_lam_k2, 'diff_subln': diff_subln,
        'diff_w_o': diff_w_o,
        'moe_w_group': moe_w_group, 'moe_b_group': moe_b_group, 'moe_w_route': moe_w_route,
        'moe_b_route': moe_b_route, 'moe_w_gate': moe_w_gate, 'moe_w_up': moe_w_up,
        'moe_w_down': moe_w_down,
    }
    w = _prep_params(p)
    tp = x_prompt.shape[1]
    ts = x_sample.shape[1]
    past = cache_k.shape[1]
    y_p, r_p, k_p, v_p = _trunk(x_prompt, jnp.arange(tp), None, None, None, p, w,
                                tm=512, ret_chunk=128, ret_tb=512, attn_tq=512)
    y_s, r_s, k_s, v_s = _trunk(x_sample, past + jnp.arange(ts), state_ret[0], cache_k, cache_v, p, w,
                                tm=x_sample.shape[0] * ts, ret_chunk=ts, ret_tb=ts, attn_tk=1024)
    return (y_p, y_s, r_p, k_p, v_p, r_s, k_s, v_s)
```

```python
import functools
import math

import numpy as np
import jax
import jax.numpy as jnp
from jax import lax
from jax.experimental import pallas as pl
from jax.experimental.pallas import tpu as pltpu

F32 = jnp.float32
BF16 = jnp.bfloat16

D_MODEL = 1024
CHUNK = 64
RET_HEADS = 4
RET_DK = 256
RET_DV = 512
RET_THETA = 10000.0
DIFF_HEADS = 8
DIFF_DH = 64
DIFF_DV = 128
ROPE_THETA = 500000.0
ROPE_DIM = 16
N_GROUPS = 4
EXPERTS_PER_GROUP = 4
N_EXPERTS = 16
D_EXPERT = 256
EPS = 1e-6

LANES = 128
VMEM_LIMIT = 56 * 1024 * 1024
ROUTE_LANES = 128
GROUP_LANE0 = N_EXPERTS
NEG_BIG = -1e30
QK_SCALE_LOG2 = DIFF_DH ** -0.5 * math.log2(math.e)


def _cparams(sem):
    return pltpu.CompilerParams(dimension_semantics=sem, vmem_limit_bytes=VMEM_LIMIT)


def _rms(x, g):
    ms = jnp.mean(x * x, axis=-1, keepdims=True)
    return x * lax.rsqrt(ms + EPS) * g


def _silu(x):
    return x * (1.0 / (1.0 + jnp.exp(-x)))


def _ret_inproj_kernel(x_ref, g_ref, w_ref, cos_ref, sin_ref, o_ref, xn_ref, *, tn):
    j = pl.program_id(1)
    n_q = (RET_HEADS * RET_DK) // tn
    n_qk = 2 * n_q
    n_qkv = n_qk + (RET_HEADS * RET_DV) // tn

    @pl.when(j == 0)
    def _():
        xn_ref[...] = _rms(x_ref[...], g_ref[...]).astype(BF16)

    def proj():
        return jnp.dot(xn_ref[...], w_ref[...], preferred_element_type=F32)

    @pl.when(j < n_qk)
    def _():
        y = proj()
        cos = cos_ref[...]
        sin = sin_ref[...]
        scale = jnp.where(j >= n_q, RET_DK ** -0.5, 1.0).astype(F32)
        half = RET_DK // 2
        for h in range(tn // RET_DK):
            x1 = y[:, h * RET_DK:h * RET_DK + half]
            x2 = y[:, h * RET_DK + half:(h + 1) * RET_DK]
            o_ref[:, h * RET_DK:h * RET_DK + half] = ((x1 * cos - x2 * sin) * scale).astype(BF16)
            o_ref[:, h * RET_DK + half:(h + 1) * RET_DK] = ((x2 * cos + x1 * sin) * scale).astype(BF16)

    @pl.when(jnp.logical_and(j >= n_qk, j < n_qkv))
    def _():
        o_ref[...] = proj().astype(BF16)

    @pl.when(j >= n_qkv)
    def _():
        o_ref[...] = _silu(proj()).astype(BF16)


def ret_inproj(x, g, w, cos, sin, *, tm, tn=1024):
    n, d = x.shape
    m = w.shape[1]
    n_pos_tiles = cos.shape[0] // tm
    return pl.pallas_call(
        functools.partial(_ret_inproj_kernel, tn=tn),
        grid=(n // tm, m // tn),
        in_specs=[
            pl.BlockSpec((tm, d), lambda i, j: (i, 0)),
            pl.BlockSpec((1, d), lambda i, j: (0, 0)),
            pl.BlockSpec((d, tn), lambda i, j: (0, j)),
            pl.BlockSpec((tm, RET_DK // 2), lambda i, j: (i % n_pos_tiles, 0)),
            pl.BlockSpec((tm, RET_DK // 2), lambda i, j: (i % n_pos_tiles, 0)),
        ],
        out_specs=pl.BlockSpec((tm, tn), lambda i, j: (i, j)),
        out_shape=jax.ShapeDtypeStruct((n, m), BF16),
        scratch_shapes=[pltpu.VMEM((tm, d), BF16)],
        compiler_params=_cparams(("parallel", "arbitrary")),
        name="ret_inproj",
    )(x, g, w, cos, sin)


def _retention_kernel(*refs, L, n_chunks, has_r0):
    if has_r0:
        (q_ref, k_ref, v_ref, g_ref, dm_ref, xi_ref, zt_ref, r0_ref, o_ref, rout_ref, r_ref) = refs
    else:
        (q_ref, k_ref, v_ref, g_ref, dm_ref, xi_ref, zt_ref, o_ref, rout_ref, r_ref) = refs
    t = pl.program_id(1)

    @pl.when(t == 0)
    def _():
        if has_r0:
            r_ref[...] = r0_ref[0]
        else:
            r_ref[...] = jnp.zeros_like(r_ref)

    def chunk(c, carry):
        r0 = pl.multiple_of(c * L, L)
        for h in range(RET_HEADS):
            lg = math.log1p(-2.0 ** (-5.0 - h))
            q = q_ref[pl.ds(r0, L), h * RET_DK:(h + 1) * RET_DK]
            k = k_ref[pl.ds(r0, L), h * RET_DK:(h + 1) * RET_DK]
            v = v_ref[pl.ds(r0, L), h * RET_DV:(h + 1) * RET_DV]
            gate = g_ref[pl.ds(r0, L), h * RET_DV:(h + 1) * RET_DV]
            rh = r_ref[h]
            s = lax.dot_general(q, k, (((1,), (1,)), ((), ())), preferred_element_type=F32)
            s = s * dm_ref[h]
            intra = jnp.dot(s.astype(BF16), v, preferred_element_type=F32)
            inter = jnp.dot(q, rh.astype(BF16), preferred_element_type=F32) * xi_ref[h]
            o = intra + inter
            kz = (k.astype(F32) * zt_ref[h]).astype(BF16)
            upd = lax.dot_general(kz, v, (((0,), (0,)), ((), ())), preferred_element_type=F32)
            r_ref[h] = math.exp(lg * L) * rh + upd
            on = o * lax.rsqrt(jnp.mean(o * o, axis=-1, keepdims=True) + EPS)
            o_ref[pl.ds(r0, L), h * RET_DV:(h + 1) * RET_DV] = (on * gate.astype(F32)).astype(BF16)
        return carry

    lax.fori_loop(0, n_chunks, chunk, 0)

    @pl.when(t == pl.num_programs(1) - 1)
    def _():
        rout_ref[0] = r_ref[...]


def _retention_tables(L):
    lg = np.log1p(-np.exp2(-5.0 - np.arange(RET_HEADS, dtype=np.float64)))
    i = np.arange(L, dtype=np.float64)
    diff = i[:, None] - i[None, :]
    dmask = np.where(diff >= 0, np.exp(lg[:, None, None] * np.maximum(diff, 0.0)), 0.0)
    xi = np.exp(lg[:, None] * (i[None, :] + 1.0))
    zeta = np.exp(lg[:, None] * (L - 1.0 - i[None, :]))
    xi = np.broadcast_to(xi[:, :, None], (RET_HEADS, L, RET_DV))
    zeta = np.broadcast_to(zeta[:, :, None], (RET_HEADS, L, RET_DK))
    return (jnp.asarray(dmask, F32), jnp.asarray(xi, F32), jnp.asarray(zeta, F32))


def retention(proj, r0, *, batch, seq, L, tb):
    n = batch * seq
    nt = seq // tb
    dmask, xi, zeta = _retention_tables(L)
    dqk = RET_HEADS * RET_DK
    dv = RET_HEADS * RET_DV
    has_r0 = r0 is not None
    const3 = lambda b, t: (0, 0, 0)
    in_specs = [
        pl.BlockSpec((tb, dqk), lambda b, t: (b * nt + t, 0)),
        pl.BlockSpec((tb, dqk), lambda b, t: (b * nt + t, 1)),
        pl.BlockSpec((tb, dv), lambda b, t: (b * nt + t, 1)),
        pl.BlockSpec((tb, dv), lambda b, t: (b * nt + t, 2)),
        pl.BlockSpec((RET_HEADS, L, L), const3),
        pl.BlockSpec((RET_HEADS, L, RET_DV), const3),
        pl.BlockSpec((RET_HEADS, L, RET_DK), const3),
    ]
    args = [proj, proj, proj, proj, dmask, xi, zeta]
    state_spec = pl.BlockSpec((1, RET_HEADS, RET_DK, RET_DV), lambda b, t: (b, 0, 0, 0))
    if has_r0:
        in_specs.append(state_spec)
        args.append(r0)
    return pl.pallas_call(
        functools.partial(_retention_kernel, L=L, n_chunks=tb // L, has_r0=has_r0),
        grid=(batch, nt),
        in_specs=in_specs,
        out_specs=[pl.BlockSpec((tb, dv), lambda b, t: (b * nt + t, 0)), state_spec],
        out_shape=[jax.ShapeDtypeStruct((n, dv), BF16),
                   jax.ShapeDtypeStruct((batch, RET_HEADS, RET_DK, RET_DV), F32)],
        scratch_shapes=[pltpu.VMEM((RET_HEADS, RET_DK, RET_DV), F32)],
        compiler_params=_cparams(("parallel", "arbitrary")),
        name="retention",
    )(*args)


def _route(u, wr_hi, wr_lo, br):
    u_hi = u.astype(BF16)
    u_lo = (u - u_hi.astype(F32)).astype(BF16)
    logits = (jnp.dot(u_hi, wr_hi, preferred_element_type=F32)
              + jnp.dot(u_lo, wr_hi, preferred_element_type=F32)
              + jnp.dot(u_hi, wr_lo, preferred_element_type=F32)) + br
    lane_i = lax.broadcasted_iota(jnp.int32, logits.shape, 1)
    lane = lane_i.astype(F32)
    lane_grp = (lane_i // EXPERTS_PER_GROUP).astype(F32)
    big = jnp.float32(1 << 20)
    is_grp = jnp.logical_and(lane_i >= GROUP_LANE0, lane_i < GROUP_LANE0 + N_GROUPS)
    gl = jnp.where(is_grp, logits, -jnp.inf)
    gmax = jnp.max(gl, axis=-1, keepdims=True)
    gsum = jnp.sum(jnp.exp(gl - gmax), axis=-1, keepdims=True)
    p_top = 1.0 / gsum
    g_top = jnp.min(jnp.where(gl == gmax, lane, big), axis=-1, keepdims=True) - GROUP_LANE0
    sel = jnp.logical_and(lane_i < N_EXPERTS, lane_grp == g_top)
    el = jnp.where(sel, logits, -jnp.inf)
    emax = jnp.max(el, axis=-1, keepdims=True)
    ee = jnp.exp(el - emax)
    ep = ee / jnp.sum(ee, axis=-1, keepdims=True)
    epm = jnp.where(sel, ep, -1.0)
    m1 = jnp.max(epm, axis=-1, keepdims=True)
    i1 = jnp.min(jnp.where(epm == m1, lane, big), axis=-1, keepdims=True)
    epm2 = jnp.where(lane == i1, -1.0, epm)
    m2 = jnp.max(epm2, axis=-1, keepdims=True)
    i2 = jnp.min(jnp.where(epm2 == m2, lane, big), axis=-1, keepdims=True)
    denom = m1 + m2
    w1 = m1 / denom * p_top
    w2 = m2 / denom * p_top
    return jnp.where(lane == i1, w1, jnp.where(lane == i2, w2, 0.0))


def _outproj_kernel(a_ref, w_ref, res_ref, g_ref, wrh_ref, wrl_ref, br_ref, h_ref, u_ref, gate_ref, *, rc):
    for c in range(a_ref.shape[0] // rc):
        rows = slice(c * rc, (c + 1) * rc)
        h = res_ref[rows, :] + jnp.dot(a_ref[rows, :], w_ref[...], preferred_element_type=F32)
        h_ref[rows, :] = h
        u = _rms(h, g_ref[...])
        u_ref[rows, :] = u.astype(BF16)
        gate_ref[rows, :] = _route(u, wrh_ref[...], wrl_ref[...], br_ref[...])


def outproj_route(a, w, res, g, wr, br, *, tm):
    n, kdim = a.shape
    d = w.shape[1]
    wr_hi, wr_lo = wr
    return pl.pallas_call(
        functools.partial(_outproj_kernel, rc=min(tm, 256)),
        grid=(n // tm,),
        in_specs=[
            pl.BlockSpec((tm, kdim), lambda i: (i, 0)),
            pl.BlockSpec((kdim, d), lambda i: (0, 0)),
            pl.BlockSpec((tm, d), lambda i: (i, 0)),
            pl.BlockSpec((1, d), lambda i: (0, 0)),
            pl.BlockSpec((d, ROUTE_LANES), lambda i: (0, 0)),
            pl.BlockSpec((d, ROUTE_LANES), lambda i: (0, 0)),
            pl.BlockSpec((1, ROUTE_LANES), lambda i: (0, 0)),
        ],
        out_specs=[
            pl.BlockSpec((tm, d), lambda i: (i, 0)),
            pl.BlockSpec((tm, d), lambda i: (i, 0)),
            pl.BlockSpec((tm, ROUTE_LANES), lambda i: (i, 0)),
        ],
        out_shape=[jax.ShapeDtypeStruct((n, d), F32),
                   jax.ShapeDtypeStruct((n, d), BF16),
                   jax.ShapeDtypeStruct((n, ROUTE_LANES), F32)],
        compiler_params=_cparams(("parallel",)),
        name="outproj_route",
    )(a, w, res, g, wr_hi, wr_lo, br)


def _moe_kernel(u_ref, gate_ref, wg_ref, wu_ref, wd_ref, res_ref, gn_ref, *out_and_scratch, n_norms, emit_h):
    acc_ref = out_and_scratch[-1]
    outs = out_and_scratch[:-1]
    grp = pl.program_id(1)

    @pl.when(grp == 0)
    def _():
        acc_ref[...] = jnp.zeros_like(acc_ref)

    u = u_ref[...]
    gate = gate_ref[...]
    lane = lax.broadcasted_iota(jnp.int32, gate.shape, 1)
    contrib = None
    for e in range(EXPERTS_PER_GROUP):
        ge = jnp.sum(jnp.where(lane == grp * EXPERTS_PER_GROUP + e, gate, 0.0), axis=-1, keepdims=True)
        hg = jnp.dot(u, wg_ref[e], preferred_element_type=F32)
        hu = jnp.dot(u, wu_ref[e], preferred_element_type=F32)
        hid = (_silu(hg) * hu * ge).astype(BF16)
        d = jnp.dot(hid, wd_ref[e], preferred_element_type=F32)
        contrib = d if contrib is None else contrib + d
    acc_ref[...] += contrib

    @pl.when(grp == pl.num_programs(1) - 1)
    def _():
        h = res_ref[...] + acc_ref[...]
        k = 0
        if emit_h:
            outs[0][...] = h
            k = 1
        for t in range(n_norms):
            o = outs[k + t]
            o[...] = _rms(h, gn_ref[t:t + 1, :]).astype(o.dtype)


def moe(u, gate, wg, wu, wd, res, gnorms, *, tm, emit_h, norm_dtypes):
    n, d = u.shape
    n_norms = len(norm_dtypes)
    tok = lambda i, g: (i, 0)
    out_specs = []
    out_shape = []
    if emit_h:
        out_specs.append(pl.BlockSpec((tm, d), tok))
        out_shape.append(jax.ShapeDtypeStruct((n, d), F32))
    for dt in norm_dtypes:
        out_specs.append(pl.BlockSpec((tm, d), tok))
        out_shape.append(jax.ShapeDtypeStruct((n, d), dt))
    return pl.pallas_call(
        functools.partial(_moe_kernel, n_norms=n_norms, emit_h=emit_h),
        grid=(n // tm, N_GROUPS),
        in_specs=[
            pl.BlockSpec((tm, d), tok),
            pl.BlockSpec((tm, ROUTE_LANES), tok),
            pl.BlockSpec((EXPERTS_PER_GROUP, d, D_EXPERT), lambda i, g: (g, 0, 0)),
            pl.BlockSpec((EXPERTS_PER_GROUP, d, D_EXPERT), lambda i, g: (g, 0, 0)),
            pl.BlockSpec((EXPERTS_PER_GROUP, D_EXPERT, d), lambda i, g: (g, 0, 0)),
            pl.BlockSpec((tm, d), tok),
            pl.BlockSpec((n_norms, d), lambda i, g: (0, 0)),
        ],
        out_specs=out_specs,
        out_shape=out_shape,
        scratch_shapes=[pltpu.VMEM((tm, d), F32)],
        compiler_params=_cparams(("parallel", "arbitrary")),
        name="moe",
    )(u, gate, wg, wu, wd, res, gnorms)


def _rope_cols(y, cosf, sina, sinb):
    blocks = []
    for c in range(y.shape[1] // LANES):
        xb = y[:, c * LANES:(c + 1) * LANES]
        blocks.append(xb * cosf + pltpu.roll(xb, LANES - ROPE_DIM // 2, 1) * sina
                      + pltpu.roll(xb, ROPE_DIM // 2, 1) * sinb)
    return blocks


def _rope_rows(kt, cost, sint):
    half = ROPE_DIM // 2
    parts = []
    for g in range(kt.shape[0] // DIFF_DH):
        b0 = g * DIFF_DH
        x1 = kt[b0:b0 + half]
        x2 = kt[b0 + half:b0 + ROPE_DIM]
        parts += [x1 * cost - x2 * sint, x2 * cost + x1 * sint, kt[b0 + ROPE_DIM:b0 + DIFF_DH]]
    return jnp.concatenate(parts, axis=0)


def _qkv_kernel(*refs, tm, k_feature_major):
    if k_feature_major:
        (ukv_ref, umix_ref, wk_ref, wv_ref, wq_ref, cos_ref, sa_ref, sb_ref, cost_ref, sint_ref,
         k32_ref, v32_ref, kb_ref, vb_ref, qb_ref) = refs
    else:
        (ukv_ref, umix_ref, wk_ref, wv_ref, wq_ref, cos_ref, sa_ref, sb_ref,
         k32_ref, v32_ref, kb_ref, vb_ref, qb_ref) = refs
    cosf = cos_ref[...]
    sina = sa_ref[...]
    sinb = sb_ref[...]
    ukv = ukv_ref[...]
    if k_feature_major:
        kt = lax.dot_general(wk_ref[...], ukv, (((1,), (1,)), ((), ())), preferred_element_type=F32)
        kt = _rope_rows(kt, cost_ref[...], sint_ref[...])
        k32_ref[0] = kt
        kb_ref[0, 0] = kt.astype(BF16)
    else:
        k = jnp.dot(ukv, wk_ref[...], preferred_element_type=F32)
        for c, blk in enumerate(_rope_cols(k, cosf, sina, sinb)):
            k32_ref[:, c * LANES:(c + 1) * LANES] = blk
            kb_ref[:, c * LANES:(c + 1) * LANES] = blk.astype(BF16)
    v = jnp.dot(ukv, wv_ref[...], preferred_element_type=F32)
    vb_ref[...] = v.astype(BF16)
    for h in range(DIFF_HEADS):
        v32_ref[pl.ds(h, tm, stride=DIFF_HEADS), :] = v[:, h * DIFF_DV:(h + 1) * DIFF_DV]
    q = jnp.dot(umix_ref[...], wq_ref[...], preferred_element_type=F32)
    for c, blk in enumerate(_rope_cols(q, cosf, sina, sinb)):
        qb_ref[:, c * LANES:(c + 1) * LANES] = (blk * QK_SCALE_LOG2).astype(BF16)


def qkv_proj(ukv, umix, wk, wv, wq, cosf, sina, sinb, cost=None, sint=None, *, tm, batch, seq):
    n, d = ukv.shape
    k_feature_major = cost is not None
    nt = max(seq // tm, 1)
    n_pos_tiles = cosf.shape[0] // tm
    tok = lambda i: (i, 0)
    wspec = pl.BlockSpec((d, d), lambda i: (0, 0))
    tspec = pl.BlockSpec((tm, LANES), lambda i: (i % n_pos_tiles, 0))
    in_specs = [pl.BlockSpec((tm, d), tok), pl.BlockSpec((tm, d), tok), wspec, wspec, wspec,
                tspec, tspec, tspec]
    args = [ukv, umix, wk, wv, wq, cosf, sina, sinb]
    rows_f32 = (pl.BlockSpec((tm, d), tok), jax.ShapeDtypeStruct((n, d), F32))
    rows_bf16 = (pl.BlockSpec((tm, d), tok), jax.ShapeDtypeStruct((n, d), BF16))
    v32 = (pl.BlockSpec((tm * DIFF_HEADS, DIFF_DV), tok), jax.ShapeDtypeStruct((n * DIFF_HEADS, DIFF_DV), F32))
    if k_feature_major:
        half = ROPE_DIM // 2
        rspec = pl.BlockSpec((half, tm), lambda i: (0, i % nt))
        in_specs += [rspec, rspec]
        args += [cost, sint]
        k32 = (pl.BlockSpec((1, d, tm), lambda i: (i // nt, 0, i % nt)),
               jax.ShapeDtypeStruct((batch, d, seq), F32))
        kb = (pl.BlockSpec((1, 1, d, tm), lambda i: (i // nt, i % nt, 0, 0)),
              jax.ShapeDtypeStruct((batch, nt, d, tm), BF16))
    else:
        k32, kb = rows_f32, rows_bf16
    outs = [k32, v32, kb, rows_bf16, rows_bf16]
    return pl.pallas_call(
        functools.partial(_qkv_kernel, tm=tm, k_feature_major=k_feature_major),
        grid=(n // tm,),
        in_specs=in_specs,
        out_specs=[o[0] for o in outs],
        out_shape=[o[1] for o in outs],
        compiler_params=_cparams(("parallel",)),
        name="qkv_proj",
    )(*args)


def _lambda(lam_ref, lam_init):
    lv = lam_ref[...]
    a = jnp.sum(lv[0:1, :] * lv[1:2, :], axis=-1, keepdims=True)
    b = jnp.sum(lv[2:3, :] * lv[3:4, :], axis=-1, keepdims=True)
    return jnp.exp(a) - jnp.exp(b) + lam_init


def _split_q(q):
    qf = q.astype(F32)
    lane = lax.broadcasted_iota(jnp.int32, qf.shape, 1)
    q0 = jnp.where(lane < DIFF_DH, qf, 0.0).astype(BF16)
    q1 = jnp.where(lane >= DIFF_DH, qf, 0.0).astype(BF16)
    return jnp.concatenate([q0, q1], axis=0)


def _online_update(s, v, m_ref, l_ref, acc_ref):
    m_old = m_ref[...]
    if s.shape[1] < LANES:
        m_new = jnp.maximum(m_old, jnp.max(s, axis=-1, keepdims=True))
        alpha = jnp.exp2(m_old - m_new)
        p = jnp.exp2(s - m_new[:, :s.shape[1]])
        lane = lax.broadcasted_iota(jnp.int32, m_old.shape, 1)
        lsum = jnp.where(lane == 0, jnp.sum(p, axis=-1, keepdims=True), 0.0)
        l_ref[...] = alpha * l_ref[...] + lsum
        acc_ref[...] = alpha * acc_ref[...] + jnp.dot(p.astype(BF16), v, preferred_element_type=F32)
        m_ref[...] = m_new
        return
    ncol = s.shape[1] // LANES
    cols = [s[:, c * LANES:(c + 1) * LANES] for c in range(ncol)]
    mx = cols[0]
    for c in range(1, ncol):
        mx = jnp.maximum(mx, cols[c])
    m_new = jnp.maximum(m_old, jnp.max(mx, axis=-1, keepdims=True))
    alpha = jnp.exp2(m_old - m_new)
    ps = [jnp.exp2(col - m_new) for col in cols]
    lsum = ps[0]
    for c in range(1, ncol):
        lsum = lsum + ps[c]
    p = jnp.concatenate([pc.astype(BF16) for pc in ps], axis=1) if ncol > 1 else ps[0].astype(BF16)
    l_ref[...] = alpha * l_ref[...] + lsum
    acc_ref[...] = alpha * acc_ref[...] + jnp.dot(p, v, preferred_element_type=F32)
    m_ref[...] = m_new


def _diff_finish(l_ref, acc_ref, lam, subln, lam_init, tq):
    o = acc_ref[...] / jnp.sum(l_ref[...], axis=-1, keepdims=True)
    a = o[:tq] - lam * o[tq:]
    return _rms(a, subln) * (1.0 - lam_init)


def _attn_prompt_kernel(q_ref, k_ref, v_ref, lam_ref, sub_ref, o_ref, q2_ref, m_ref, l_ref, acc_ref,
                        *, tq, kblk, qg, lam_init):
    i = pl.program_id(2)
    ng = tq // qg
    for g in range(ng):
        q2_ref[2 * g * qg:2 * (g + 1) * qg, :] = _split_q(q_ref[g * qg:(g + 1) * qg, :])
    m_ref[...] = jnp.full_like(m_ref, NEG_BIG)
    l_ref[...] = jnp.zeros_like(l_ref)
    acc_ref[...] = jnp.zeros_like(acc_ref)

    def step(j, nkeys, rows, mask_group=None):
        r0 = pl.multiple_of(j * kblk, kblk)
        v = v_ref[pl.ds(r0, nkeys), :]
        q2 = q2_ref[rows, :]
        parts = []
        for t in range(-(-nkeys // kblk)):
            kt = k_ref[0, j + t]
            if (t + 1) * kblk > nkeys:
                kt = kt[:, :nkeys - t * kblk]
            parts.append(jnp.dot(q2, kt, preferred_element_type=F32))
        s = parts[0] if len(parts) == 1 else jnp.concatenate(parts, axis=1)
        if mask_group is not None:
            qrow = mask_group * qg + lax.broadcasted_iota(jnp.int32, s.shape, 0) % qg
            col = lax.broadcasted_iota(jnp.int32, s.shape, 1)
            s = jnp.where(col < (qrow // CHUNK + 1) * CHUNK, s, NEG_BIG)
        _online_update(s, v, m_ref.at[rows, :], l_ref.at[rows, :], acc_ref.at[rows, :])

    all_rows = slice(0, 2 * tq)
    nb = tq // kblk
    n_full = i * nb

    def body(jj, carry):
        step(2 * jj, 2 * kblk, all_rows)
        return carry

    lax.fori_loop(0, n_full // 2, body, 0)
    if nb % 2 == 1:
        @pl.when(n_full % 2 == 1)
        def _():
            step(n_full - 1, kblk, all_rows)

    lam = _lambda(lam_ref, lam_init)
    for g in range(ng):
        rows = slice(2 * g * qg, 2 * (g + 1) * qg)
        step(n_full, (g + 1) * qg, rows, mask_group=g)
        o_ref[g * qg:(g + 1) * qg, :] = _diff_finish(
            l_ref.at[rows, :], acc_ref.at[rows, :], lam, sub_ref[...], lam_init, qg).astype(BF16)


def attn_prompt(qb, kb, vb, lamv, subln, *, batch, seq, tq, kblk, lam_init):
    n = batch * seq
    nq = seq // tq
    qg = tq // 2
    return pl.pallas_call(
        functools.partial(_attn_prompt_kernel, tq=tq, kblk=kblk, qg=qg, lam_init=lam_init),
        grid=(batch, DIFF_HEADS, nq),
        in_specs=[
            pl.BlockSpec((tq, LANES), lambda b, h, i: (b * nq + i, h)),
            pl.BlockSpec((1, seq // kblk, LANES, kblk), lambda b, h, i: (b, 0, h, 0)),
            pl.BlockSpec((seq, LANES), lambda b, h, i: (b, h)),
            pl.BlockSpec((8, LANES), lambda b, h, i: (0, 0)),
            pl.BlockSpec((1, LANES), lambda b, h, i: (0, 0)),
        ],
        out_specs=pl.BlockSpec((tq, LANES), lambda b, h, i: (b * nq + i, h)),
        out_shape=jax.ShapeDtypeStruct((n, DIFF_HEADS * DIFF_DV), BF16),
        scratch_shapes=[pltpu.VMEM((2 * tq, LANES), BF16), pltpu.VMEM((2 * tq, LANES), F32),
                        pltpu.VMEM((2 * tq, LANES), F32), pltpu.VMEM((2 * tq, LANES), F32)],
        compiler_params=_cparams(("parallel", "parallel", "arbitrary")),
        name="attn_prompt",
    )(qb, kb, vb, lamv, subln)


def _attn_sample_kernel(q_ref, ck_ref, cv_ref, kn_ref, vn_ref, lam_ref, sub_ref, o_ref,
                        q2_ref, m_ref, l_ref, acc_ref, *, tq, past, lam_init):
    j = pl.program_id(1)
    last = pl.num_programs(1) - 1

    @pl.when(j == 0)
    def _():
        for h in range(DIFF_HEADS):
            q2_ref[h] = _split_q(q_ref[:, h * LANES:(h + 1) * LANES])
        m_ref[...] = jnp.full_like(m_ref, NEG_BIG)
        l_ref[...] = jnp.zeros_like(l_ref)
        acc_ref[...] = jnp.zeros_like(acc_ref)

    @pl.when(j < last)
    def _():
        tk = ck_ref.shape[-1]
        for h in range(DIFF_HEADS):
            kt = ck_ref[0, h].astype(BF16)
            v = cv_ref[pl.ds(h, tk, stride=DIFF_HEADS), :].astype(BF16)
            s = jnp.dot(q2_ref[h], kt, preferred_element_type=F32)
            _online_update(s, v, m_ref.at[h], l_ref.at[h], acc_ref.at[h])

    @pl.when(j == last)
    def _():
        lam = _lambda(lam_ref, lam_init)
        for h in range(DIFF_HEADS):
            k = kn_ref[:, h * LANES:(h + 1) * LANES]
            v = vn_ref[:, h * LANES:(h + 1) * LANES]
            s = lax.dot_general(q2_ref[h], k, (((1,), (1,)), ((), ())), preferred_element_type=F32)
            qpos = past + lax.broadcasted_iota(jnp.int32, s.shape, 0) % tq
            kpos = past + lax.broadcasted_iota(jnp.int32, s.shape, 1)
            s = jnp.where(kpos < (qpos // CHUNK + 1) * CHUNK, s, NEG_BIG)
            _online_update(s, v, m_ref.at[h], l_ref.at[h], acc_ref.at[h])
            o_ref[:, h * LANES:(h + 1) * LANES] = _diff_finish(
                l_ref.at[h], acc_ref.at[h], lam, sub_ref[...], lam_init, tq).astype(BF16)


def attn_sample(qb, cache_k, cache_v, kb, vb, lamv, subln, *, batch, seq, tk, lam_init):
    past = cache_k.shape[-1]
    d = DIFF_HEADS * DIFF_DV
    nk = past // tk
    ck_spec = pl.BlockSpec((1, DIFF_HEADS, LANES, tk), lambda b, j: (b, 0, 0, jnp.minimum(j, nk - 1)))
    cv_spec = pl.BlockSpec((tk * DIFF_HEADS, DIFF_DV), lambda b, j: (b * nk + jnp.minimum(j, nk - 1), 0))
    row_spec = pl.BlockSpec((seq, d), lambda b, j: (b, 0))
    return pl.pallas_call(
        functools.partial(_attn_sample_kernel, tq=seq, past=past, lam_init=lam_init),
        grid=(batch, nk + 1),
        in_specs=[row_spec, ck_spec, cv_spec, row_spec, row_spec,
                  pl.BlockSpec((8, LANES), lambda b, j: (0, 0)),
                  pl.BlockSpec((1, LANES), lambda b, j: (0, 0))],
        out_specs=row_spec,
        out_shape=jax.ShapeDtypeStruct((batch * seq, d), BF16),
        scratch_shapes=[pltpu.VMEM((DIFF_HEADS, 2 * seq, LANES), BF16),
                        pltpu.VMEM((DIFF_HEADS, 2 * seq, LANES), F32),
                        pltpu.VMEM((DIFF_HEADS, 2 * seq, LANES), F32),
                        pltpu.VMEM((DIFF_HEADS, 2 * seq, LANES), F32)],
        compiler_params=_cparams(("parallel", "arbitrary")),
        name="attn_sample",
    )(qb, cache_k, cache_v, kb, vb, lamv, subln)


def _angles(pos, half, theta):
    pos = np.asarray(pos)
    inv_freq = np.power(np.float32(theta), -np.arange(half, dtype=np.float32) / np.float32(half))
    return (pos.astype(np.float32)[:, None] * inv_freq.astype(np.float32)[None, :]).astype(np.float64)


def _ret_rope_tables(pos):
    ang = _angles(pos, RET_DK // 2, RET_THETA)
    return np.cos(ang).astype(np.float32), np.sin(ang).astype(np.float32)


def _diff_rope_tables(pos):
    half = ROPE_DIM // 2
    ang = _angles(pos, half, ROPE_THETA)
    cos, sin = np.cos(ang).astype(np.float32), np.sin(ang).astype(np.float32)
    t = cos.shape[0]
    pad = DIFF_DH - ROPE_DIM
    cos64 = np.concatenate([cos, cos, np.ones((t, pad), np.float32)], axis=1)
    sina64 = np.concatenate([-sin, np.zeros((t, DIFF_DH - half), np.float32)], axis=1)
    sinb64 = np.concatenate([np.zeros((t, half), np.float32), sin, np.zeros((t, pad), np.float32)], axis=1)
    rep = LANES // DIFF_DH
    return (np.tile(cos64, (1, rep)), np.tile(sina64, (1, rep)), np.tile(sinb64, (1, rep)),
            np.ascontiguousarray(cos.T), np.ascontiguousarray(sin.T))


def _prep_params(p):
    w = {}
    w['ret_w_in'] = p['ret_w_in'][0].astype(BF16)
    w['ret_w_out'] = p['ret_w_out'][0].astype(BF16)
    w['kv_w_k'] = p['kv_w_k'].astype(BF16)
    w['kv_w_k_t'] = p['kv_w_k'].T.astype(BF16)
    w['kv_w_v'] = p['kv_w_v'].astype(BF16)
    w['diff_w_q'] = p['diff_w_q'][0].astype(BF16)
    w['diff_w_o'] = p['diff_w_o'][0].astype(BF16)
    w['moe_w_gate'] = p['moe_w_gate'].astype(BF16)
    w['moe_w_up'] = p['moe_w_up'].astype(BF16)
    w['moe_w_down'] = p['moe_w_down'].astype(BF16)
    wr, br = [], []
    for layer in range(2):
        route = jnp.transpose(p['moe_w_route'][layer], (1, 0, 2)).reshape(D_MODEL, N_EXPERTS)
        cols = jnp.concatenate([route, p['moe_w_group'][layer]], axis=1)
        full = jnp.pad(cols, ((0, 0), (0, ROUTE_LANES - cols.shape[1])))
        hi = full.astype(BF16)
        wr.append((hi, (full - hi.astype(F32)).astype(BF16)))
        bias = jnp.concatenate([p['moe_b_route'][layer].reshape(-1), p['moe_b_group'][layer]])
        br.append(jnp.pad(bias, (0, ROUTE_LANES - bias.shape[0]))[None, :])
    w['route_w'] = wr
    w['route_b'] = br
    lamv = jnp.concatenate([p['diff_lam_q1'][0][None], p['diff_lam_k1'][0][None],
                            p['diff_lam_q2'][0][None], p['diff_lam_k2'][0][None]], axis=0)
    w['lamv'] = jnp.pad(lamv, ((0, 4), (0, LANES - DIFF_DH)))
    return w


def _trunk(x, pos, r0, past_k, past_v, p, w, *, tm, ret_chunk, ret_tb, attn_tq=None, attn_tk=None):
    batch, seq, d = x.shape
    n = batch * seq
    xf = x.reshape(n, d)
    row = lambda v: v.reshape(1, -1)

    cos, sin = _ret_rope_tables(pos)
    if seq < tm:
        cos, sin = np.tile(cos, (tm // seq, 1)), np.tile(sin, (tm // seq, 1))
    proj = ret_inproj(xf, row(p['norm_mix'][0]), w['ret_w_in'], cos, sin, tm=min(2 * tm, seq, n) if seq >= tm else tm)
    o_gated, r_new = retention(proj, r0, batch=batch, seq=seq, L=ret_chunk, tb=ret_tb)
    h1, u1, gate1 = outproj_route(o_gated, w['ret_w_out'], xf, row(p['norm_ffn'][0]),
                                  w['route_w'][0], w['route_b'][0], tm=tm)
    gn = jnp.stack([p['kv_norm'], p['norm_mix'][1]])
    h2, ukv, umix = moe(u1, gate1, w['moe_w_gate'][0], w['moe_w_up'][0], w['moe_w_down'][0], h1, gn,
                        tm=tm, emit_h=True, norm_dtypes=(BF16, BF16))

    cosf, sina, sinb, cost, sint = _diff_rope_tables(pos)
    if seq < tm:
        rep = (tm // seq, 1)
        cosf, sina, sinb = np.tile(cosf, rep), np.tile(sina, rep), np.tile(sinb, rep)
    lam_init = 0.8 - 0.6 * math.exp(-0.3 * 1)
    subln = row(p['diff_subln'][0])
    if past_k is None:
        assert attn_tq % tm == 0
        k32, v32, kb, vb, qb = qkv_proj(ukv, umix, w['kv_w_k_t'], w['kv_w_v'], w['diff_w_q'],
                                        cosf, sina, sinb, cost, sint, tm=tm, batch=batch, seq=seq)
        attn = attn_prompt(qb, kb, vb, w['lamv'], subln, batch=batch, seq=seq, tq=attn_tq, kblk=tm,
                           lam_init=lam_init)
        k_out = jnp.transpose(k32.reshape(batch, DIFF_HEADS, 2, DIFF_DH, seq), (0, 4, 1, 2, 3))
    else:
        past = past_k.shape[1]
        k32, v32, kb, vb, qb = qkv_proj(ukv, umix, w['kv_w_k'], w['kv_w_v'], w['diff_w_q'],
                                        cosf, sina, sinb, tm=tm, batch=batch, seq=seq)
        ck = jnp.transpose(past_k, (0, 2, 3, 4, 1)).reshape(batch, DIFF_HEADS, 2 * DIFF_DH, past)
        cv = past_v.reshape(batch * past * DIFF_HEADS, DIFF_DV)
        attn = attn_sample(qb, ck, cv, kb, vb, w['lamv'], subln, batch=batch, seq=seq, tk=attn_tk,
                           lam_init=lam_init)
        k_out = k32.reshape(batch, seq, DIFF_HEADS, 2, DIFF_DH)
    h3, u3, gate3 = outproj_route(attn, w['diff_w_o'], h2, row(p['norm_ffn'][1]),
                                  w['route_w'][1], w['route_b'][1], tm=tm)
    (y,) = moe(u3, gate3, w['moe_w_gate'][1], w['moe_w_up'][1], w['moe_w_down'][1], h3,
               row(p['norm_final']), tm=tm, emit_h=False, norm_dtypes=(F32,))

    return (y.reshape(batch, seq, d), r_new[None], k_out, v32.reshape(batch, seq, DIFF_HEADS, DIFF_DV))


def kernel(x_prompt, x_sample, state_ret, cache_k, cache_v, norm_mix, norm_ffn, norm_final, ret_w_in, ret_w_out, kv_norm, kv_w_k, kv_w_v, diff_w_q, diff_lam_q1, diff_lam_k1, diff_lam_q2, diff_lam_k2, diff_subln, diff_w_o, moe_w_group, moe_b_group, moe_w_route, moe_b_route, moe_w_gate, moe_w_up, moe_w_down):
    p = {
        'norm_mix': norm_mix, 'norm_ffn': norm_ffn, 'norm_final': norm_final,
        'ret_w_in': ret_w_in, 'ret_w_out': ret_w_out,
        'kv_norm': kv_norm, 'kv_w_k': kv_w_k, 'kv_w_v': kv_w_v,
        'diff_w_q': diff_w_q, 'diff_lam_q1': diff_lam_q1, 'diff_lam_k1': diff_lam_k1,
        'diff_lam_q2': diff_lam_q2, 'diff_lam_k2': diff_lam_k2, 'diff_subln': diff_subln,
        'diff_w_o': diff_w_o,
        'moe_w_group': moe_w_group, 'moe_b_group': moe_b_group, 'moe_w_route': moe_w_route,
        'moe_b_route': moe_b_route, 'moe_w_gate': moe_w_gate, 'moe_w_up': moe_w_up,
        'moe_w_down': moe_w_down,
    }
    w = _prep_params(p)
    tp = x_prompt.shape[1]
    ts = x_sample.shape[1]
    past = cache_k.shape[1]
    y_p, r_p, k_p, v_p = _trunk(x_prompt, np.arange(tp), None, None, None, p, w,
                                tm=512, ret_chunk=256, ret_tb=512, attn_tq=512)
    y_s, r_s, k_s, v_s = _trunk(x_sample, past + np.arange(ts), state_ret[0], cache_k, cache_v, p, w,
                                tm=x_sample.shape[0] * ts, ret_chunk=ts, ret_tb=ts, attn_tk=1024)
    return (y_p, y_s, r_p, k_p, v_p, r_s, k_s, v_s)
```

```python
import functools
import math

import numpy as np
import jax
import jax.numpy as jnp
from jax import lax
from jax.experimental import pallas as pl
from jax.experimental.pallas import tpu as pltpu

F32 = jnp.float32
BF16 = jnp.bfloat16

D_MODEL = 1024
CHUNK = 64
RET_HEADS = 4
RET_DK = 256
RET_DV = 512
RET_THETA = 10000.0
DIFF_HEADS = 8
DIFF_DH = 64
DIFF_DV = 128
ROPE_THETA = 500000.0
ROPE_DIM = 16
N_GROUPS = 4
EXPERTS_PER_GROUP = 4
N_EXPERTS = 16
D_EXPERT = 256
EPS = 1e-6

LANES = 128
VMEM_LIMIT = 56 * 1024 * 1024
ROUTE_LANES = 128
GROUP_LANE0 = N_EXPERTS
GROUP_ID_LANE = GROUP_LANE0 + N_GROUPS
HX_WIDTH = D_MODEL + ROUTE_LANES
NEG_BIG = -1e30
QK_SCALE_LOG2 = DIFF_DH ** -0.5 * math.log2(math.e)


def _cparams(sem):
    return pltpu.CompilerParams(dimension_semantics=sem, vmem_limit_bytes=VMEM_LIMIT)


def _rms(x, g):
    ms = jnp.mean(x * x, axis=-1, keepdims=True)
    return x * lax.rsqrt(ms + EPS) * g


def _silu(x):
    return x * (1.0 / (1.0 + jnp.exp(-x)))


def _ret_inproj_kernel(x_ref, g_ref, w_ref, cos_ref, sin_ref, o_ref, xn_ref, *, tn):
    j = pl.program_id(1)
    n_q = (RET_HEADS * RET_DK) // tn
    n_qk = 2 * n_q
    n_qkv = n_qk + (RET_HEADS * RET_DV) // tn

    @pl.when(j == 0)
    def _():
        xn_ref[...] = _rms(x_ref[...], g_ref[...]).astype(BF16)

    def proj():
        return jnp.dot(xn_ref[...], w_ref[...], preferred_element_type=F32)

    @pl.when(j < n_qk)
    def _():
        y = proj()
        cos = cos_ref[...]
        sin = sin_ref[...]
        scale = jnp.where(j >= n_q, RET_DK ** -0.5, 1.0).astype(F32)
        half = RET_DK // 2
        for h in range(tn // RET_DK):
            x1 = y[:, h * RET_DK:h * RET_DK + half]
            x2 = y[:, h * RET_DK + half:(h + 1) * RET_DK]
            o_ref[:, h * RET_DK:h * RET_DK + half] = ((x1 * cos - x2 * sin) * scale).astype(BF16)
            o_ref[:, h * RET_DK + half:(h + 1) * RET_DK] = ((x2 * cos + x1 * sin) * scale).astype(BF16)

    @pl.when(jnp.logical_and(j >= n_qk, j < n_qkv))
    def _():
        o_ref[...] = proj().astype(BF16)

    @pl.when(j >= n_qkv)
    def _():
        o_ref[...] = _silu(proj()).astype(BF16)


def ret_inproj(x, g, w, cos, sin, *, tm, tn=1024):
    n, d = x.shape
    m = w.shape[1]
    n_pos_tiles = cos.shape[0] // tm
    return pl.pallas_call(
        functools.partial(_ret_inproj_kernel, tn=tn),
        grid=(n // tm, m // tn),
        in_specs=[
            pl.BlockSpec((tm, d), lambda i, j: (i, 0)),
            pl.BlockSpec((1, d), lambda i, j: (0, 0)),
            pl.BlockSpec((d, tn), lambda i, j: (0, j)),
            pl.BlockSpec((tm, RET_DK // 2), lambda i, j: (i % n_pos_tiles, 0)),
            pl.BlockSpec((tm, RET_DK // 2), lambda i, j: (i % n_pos_tiles, 0)),
        ],
        out_specs=pl.BlockSpec((tm, tn), lambda i, j: (i, j)),
        out_shape=jax.ShapeDtypeStruct((n, m), BF16),
        scratch_shapes=[pltpu.VMEM((tm, d), BF16)],
        compiler_params=_cparams(("parallel", "arbitrary")),
        name="ret_inproj",
    )(x, g, w, cos, sin)


def _retention_kernel(*refs, L, n_chunks, has_r0):
    if has_r0:
        (q_ref, k_ref, v_ref, g_ref, dm_ref, xi_ref, zt_ref, r0_ref, o_ref, rout_ref, r_ref) = refs
    else:
        (q_ref, k_ref, v_ref, g_ref, dm_ref, xi_ref, zt_ref, o_ref, rout_ref, r_ref) = refs
    t = pl.program_id(1)

    @pl.when(t == 0)
    def _():
        if has_r0:
            r_ref[...] = r0_ref[0]
        else:
            r_ref[...] = jnp.zeros_like(r_ref)

    def chunk(c, carry):
        r0 = pl.multiple_of(c * L, L)
        for h in range(RET_HEADS):
            lg = math.log1p(-2.0 ** (-5.0 - h))
            q = q_ref[pl.ds(r0, L), h * RET_DK:(h + 1) * RET_DK]
            k = k_ref[pl.ds(r0, L), h * RET_DK:(h + 1) * RET_DK]
            v = v_ref[pl.ds(r0, L), h * RET_DV:(h + 1) * RET_DV]
            gate = g_ref[pl.ds(r0, L), h * RET_DV:(h + 1) * RET_DV]
            rh = r_ref[h]
            s = lax.dot_general(q, k, (((1,), (1,)), ((), ())), preferred_element_type=F32)
            s = s * dm_ref[h]
            intra = jnp.dot(s.astype(BF16), v, preferred_element_type=F32)
            inter = jnp.dot(q, rh.astype(BF16), preferred_element_type=F32) * xi_ref[h]
            o = intra + inter
            kz = (k.astype(F32) * zt_ref[h]).astype(BF16)
            upd = lax.dot_general(kz, v, (((0,), (0,)), ((), ())), preferred_element_type=F32)
            r_ref[h] = math.exp(lg * L) * rh + upd
            on = o * lax.rsqrt(jnp.mean(o * o, axis=-1, keepdims=True) + EPS)
            o_ref[pl.ds(r0, L), h * RET_DV:(h + 1) * RET_DV] = (on * gate.astype(F32)).astype(BF16)
        return carry

    lax.fori_loop(0, n_chunks, chunk, 0)

    @pl.when(t == pl.num_programs(1) - 1)
    def _():
        rout_ref[0] = r_ref[...]


def _retention_tables(L):
    lg = np.log1p(-np.exp2(-5.0 - np.arange(RET_HEADS, dtype=np.float64)))
    i = np.arange(L, dtype=np.float64)
    diff = i[:, None] - i[None, :]
    dmask = np.where(diff >= 0, np.exp(lg[:, None, None] * np.maximum(diff, 0.0)), 0.0)
    xi = np.exp(lg[:, None] * (i[None, :] + 1.0))
    zeta = np.exp(lg[:, None] * (L - 1.0 - i[None, :]))
    xi = np.broadcast_to(xi[:, :, None], (RET_HEADS, L, RET_DV))
    zeta = np.broadcast_to(zeta[:, :, None], (RET_HEADS, L, RET_DK))
    return (jnp.asarray(dmask, F32), jnp.asarray(xi, F32), jnp.asarray(zeta, F32))


def retention(proj, r0, *, batch, seq, L, tb):
    n = batch * seq
    nt = seq // tb
    dmask, xi, zeta = _retention_tables(L)
    dqk = RET_HEADS * RET_DK
    dv = RET_HEADS * RET_DV
    has_r0 = r0 is not None
    const3 = lambda b, t: (0, 0, 0)
    in_specs = [
        pl.BlockSpec((tb, dqk), lambda b, t: (b * nt + t, 0)),
        pl.BlockSpec((tb, dqk), lambda b, t: (b * nt + t, 1)),
        pl.BlockSpec((tb, dv), lambda b, t: (b * nt + t, 1)),
        pl.BlockSpec((tb, dv), lambda b, t: (b * nt + t, 2)),
        pl.BlockSpec((RET_HEADS, L, L), const3),
        pl.BlockSpec((RET_HEADS, L, RET_DV), const3),
        pl.BlockSpec((RET_HEADS, L, RET_DK), const3),
    ]
    args = [proj, proj, proj, proj, dmask, xi, zeta]
    state_spec = pl.BlockSpec((1, RET_HEADS, RET_DK, RET_DV), lambda b, t: (b, 0, 0, 0))
    if has_r0:
        in_specs.append(state_spec)
        args.append(r0)
    return pl.pallas_call(
        functools.partial(_retention_kernel, L=L, n_chunks=tb // L, has_r0=has_r0),
        grid=(batch, nt),
        in_specs=in_specs,
        out_specs=[pl.BlockSpec((tb, dv), lambda b, t: (b * nt + t, 0)), state_spec],
        out_shape=[jax.ShapeDtypeStruct((n, dv), BF16),
                   jax.ShapeDtypeStruct((batch, RET_HEADS, RET_DK, RET_DV), F32)],
        scratch_shapes=[pltpu.VMEM((RET_HEADS, RET_DK, RET_DV), F32)],
        compiler_params=_cparams(("parallel", "arbitrary")),
        name="retention",
    )(*args)


def _route(u, wr_hi, wr_lo, br):
    u_hi = u.astype(BF16)
    u_lo = (u - u_hi.astype(F32)).astype(BF16)
    logits = (jnp.dot(u_hi, wr_hi, preferred_element_type=F32)
              + jnp.dot(u_lo, wr_hi, preferred_element_type=F32)
              + jnp.dot(u_hi, wr_lo, preferred_element_type=F32)) + br
    lane_i = lax.broadcasted_iota(jnp.int32, logits.shape, 1)
    lane = lane_i.astype(F32)
    lane_grp = (lane_i // EXPERTS_PER_GROUP).astype(F32)
    big = jnp.float32(1 << 20)
    is_grp = jnp.logical_and(lane_i >= GROUP_LANE0, lane_i < GROUP_LANE0 + N_GROUPS)
    gl = jnp.where(is_grp, logits, -jnp.inf)
    gmax = jnp.max(gl, axis=-1, keepdims=True)
    gsum = jnp.sum(jnp.exp(gl - gmax), axis=-1, keepdims=True)
    p_top = 1.0 / gsum
    g_top = jnp.min(jnp.where(gl == gmax, lane, big), axis=-1, keepdims=True) - GROUP_LANE0
    sel = jnp.logical_and(lane_i < N_EXPERTS, lane_grp == g_top)
    el = jnp.where(sel, logits, -jnp.inf)
    emax = jnp.max(el, axis=-1, keepdims=True)
    ee = jnp.exp(el - emax)
    ep = ee / jnp.sum(ee, axis=-1, keepdims=True)
    epm = jnp.where(sel, ep, -1.0)
    m1 = jnp.max(epm, axis=-1, keepdims=True)
    i1 = jnp.min(jnp.where(epm == m1, lane, big), axis=-1, keepdims=True)
    epm2 = jnp.where(lane == i1, -1.0, epm)
    m2 = jnp.max(epm2, axis=-1, keepdims=True)
    i2 = jnp.min(jnp.where(epm2 == m2, lane, big), axis=-1, keepdims=True)
    denom = m1 + m2
    w1 = m1 / denom * p_top
    w2 = m2 / denom * p_top
    gate = jnp.where(lane == i1, w1, jnp.where(lane == i2, w2, 0.0))
    return jnp.where(lane_i == GROUP_ID_LANE, g_top, gate)


def _outproj_kernel(a_ref, w_ref, res_ref, g_ref, wrh_ref, wrl_ref, br_ref, hx_ref, *, rc):
    d = w_ref.shape[1]
    for c in range(a_ref.shape[0] // rc):
        rows = slice(c * rc, (c + 1) * rc)
        h = res_ref[rows, :] + jnp.dot(a_ref[rows, :], w_ref[...], preferred_element_type=F32)
        hx_ref[rows, :d] = h
        hx_ref[rows, d:] = _route(_rms(h, g_ref[...]), wrh_ref[...], wrl_ref[...], br_ref[...])


def outproj_route(a, w, res, g, wr, br, *, tm):
    n, kdim = a.shape
    d = w.shape[1]
    wr_hi, wr_lo = wr
    return pl.pallas_call(
        functools.partial(_outproj_kernel, rc=min(tm, 256)),
        grid=(n // tm,),
        in_specs=[
            pl.BlockSpec((tm, kdim), lambda i: (i, 0)),
            pl.BlockSpec((kdim, d), lambda i: (0, 0)),
            pl.BlockSpec((tm, d), lambda i: (i, 0)),
            pl.BlockSpec((1, d), lambda i: (0, 0)),
            pl.BlockSpec((d, ROUTE_LANES), lambda i: (0, 0)),
            pl.BlockSpec((d, ROUTE_LANES), lambda i: (0, 0)),
            pl.BlockSpec((1, ROUTE_LANES), lambda i: (0, 0)),
        ],
        out_specs=pl.BlockSpec((tm, HX_WIDTH), lambda i: (i, 0)),
        out_shape=jax.ShapeDtypeStruct((n, HX_WIDTH), F32),
        compiler_params=_cparams(("parallel",)),
        name="outproj_route",
    )(a, w, res, g, wr_hi, wr_lo, br)


def _group_experts(u, gate, grp, wg_ref, wu_ref, wd_ref):
    lane = lax.broadcasted_iota(jnp.int32, gate.shape, 1)
    contrib = None
    for e in range(EXPERTS_PER_GROUP):
        ge = jnp.sum(jnp.where(lane == grp * EXPERTS_PER_GROUP + e, gate, 0.0), axis=-1, keepdims=True)
        hg = jnp.dot(u, wg_ref[e], preferred_element_type=F32)
        hu = jnp.dot(u, wu_ref[e], preferred_element_type=F32)
        hid = (_silu(hg) * hu * ge).astype(BF16)
        d = jnp.dot(hid, wd_ref[e], preferred_element_type=F32)
        contrib = d if contrib is None else contrib + d
    return contrib


def _moe_kernel(*refs, sorted_rows, final_norm):
    if sorted_rows:
        tile_ref, group_ref, lo_ref, hi_ref, first_ref, last_ref = refs[:6]
        refs = refs[6:]
    if final_norm:
        hx_ref, gffn_ref, wg_ref, wu_ref, wd_ref, gfin_ref, out_ref, u_ref, acc_ref = refs
    else:
        hx_ref, gffn_ref, wg_ref, wu_ref, wd_ref, out_ref, u_ref, acc_ref = refs
    d = out_ref.shape[1]
    if sorted_rows:
        s = pl.program_id(0)
        grp = group_ref[s]
        is_first = first_ref[s] == 1
        is_last = last_ref[s] == 1
        lo = lo_ref[s]
        hi = hi_ref[s]
    else:
        grp = pl.program_id(1)
        is_first = grp == 0
        is_last = grp == pl.num_programs(1) - 1

    @pl.when(is_first)
    def _():
        u_ref[...] = _rms(hx_ref[:, :d], gffn_ref[...]).astype(BF16)
        acc_ref[...] = jnp.zeros_like(acc_ref)

    def accumulate():
        acc_ref[...] += _group_experts(u_ref[...], hx_ref[:, d:], grp, wg_ref, wu_ref, wd_ref)

    if sorted_rows:
        pl.when(hi > lo)(accumulate)
    else:
        accumulate()

    @pl.when(is_last)
    def _():
        h = hx_ref[:, :d] + acc_ref[...]
        out_ref[...] = _rms(h, gfin_ref[...]) if final_norm else h


def moe(hx, g_ffn, wg, wu, wd, g_final=None, items=None, *, tm):
    n = hx.shape[0]
    d = D_MODEL
    sorted_rows = items is not None
    final_norm = g_final is not None
    if sorted_rows:
        tile_of = lambda s, tile, group, *_: (tile[s], 0)
        group_of = lambda s, tile, group, *_: (group[s], 0, 0)
        const = lambda s, *_: (0, 0)
        grid = (items[0].shape[0],)
    else:
        tile_of = lambda i, g: (i, 0)
        group_of = lambda i, g: (g, 0, 0)
        const = lambda i, g: (0, 0)
        grid = (n // tm, N_GROUPS)
    in_specs = [
        pl.BlockSpec((tm, HX_WIDTH), tile_of),
        pl.BlockSpec((1, d), const),
        pl.BlockSpec((EXPERTS_PER_GROUP, d, D_EXPERT), group_of),
        pl.BlockSpec((EXPERTS_PER_GROUP, d, D_EXPERT), group_of),
        pl.BlockSpec((EXPERTS_PER_GROUP, D_EXPERT, d), group_of),
    ]
    args = [hx, g_ffn, wg, wu, wd]
    if final_norm:
        in_specs.append(pl.BlockSpec((1, d), const))
        args.append(g_final)
    return pl.pallas_call(
        functools.partial(_moe_kernel, sorted_rows=sorted_rows, final_norm=final_norm),
        grid_spec=pltpu.PrefetchScalarGridSpec(
            num_scalar_prefetch=6 if sorted_rows else 0,
            grid=grid,
            in_specs=in_specs,
            out_specs=pl.BlockSpec((tm, d), tile_of),
            scratch_shapes=[pltpu.VMEM((tm, d), BF16), pltpu.VMEM((tm, d), F32)],
        ),
        out_shape=jax.ShapeDtypeStruct((n, d), F32),
        compiler_params=_cparams(("arbitrary",) if sorted_rows else ("parallel", "arbitrary")),
        name="moe",
    )(*(tuple(items) if sorted_rows else ()), *args)


def _permute_kernel(idx_ref, src_ref, dst_ref, sem, *, ch, scatter):
    base = pl.program_id(0) * ch

    def row_copy(r):
        t = base + r
        j = idx_ref[t]
        if scatter:
            return pltpu.make_async_copy(src_ref.at[pl.ds(t, 1), :], dst_ref.at[pl.ds(j, 1), :], sem)
        return pltpu.make_async_copy(src_ref.at[pl.ds(j, 1), :], dst_ref.at[pl.ds(t, 1), :], sem)

    def issue(r, carry):
        row_copy(r).start()
        return carry

    lax.fori_loop(0, ch, issue, 0, unroll=8)
    rows = pl.ds(base, ch)
    pltpu.make_async_copy(src_ref.at[rows, :], dst_ref.at[rows, :], sem).wait()


def permute_rows(src, idx, *, scatter, ch=2048):
    n, width = src.shape
    ch = min(ch, n)
    return pl.pallas_call(
        functools.partial(_permute_kernel, ch=ch, scatter=scatter),
        grid_spec=pltpu.PrefetchScalarGridSpec(
            num_scalar_prefetch=1,
            grid=(n // ch,),
            in_specs=[pl.BlockSpec(memory_space=pl.ANY)],
            out_specs=pl.BlockSpec(memory_space=pl.ANY),
            scratch_shapes=[pltpu.SemaphoreType.DMA(())],
        ),
        out_shape=jax.ShapeDtypeStruct((n, width), src.dtype),
        compiler_params=_cparams(("arbitrary",)),
        name="permute_rows",
    )(idx, src)


def _rope_cols(y, cosf, sina, sinb):
    blocks = []
    for c in range(y.shape[1] // LANES):
        xb = y[:, c * LANES:(c + 1) * LANES]
        blocks.append(xb * cosf + pltpu.roll(xb, LANES - ROPE_DIM // 2, 1) * sina
                      + pltpu.roll(xb, ROPE_DIM // 2, 1) * sinb)
    return blocks


def _rope_rows(kt, cost, sint):
    half = ROPE_DIM // 2
    parts = []
    for g in range(kt.shape[0] // DIFF_DH):
        b0 = g * DIFF_DH
        x1 = kt[b0:b0 + half]
        x2 = kt[b0 + half:b0 + ROPE_DIM]
        parts += [x1 * cost - x2 * sint, x2 * cost + x1 * sint, kt[b0 + ROPE_DIM:b0 + DIFF_DH]]
    return jnp.concatenate(parts, axis=0)


def _qkv_kernel(*refs, tm, k_feature_major):
    if k_feature_major:
        (h_ref, gn_ref, wk_ref, wv_ref, wq_ref, cos_ref, sa_ref, sb_ref, cost_ref, sint_ref,
         k32_ref, v32_ref, kb_ref, vb_ref, qb_ref) = refs
    else:
        (h_ref, gn_ref, wk_ref, wv_ref, wq_ref, cos_ref, sa_ref, sb_ref,
         k32_ref, v32_ref, kb_ref, vb_ref, qb_ref) = refs
    cosf = cos_ref[...]
    sina = sa_ref[...]
    sinb = sb_ref[...]
    h = h_ref[...]
    inv = lax.rsqrt(jnp.mean(h * h, axis=-1, keepdims=True) + EPS)
    ukv = (h * inv * gn_ref[0:1, :]).astype(BF16)
    umix = (h * inv * gn_ref[1:2, :]).astype(BF16)
    if k_feature_major:
        kt = lax.dot_general(wk_ref[...], ukv, (((1,), (1,)), ((), ())), preferred_element_type=F32)
        kt = _rope_rows(kt, cost_ref[...], sint_ref[...])
        k32_ref[0] = kt
        kb_ref[0, 0] = kt.astype(BF16)
    else:
        k = jnp.dot(ukv, wk_ref[...], preferred_element_type=F32)
        for c, blk in enumerate(_rope_cols(k, cosf, sina, sinb)):
            k32_ref[:, c * LANES:(c + 1) * LANES] = blk
            kb_ref[:, c * LANES:(c + 1) * LANES] = blk.astype(BF16)
    v = jnp.dot(ukv, wv_ref[...], preferred_element_type=F32)
    vb_ref[...] = v.astype(BF16)
    for h in range(DIFF_HEADS):
        v32_ref[pl.ds(h, tm, stride=DIFF_HEADS), :] = v[:, h * DIFF_DV:(h + 1) * DIFF_DV]
    q = jnp.dot(umix, wq_ref[...], preferred_element_type=F32)
    for c, blk in enumerate(_rope_cols(q, cosf, sina, sinb)):
        qb_ref[:, c * LANES:(c + 1) * LANES] = (blk * QK_SCALE_LOG2).astype(BF16)


def qkv_proj(h, gains, wk, wv, wq, cosf, sina, sinb, cost=None, sint=None, *, tm, batch, seq):
    n, d = h.shape
    k_feature_major = cost is not None
    nt = max(seq // tm, 1)
    n_pos_tiles = cosf.shape[0] // tm
    tok = lambda i: (i, 0)
    wspec = pl.BlockSpec((d, d), lambda i: (0, 0))
    tspec = pl.BlockSpec((tm, LANES), lambda i: (i % n_pos_tiles, 0))
    in_specs = [pl.BlockSpec((tm, d), tok), pl.BlockSpec((2, d), lambda i: (0, 0)), wspec, wspec, wspec,
                tspec, tspec, tspec]
    args = [h, gains, wk, wv, wq, cosf, sina, sinb]
    rows_f32 = (pl.BlockSpec((tm, d), tok), jax.ShapeDtypeStruct((n, d), F32))
    rows_bf16 = (pl.BlockSpec((tm, d), tok), jax.ShapeDtypeStruct((n, d), BF16))
    v32 = (pl.BlockSpec((tm * DIFF_HEADS, DIFF_DV), tok), jax.ShapeDtypeStruct((n * DIFF_HEADS, DIFF_DV), F32))
    if k_feature_major:
        half = ROPE_DIM // 2
        rspec = pl.BlockSpec((half, tm), lambda i: (0, i % nt))
        in_specs += [rspec, rspec]
        args += [cost, sint]
        k32 = (pl.BlockSpec((1, d, tm), lambda i: (i // nt, 0, i % nt)),
               jax.ShapeDtypeStruct((batch, d, seq), F32))
        kb = (pl.BlockSpec((1, 1, d, tm), lambda i: (i // nt, i % nt, 0, 0)),
              jax.ShapeDtypeStruct((batch, nt, d, tm), BF16))
    else:
        k32, kb = rows_f32, rows_bf16
    outs = [k32, v32, kb, rows_bf16, rows_bf16]
    return pl.pallas_call(
        functools.partial(_qkv_kernel, tm=tm, k_feature_major=k_feature_major),
        grid=(n // tm,),
        in_specs=in_specs,
        out_specs=[o[0] for o in outs],
        out_shape=[o[1] for o in outs],
        compiler_params=_cparams(("parallel",)),
        name="qkv_proj",
    )(*args)


def _lambda(lam_ref, lam_init):
    lv = lam_ref[...]
    a = jnp.sum(lv[0:1, :] * lv[1:2, :], axis=-1, keepdims=True)
    b = jnp.sum(lv[2:3, :] * lv[3:4, :], axis=-1, keepdims=True)
    return jnp.exp(a) - jnp.exp(b) + lam_init


def _split_q(q):
    qf = q.astype(F32)
    lane = lax.broadcasted_iota(jnp.int32, qf.shape, 1)
    q0 = jnp.where(lane < DIFF_DH, qf, 0.0).astype(BF16)
    q1 = jnp.where(lane >= DIFF_DH, qf, 0.0).astype(BF16)
    return jnp.concatenate([q0, q1], axis=0)


def _online_update(s, v, m_ref, l_ref, acc_ref):
    m_old = m_ref[...]
    if s.shape[1] < LANES:
        m_new = jnp.maximum(m_old, jnp.max(s, axis=-1, keepdims=True))
        alpha = jnp.exp2(m_old - m_new)
        p = jnp.exp2(s - m_new[:, :s.shape[1]])
        lane = lax.broadcasted_iota(jnp.int32, m_old.shape, 1)
        lsum = jnp.where(lane == 0, jnp.sum(p, axis=-1, keepdims=True), 0.0)
        l_ref[...] = alpha * l_ref[...] + lsum
        acc_ref[...] = alpha * acc_ref[...] + jnp.dot(p.astype(BF16), v, preferred_element_type=F32)
        m_ref[...] = m_new
        return
    ncol = s.shape[1] // LANES
    cols = [s[:, c * LANES:(c + 1) * LANES] for c in range(ncol)]
    mx = cols[0]
    for c in range(1, ncol):
        mx = jnp.maximum(mx, cols[c])
    m_new = jnp.maximum(m_old, jnp.max(mx, axis=-1, keepdims=True))
    alpha = jnp.exp2(m_old - m_new)
    ps = [jnp.exp2(col - m_new) for col in cols]
    lsum = ps[0]
    for c in range(1, ncol):
        lsum = lsum + ps[c]
    p = jnp.concatenate([pc.astype(BF16) for pc in ps], axis=1) if ncol > 1 else ps[0].astype(BF16)
    l_ref[...] = alpha * l_ref[...] + lsum
    acc_ref[...] = alpha * acc_ref[...] + jnp.dot(p, v, preferred_element_type=F32)
    m_ref[...] = m_new


def _diff_finish(l_ref, acc_ref, lam, subln, lam_init, tq):
    o = acc_ref[...] / jnp.sum(l_ref[...], axis=-1, keepdims=True)
    a = o[:tq] - lam * o[tq:]
    return _rms(a, subln) * (1.0 - lam_init)


def _attn_prompt_kernel(q_ref, k_ref, v_ref, lam_ref, sub_ref, o_ref, q2_ref, m_ref, l_ref, acc_ref,
                        *, tq, kblk, qg, lam_init):
    i = pl.program_id(2)
    ng = tq // qg
    for g in range(ng):
        q2_ref[2 * g * qg:2 * (g + 1) * qg, :] = _split_q(q_ref[g * qg:(g + 1) * qg, :])
    m_ref[...] = jnp.full_like(m_ref, NEG_BIG)
    l_ref[...] = jnp.zeros_like(l_ref)
    acc_ref[...] = jnp.zeros_like(acc_ref)

    def step(j, nkeys, rows, mask_group=None):
        r0 = pl.multiple_of(j * kblk, kblk)
        v = v_ref[pl.ds(r0, nkeys), :]
        q2 = q2_ref[rows, :]
        parts = []
        for t in range(-(-nkeys // kblk)):
            kt = k_ref[0, j + t]
            if (t + 1) * kblk > nkeys:
                kt = kt[:, :nkeys - t * kblk]
            parts.append(jnp.dot(q2, kt, preferred_element_type=F32))
        s = parts[0] if len(parts) == 1 else jnp.concatenate(parts, axis=1)
        if mask_group is not None:
            qrow = mask_group * qg + lax.broadcasted_iota(jnp.int32, s.shape, 0) % qg
            col = lax.broadcasted_iota(jnp.int32, s.shape, 1)
            s = jnp.where(col < (qrow // CHUNK + 1) * CHUNK, s, NEG_BIG)
        _online_update(s, v, m_ref.at[rows, :], l_ref.at[rows, :], acc_ref.at[rows, :])

    all_rows = slice(0, 2 * tq)
    nb = tq // kblk
    n_full = i * nb

    def body(jj, carry):
        step(2 * jj, 2 * kblk, all_rows)
        return carry

    lax.fori_loop(0, n_full // 2, body, 0)
    if nb % 2 == 1:
        @pl.when(n_full % 2 == 1)
        def _():
            step(n_full - 1, kblk, all_rows)

    lam = _lambda(lam_ref, lam_init)
    for g in range(ng):
        rows = slice(2 * g * qg, 2 * (g + 1) * qg)
        step(n_full, (g + 1) * qg, rows, mask_group=g)
        o_ref[g * qg:(g + 1) * qg, :] = _diff_finish(
            l_ref.at[rows, :], acc_ref.at[rows, :], lam, sub_ref[...], lam_init, qg).astype(BF16)


def attn_prompt(qb, kb, vb, lamv, subln, *, batch, seq, tq, kblk, lam_init):
    n = batch * seq
    nq = seq // tq
    qg = tq // 2
    return pl.pallas_call(
        functools.partial(_attn_prompt_kernel, tq=tq, kblk=kblk, qg=qg, lam_init=lam_init),
        grid=(batch, DIFF_HEADS, nq),
        in_specs=[
            pl.BlockSpec((tq, LANES), lambda b, h, i: (b * nq + i, h)),
            pl.BlockSpec((1, seq // kblk, LANES, kblk), lambda b, h, i: (b, 0, h, 0)),
            pl.BlockSpec((seq, LANES), lambda b, h, i: (b, h)),
            pl.BlockSpec((8, LANES), lambda b, h, i: (0, 0)),
            pl.BlockSpec((1, LANES), lambda b, h, i: (0, 0)),
        ],
        out_specs=pl.BlockSpec((tq, LANES), lambda b, h, i: (b * nq + i, h)),
        out_shape=jax.ShapeDtypeStruct((n, DIFF_HEADS * DIFF_DV), BF16),
        scratch_shapes=[pltpu.VMEM((2 * tq, LANES), BF16), pltpu.VMEM((2 * tq, LANES), F32),
                        pltpu.VMEM((2 * tq, LANES), F32), pltpu.VMEM((2 * tq, LANES), F32)],
        compiler_params=_cparams(("parallel", "parallel", "arbitrary")),
        name="attn_prompt",
    )(qb, kb, vb, lamv, subln)


def _attn_sample_kernel(q_ref, ck_ref, cv_ref, kn_ref, vn_ref, lam_ref, sub_ref, o_ref,
                        q2_ref, m_ref, l_ref, acc_ref, *, tq, past, lam_init):
    j = pl.program_id(1)
    last = pl.num_programs(1) - 1

    @pl.when(j == 0)
    def _():
        for h in range(DIFF_HEADS):
            q2_ref[h] = _split_q(q_ref[:, h * LANES:(h + 1) * LANES])
        m_ref[...] = jnp.full_like(m_ref, NEG_BIG)
        l_ref[...] = jnp.zeros_like(l_ref)
        acc_ref[...] = jnp.zeros_like(acc_ref)

    @pl.when(j < last)
    def _():
        tk = ck_ref.shape[-1]
        for h in range(DIFF_HEADS):
            kt = ck_ref[0, h].astype(BF16)
            v = cv_ref[pl.ds(h, tk, stride=DIFF_HEADS), :].astype(BF16)
            s = jnp.dot(q2_ref[h], kt, preferred_element_type=F32)
            _online_update(s, v, m_ref.at[h], l_ref.at[h], acc_ref.at[h])

    @pl.when(j == last)
    def _():
        lam = _lambda(lam_ref, lam_init)
        for h in range(DIFF_HEADS):
            k = kn_ref[:, h * LANES:(h + 1) * LANES]
            v = vn_ref[:, h * LANES:(h + 1) * LANES]
            s = lax.dot_general(q2_ref[h], k, (((1,), (1,)), ((), ())), preferred_element_type=F32)
            qpos = past + lax.broadcasted_iota(jnp.int32, s.shape, 0) % tq
            kpos = past + lax.broadcasted_iota(jnp.int32, s.shape, 1)
            s = jnp.where(kpos < (qpos // CHUNK + 1) * CHUNK, s, NEG_BIG)
            _online_update(s, v, m_ref.at[h], l_ref.at[h], acc_ref.at[h])
            o_ref[:, h * LANES:(h + 1) * LANES] = _diff_finish(
                l_ref.at[h], acc_ref.at[h], lam, sub_ref[...], lam_init, tq).astype(BF16)


def attn_sample(qb, cache_k, cache_v, kb, vb, lamv, subln, *, batch, seq, tk, lam_init):
    past = cache_k.shape[-1]
    d = DIFF_HEADS * DIFF_DV
    nk = past // tk
    ck_spec = pl.BlockSpec((1, DIFF_HEADS, LANES, tk), lambda b, j: (b, 0, 0, jnp.minimum(j, nk - 1)))
    cv_spec = pl.BlockSpec((tk * DIFF_HEADS, DIFF_DV), lambda b, j: (b * nk + jnp.minimum(j, nk - 1), 0))
    row_spec = pl.BlockSpec((seq, d), lambda b, j: (b, 0))
    return pl.pallas_call(
        functools.partial(_attn_sample_kernel, tq=seq, past=past, lam_init=lam_init),
        grid=(batch, nk + 1),
        in_specs=[row_spec, ck_spec, cv_spec, row_spec, row_spec,
                  pl.BlockSpec((8, LANES), lambda b, j: (0, 0)),
                  pl.BlockSpec((1, LANES), lambda b, j: (0, 0))],
        out_specs=row_spec,
        out_shape=jax.ShapeDtypeStruct((batch * seq, d), BF16),
        scratch_shapes=[pltpu.VMEM((DIFF_HEADS, 2 * seq, LANES), BF16),
                        pltpu.VMEM((DIFF_HEADS, 2 * seq, LANES), F32),
                        pltpu.VMEM((DIFF_HEADS, 2 * seq, LANES), F32),
                        pltpu.VMEM((DIFF_HEADS, 2 * seq, LANES), F32)],
        compiler_params=_cparams(("parallel", "arbitrary")),
        name="attn_sample",
    )(qb, cache_k, cache_v, kb, vb, lamv, subln)


def _angles(pos, half, theta):
    pos = np.asarray(pos).astype(np.float64)
    inv_freq = np.power(np.float64(theta), -np.arange(half, dtype=np.float64) / half)
    return pos[:, None] * inv_freq[None, :]


def _ret_rope_tables(pos):
    ang = _angles(pos, RET_DK // 2, RET_THETA)
    return np.cos(ang).astype(np.float32), np.sin(ang).astype(np.float32)


def _diff_rope_tables(pos):
    half = ROPE_DIM // 2
    ang = _angles(pos, half, ROPE_THETA)
    cos, sin = np.cos(ang).astype(np.float32), np.sin(ang).astype(np.float32)
    t = cos.shape[0]
    pad = DIFF_DH - ROPE_DIM
    cos64 = np.concatenate([cos, cos, np.ones((t, pad), np.float32)], axis=1)
    sina64 = np.concatenate([-sin, np.zeros((t, DIFF_DH - half), np.float32)], axis=1)
    sinb64 = np.concatenate([np.zeros((t, half), np.float32), sin, np.zeros((t, pad), np.float32)], axis=1)
    rep = LANES // DIFF_DH
    return (np.tile(cos64, (1, rep)), np.tile(sina64, (1, rep)), np.tile(sinb64, (1, rep)),
            np.ascontiguousarray(cos.T), np.ascontiguousarray(sin.T))


def _prep_params(p):
    w = {}
    w['ret_w_in'] = p['ret_w_in'][0].astype(BF16)
    w['ret_w_out'] = p['ret_w_out'][0].astype(BF16)
    w['kv_w_k'] = p['kv_w_k'].astype(BF16)
    w['kv_w_k_t'] = p['kv_w_k'].T.astype(BF16)
    w['kv_w_v'] = p['kv_w_v'].astype(BF16)
    w['diff_w_q'] = p['diff_w_q'][0].astype(BF16)
    w['diff_w_o'] = p['diff_w_o'][0].astype(BF16)
    w['moe_w_gate'] = p['moe_w_gate'].astype(BF16)
    w['moe_w_up'] = p['moe_w_up'].astype(BF16)
    w['moe_w_down'] = p['moe_w_down'].astype(BF16)
    wr, br = [], []
    for layer in range(2):
        route = jnp.transpose(p['moe_w_route'][layer], (1, 0, 2)).reshape(D_MODEL, N_EXPERTS)
        cols = jnp.concatenate([route, p['moe_w_group'][layer]], axis=1)
        full = jnp.pad(cols, ((0, 0), (0, ROUTE_LANES - cols.shape[1])))
        hi = full.astype(BF16)
        wr.append((hi, (full - hi.astype(F32)).astype(BF16)))
        bias = jnp.concatenate([p['moe_b_route'][layer].reshape(-1), p['moe_b_group'][layer]])
        br.append(jnp.pad(bias, (0, ROUTE_LANES - bias.shape[0]))[None, :])
    w['route_w'] = wr
    w['route_b'] = br
    lamv = jnp.concatenate([p['diff_lam_q1'][0][None], p['diff_lam_k1'][0][None],
                            p['diff_lam_q2'][0][None], p['diff_lam_k2'][0][None]], axis=0)
    w['lamv'] = jnp.pad(lamv, ((0, 4), (0, LANES - DIFF_DH)))
    return w


def _group_sort_tables(hx, tm):
    n = hx.shape[0]
    n_tiles = n // tm
    i32 = jnp.int32
    g = hx[:, D_MODEL + GROUP_ID_LANE].astype(i32)
    onehot = (g[:, None] == jnp.arange(N_GROUPS, dtype=i32)[None, :]).astype(i32)
    csum = jnp.cumsum(onehot, axis=0)
    ends = jnp.cumsum(csum[-1])
    starts = ends - csum[-1]
    pos = jnp.sum(onehot * (starts[None, :] + csum - 1), axis=1)
    inner_ends = ends[:N_GROUPS - 1]
    lo = jnp.sort(jnp.concatenate([jnp.arange(n_tiles, dtype=i32) * tm, inner_ends]))
    hi = jnp.concatenate([lo[1:], jnp.full((1,), n, i32)])
    tile = jnp.minimum(lo // tm, n_tiles - 1)
    group = jnp.sum((inner_ends[None, :] <= lo[:, None]).astype(i32), axis=1)
    change = (tile[1:] != tile[:-1]).astype(i32)
    one = jnp.ones((1,), i32)
    return pos, (tile, group, lo, hi, jnp.concatenate([one, change]), jnp.concatenate([change, one]))


def _moe_layer(hx, g_ffn, wg, wu, wd, g_final, *, tm):
    if hx.shape[0] < 8 * tm:
        return moe(hx, g_ffn, wg, wu, wd, g_final, tm=tm)
    pos, items = _group_sort_tables(hx, tm)
    xs = permute_rows(hx, pos, scatter=True)
    ys = moe(xs, g_ffn, wg, wu, wd, g_final, items, tm=tm)
    return permute_rows(ys, pos, scatter=False)


def _trunk(x, pos, r0, past_k, past_v, p, w, *, tm, ret_chunk, ret_tb, attn_tq=None, attn_tk=None):
    batch, seq, d = x.shape
    n = batch * seq
    xf = x.reshape(n, d)
    row = lambda v: v.reshape(1, -1)

    cos, sin = _ret_rope_tables(pos)
    if seq < tm:
        cos, sin = np.tile(cos, (tm // seq, 1)), np.tile(sin, (tm // seq, 1))
    proj = ret_inproj(xf, row(p['norm_mix'][0]), w['ret_w_in'], cos, sin, tm=min(2 * tm, seq, n) if seq >= tm else tm)
    o_gated, r_new = retention(proj, r0, batch=batch, seq=seq, L=ret_chunk, tb=ret_tb)
    hx1 = outproj_route(o_gated, w['ret_w_out'], xf, row(p['norm_ffn'][0]),
                        w['route_w'][0], w['route_b'][0], tm=tm)
    h2 = _moe_layer(hx1, row(p['norm_ffn'][0]), w['moe_w_gate'][0], w['moe_w_up'][0], w['moe_w_down'][0],
                    None, tm=tm)
    gn = jnp.stack([p['kv_norm'], p['norm_mix'][1]])

    cosf, sina, sinb, cost, sint = _diff_rope_tables(pos)
    if seq < tm:
        rep = (tm // seq, 1)
        cosf, sina, sinb = np.tile(cosf, rep), np.tile(sina, rep), np.tile(sinb, rep)
    lam_init = 0.8 - 0.6 * math.exp(-0.3 * 1)
    subln = row(p['diff_subln'][0])
    if past_k is None:
        assert attn_tq % tm == 0
        k32, v32, kb, vb, qb = qkv_proj(h2, gn, w['kv_w_k_t'], w['kv_w_v'], w['diff_w_q'],
                                        cosf, sina, sinb, cost, sint, tm=tm, batch=batch, seq=seq)
        attn = attn_prompt(qb, kb, vb, w['lamv'], subln, batch=batch, seq=seq, tq=attn_tq, kblk=tm,
                           lam_init=lam_init)
        k_out = jnp.transpose(k32.reshape(batch, DIFF_HEADS, 2, DIFF_DH, seq), (0, 4, 1, 2, 3))
    else:
        past = past_k.shape[1]
        k32, v32, kb, vb, qb = qkv_proj(h2, gn, w['kv_w_k'], w['kv_w_v'], w['diff_w_q'],
                                        cosf, sina, sinb, tm=tm, batch=batch, seq=seq)
        ck = jnp.transpose(past_k, (0, 2, 3, 4, 1)).reshape(batch, DIFF_HEADS, 2 * DIFF_DH, past)
        cv = past_v.reshape(batch * past * DIFF_HEADS, DIFF_DV)
        attn = attn_sample(qb, ck, cv, kb, vb, w['lamv'], subln, batch=batch, seq=seq, tk=attn_tk,
                           lam_init=lam_init)
        k_out = k32.reshape(batch, seq, DIFF_HEADS, 2, DIFF_DH)
    hx3 = outproj_route(attn, w['diff_w_o'], h2, row(p['norm_ffn'][1]),
                        w['route_w'][1], w['route_b'][1], tm=tm)
    y = _moe_layer(hx3, row(p['norm_ffn'][1]), w['moe_w_gate'][1], w['moe_w_up'][1], w['moe_w_down'][1],
                   row(p['norm_final']), tm=tm)

    return (y.reshape(batch, seq, d), r_new[None], k_out, v32.reshape(batch, seq, DIFF_HEADS, DIFF_DV))


def kernel(x_prompt, x_sample, state_ret, cache_k, cache_v, norm_mix, norm_ffn, norm_final, ret_w_in, ret_w_out, kv_norm, kv_w_k, kv_w_v, diff_w_q, diff_lam_q1, diff_lam_k1, diff_lam_q2, diff_lam_k2, diff_subln, diff_w_o, moe_w_group, moe_b_group, moe_w_route, moe_b_route, moe_w_gate, moe_w_up, moe_w_down):
    p = {
        'norm_mix': norm_mix, 'norm_ffn': norm_ffn, 'norm_final': norm_final,
        'ret_w_in': ret_w_in, 'ret_w_out': ret_w_out,
        'kv_norm': kv_norm, 'kv_w_k': kv_w_k, 'kv_w_v': kv_w_v,
        'diff_w_q': diff_w_q, 'diff_lam_q1': diff_lam_q1, 'diff_lam_k1': diff_lam_k1,
        'diff_lam_q2': diff_lam_q2, 'diff_lam_k2': diff_lam_k2, 'diff_subln': diff_subln,
        'diff_w_o': diff_w_o,
        'moe_w_group': moe_w_group, 'moe_b_group': moe_b_group, 'moe_w_route': moe_w_route,
        'moe_b_route': moe_b_route, 'moe_w_gate': moe_w_gate, 'moe_w_up': moe_w_up,
        'moe_w_down': moe_w_down,
    }
    w = _prep_params(p)
    tp = x_prompt.shape[1]
    ts = x_sample.shape[1]
    past = cache_k.shape[1]
    y_p, r_p, k_p, v_p = _trunk(x_prompt, np.arange(tp), None, None, None, p, w,
                                tm=512, ret_chunk=256, ret_tb=512, attn_tq=512)
    y_s, r_s, k_s, v_s = _trunk(x_sample, past + np.arange(ts), state_ret[0], cache_k, cache_v, p, w,
                                tm=x_sample.shape[0] * ts, ret_chunk=ts, ret_tb=ts, attn_tk=1024)
    return (y_p, y_s, r_p, k_p, v_p, r_s, k_s, v_s)
```

```python
import functools
import math

import numpy as np
import jax
import jax.numpy as jnp
from jax import lax
from jax.experimental import pallas as pl
from jax.experimental.pallas import tpu as pltpu

F32 = jnp.float32
BF16 = jnp.bfloat16

D_MODEL = 1024
CHUNK = 64
RET_HEADS = 4
RET_DK = 256
RET_DV = 512
RET_THETA = 10000.0
DIFF_HEADS = 8
DIFF_DH = 64
DIFF_DV = 128
ROPE_THETA = 500000.0
ROPE_DIM = 16
N_GROUPS = 4
EXPERTS_PER_GROUP = 4
N_EXPERTS = 16
D_EXPERT = 256
EPS = 1e-6

LANES = 128
VMEM_LIMIT = 56 * 1024 * 1024
ROUTE_LANES = 128
GROUP_LANE0 = N_EXPERTS
GROUP_ID_LANE = GROUP_LANE0 + N_GROUPS
HX_WIDTH = D_MODEL + ROUTE_LANES
NEG_BIG = -1e30
QK_SCALE_LOG2 = DIFF_DH ** -0.5 * math.log2(math.e)


def _cparams(sem):
    return pltpu.CompilerParams(dimension_semantics=sem, vmem_limit_bytes=VMEM_LIMIT)


def _rms(x, g):
    ms = jnp.mean(x * x, axis=-1, keepdims=True)
    return x * lax.rsqrt(ms + EPS) * g


def _silu(x):
    return x * (1.0 / (1.0 + jnp.exp(-x)))


def _ret_inproj_kernel(x_ref, g_ref, w_ref, cos_ref, sin_ref, o_ref, xn_ref, *, tn):
    j = pl.program_id(1)
    n_q = (RET_HEADS * RET_DK) // tn
    n_qk = 2 * n_q
    n_qkv = n_qk + (RET_HEADS * RET_DV) // tn

    @pl.when(j == 0)
    def _():
        xn_ref[...] = _rms(x_ref[...], g_ref[...]).astype(BF16)

    def proj():
        return jnp.dot(xn_ref[...], w_ref[...], preferred_element_type=F32)

    @pl.when(j < n_qk)
    def _():
        y = proj()
        cos = cos_ref[...]
        sin = sin_ref[...]
        scale = jnp.where(j >= n_q, RET_DK ** -0.5, 1.0).astype(F32)
        half = RET_DK // 2
        for h in range(tn // RET_DK):
            x1 = y[:, h * RET_DK:h * RET_DK + half]
            x2 = y[:, h * RET_DK + half:(h + 1) * RET_DK]
            o_ref[:, h * RET_DK:h * RET_DK + half] = ((x1 * cos - x2 * sin) * scale).astype(BF16)
            o_ref[:, h * RET_DK + half:(h + 1) * RET_DK] = ((x2 * cos + x1 * sin) * scale).astype(BF16)

    @pl.when(jnp.logical_and(j >= n_qk, j < n_qkv))
    def _():
        o_ref[...] = proj().astype(BF16)

    @pl.when(j >= n_qkv)
    def _():
        o_ref[...] = _silu(proj()).astype(BF16)


def ret_inproj(x, g, w, cos, sin, *, tm, tn=1024):
    n, d = x.shape
    m = w.shape[1]
    n_pos_tiles = cos.shape[0] // tm
    return pl.pallas_call(
        functools.partial(_ret_inproj_kernel, tn=tn),
        grid=(n // tm, m // tn),
        in_specs=[
            pl.BlockSpec((tm, d), lambda i, j: (i, 0)),
            pl.BlockSpec((1, d), lambda i, j: (0, 0)),
            pl.BlockSpec((d, tn), lambda i, j: (0, j)),
            pl.BlockSpec((tm, RET_DK // 2), lambda i, j: (i % n_pos_tiles, 0)),
            pl.BlockSpec((tm, RET_DK // 2), lambda i, j: (i % n_pos_tiles, 0)),
        ],
        out_specs=pl.BlockSpec((tm, tn), lambda i, j: (i, j)),
        out_shape=jax.ShapeDtypeStruct((n, m), BF16),
        scratch_shapes=[pltpu.VMEM((tm, d), BF16)],
        compiler_params=_cparams(("parallel", "arbitrary")),
        name="ret_inproj",
    )(x, g, w, cos, sin)


def _retention_kernel(*refs, L, n_chunks, has_r0):
    if has_r0:
        (q_ref, k_ref, v_ref, g_ref, dm_ref, xi_ref, zt_ref, r0_ref, o_ref, rout_ref, r_ref) = refs
    else:
        (q_ref, k_ref, v_ref, g_ref, dm_ref, xi_ref, zt_ref, o_ref, rout_ref, r_ref) = refs
    t = pl.program_id(1)

    @pl.when(t == 0)
    def _():
        if has_r0:
            r_ref[...] = r0_ref[0]
        else:
            r_ref[...] = jnp.zeros_like(r_ref)

    def chunk(c, carry):
        r0 = pl.multiple_of(c * L, L)
        for h in range(RET_HEADS):
            lg = math.log1p(-2.0 ** (-5.0 - h))
            q = q_ref[pl.ds(r0, L), h * RET_DK:(h + 1) * RET_DK]
            k = k_ref[pl.ds(r0, L), h * RET_DK:(h + 1) * RET_DK]
            v = v_ref[pl.ds(r0, L), h * RET_DV:(h + 1) * RET_DV]
            gate = g_ref[pl.ds(r0, L), h * RET_DV:(h + 1) * RET_DV]
            rh = r_ref[h]
            s = lax.dot_general(q, k, (((1,), (1,)), ((), ())), preferred_element_type=F32)
            s = s * dm_ref[h]
            intra = jnp.dot(s.astype(BF16), v, preferred_element_type=F32)
            inter = jnp.dot(q, rh.astype(BF16), preferred_element_type=F32) * xi_ref[h]
            o = intra + inter
            kz = (k.astype(F32) * zt_ref[h]).astype(BF16)
            upd = lax.dot_general(kz, v, (((0,), (0,)), ((), ())), preferred_element_type=F32)
            r_ref[h] = math.exp(lg * L) * rh + upd
            on = o * lax.rsqrt(jnp.mean(o * o, axis=-1, keepdims=True) + EPS)
            o_ref[pl.ds(r0, L), h * RET_DV:(h + 1) * RET_DV] = (on * gate.astype(F32)).astype(BF16)
        return carry

    lax.fori_loop(0, n_chunks, chunk, 0)

    @pl.when(t == pl.num_programs(1) - 1)
    def _():
        rout_ref[0] = r_ref[...]


def _retention_tables(L):
    lg = np.log1p(-np.exp2(-5.0 - np.arange(RET_HEADS, dtype=np.float64)))
    i = np.arange(L, dtype=np.float64)
    diff = i[:, None] - i[None, :]
    dmask = np.where(diff >= 0, np.exp(lg[:, None, None] * np.maximum(diff, 0.0)), 0.0)
    xi = np.exp(lg[:, None] * (i[None, :] + 1.0))
    zeta = np.exp(lg[:, None] * (L - 1.0 - i[None, :]))
    xi = np.broadcast_to(xi[:, :, None], (RET_HEADS, L, RET_DV))
    zeta = np.broadcast_to(zeta[:, :, None], (RET_HEADS, L, RET_DK))
    return (jnp.asarray(dmask, F32), jnp.asarray(xi, F32), jnp.asarray(zeta, F32))


def retention(proj, r0, *, batch, seq, L, tb):
    n = batch * seq
    nt = seq // tb
    dmask, xi, zeta = _retention_tables(L)
    dqk = RET_HEADS * RET_DK
    dv = RET_HEADS * RET_DV
    has_r0 = r0 is not None
    const3 = lambda b, t: (0, 0, 0)
    in_specs = [
        pl.BlockSpec((tb, dqk), lambda b, t: (b * nt + t, 0)),
        pl.BlockSpec((tb, dqk), lambda b, t: (b * nt + t, 1)),
        pl.BlockSpec((tb, dv), lambda b, t: (b * nt + t, 1)),
        pl.BlockSpec((tb, dv), lambda b, t: (b * nt + t, 2)),
        pl.BlockSpec((RET_HEADS, L, L), const3),
        pl.BlockSpec((RET_HEADS, L, RET_DV), const3),
        pl.BlockSpec((RET_HEADS, L, RET_DK), const3),
    ]
    args = [proj, proj, proj, proj, dmask, xi, zeta]
    state_spec = pl.BlockSpec((1, RET_HEADS, RET_DK, RET_DV), lambda b, t: (b, 0, 0, 0))
    if has_r0:
        in_specs.append(state_spec)
        args.append(r0)
    return pl.pallas_call(
        functools.partial(_retention_kernel, L=L, n_chunks=tb // L, has_r0=has_r0),
        grid=(batch, nt),
        in_specs=in_specs,
        out_specs=[pl.BlockSpec((tb, dv), lambda b, t: (b * nt + t, 0)), state_spec],
        out_shape=[jax.ShapeDtypeStruct((n, dv), BF16),
                   jax.ShapeDtypeStruct((batch, RET_HEADS, RET_DK, RET_DV), F32)],
        scratch_shapes=[pltpu.VMEM((RET_HEADS, RET_DK, RET_DV), F32)],
        compiler_params=_cparams(("parallel", "arbitrary")),
        name="retention",
    )(*args)


def _route(u, wr_hi, wr_lo, br):
    u_hi = u.astype(BF16)
    u_lo = (u - u_hi.astype(F32)).astype(BF16)
    logits = (jnp.dot(u_hi, wr_hi, preferred_element_type=F32)
              + jnp.dot(u_lo, wr_hi, preferred_element_type=F32)
              + jnp.dot(u_hi, wr_lo, preferred_element_type=F32)) + br
    lane_i = lax.broadcasted_iota(jnp.int32, logits.shape, 1)
    lane = lane_i.astype(F32)
    lane_grp = (lane_i // EXPERTS_PER_GROUP).astype(F32)
    big = jnp.float32(1 << 20)
    is_grp = jnp.logical_and(lane_i >= GROUP_LANE0, lane_i < GROUP_LANE0 + N_GROUPS)
    gl = jnp.where(is_grp, logits, -jnp.inf)
    gmax = jnp.max(gl, axis=-1, keepdims=True)
    gsum = jnp.sum(jnp.exp(gl - gmax), axis=-1, keepdims=True)
    p_top = 1.0 / gsum
    g_top = jnp.min(jnp.where(gl == gmax, lane, big), axis=-1, keepdims=True) - GROUP_LANE0
    sel = jnp.logical_and(lane_i < N_EXPERTS, lane_grp == g_top)
    el = jnp.where(sel, logits, -jnp.inf)
    emax = jnp.max(el, axis=-1, keepdims=True)
    ee = jnp.exp(el - emax)
    ep = ee / jnp.sum(ee, axis=-1, keepdims=True)
    epm = jnp.where(sel, ep, -1.0)
    m1 = jnp.max(epm, axis=-1, keepdims=True)
    i1 = jnp.min(jnp.where(epm == m1, lane, big), axis=-1, keepdims=True)
    epm2 = jnp.where(lane == i1, -1.0, epm)
    m2 = jnp.max(epm2, axis=-1, keepdims=True)
    i2 = jnp.min(jnp.where(epm2 == m2, lane, big), axis=-1, keepdims=True)
    denom = m1 + m2
    w1 = m1 / denom * p_top
    w2 = m2 / denom * p_top
    gate = jnp.where(lane == i1, w1, jnp.where(lane == i2, w2, 0.0))
    return jnp.where(lane_i == GROUP_ID_LANE, g_top, gate)


def _outproj_kernel(a_ref, w_ref, res_ref, g_ref, wrh_ref, wrl_ref, br_ref, hx_ref, *, rc):
    d = w_ref.shape[1]
    for c in range(a_ref.shape[0] // rc):
        rows = slice(c * rc, (c + 1) * rc)
        h = res_ref[rows, :] + jnp.dot(a_ref[rows, :], w_ref[...], preferred_element_type=F32)
        hx_ref[rows, :d] = h
        hx_ref[rows, d:] = _route(_rms(h, g_ref[...]), wrh_ref[...], wrl_ref[...], br_ref[...])


def outproj_route(a, w, res, g, wr, br, *, tm):
    n, kdim = a.shape
    d = w.shape[1]
    wr_hi, wr_lo = wr
    return pl.pallas_call(
        functools.partial(_outproj_kernel, rc=min(tm, 256)),
        grid=(n // tm,),
        in_specs=[
            pl.BlockSpec((tm, kdim), lambda i: (i, 0)),
            pl.BlockSpec((kdim, d), lambda i: (0, 0)),
            pl.BlockSpec((tm, d), lambda i: (i, 0)),
            pl.BlockSpec((1, d), lambda i: (0, 0)),
            pl.BlockSpec((d, ROUTE_LANES), lambda i: (0, 0)),
            pl.BlockSpec((d, ROUTE_LANES), lambda i: (0, 0)),
            pl.BlockSpec((1, ROUTE_LANES), lambda i: (0, 0)),
        ],
        out_specs=pl.BlockSpec((tm, HX_WIDTH), lambda i: (i, 0)),
        out_shape=jax.ShapeDtypeStruct((n, HX_WIDTH), F32),
        compiler_params=_cparams(("parallel",)),
        name="outproj_route",
    )(a, w, res, g, wr_hi, wr_lo, br)


def _group_experts(u, gate, grp, wg_ref, wu_ref, wd_ref):
    lane = lax.broadcasted_iota(jnp.int32, gate.shape, 1)
    contrib = None
    for e in range(EXPERTS_PER_GROUP):
        ge = jnp.sum(jnp.where(lane == grp * EXPERTS_PER_GROUP + e, gate, 0.0), axis=-1, keepdims=True)
        hg = jnp.dot(u, wg_ref[e], preferred_element_type=F32)
        hu = jnp.dot(u, wu_ref[e], preferred_element_type=F32)
        hid = (_silu(hg) * hu * ge).astype(BF16)
        d = jnp.dot(hid, wd_ref[e], preferred_element_type=F32)
        contrib = d if contrib is None else contrib + d
    return contrib


def _moe_kernel(*refs, sorted_rows, final_norm):
    if sorted_rows:
        tile_ref, group_ref, lo_ref, hi_ref, first_ref, last_ref = refs[:6]
        refs = refs[6:]
    if final_norm:
        hx_ref, gffn_ref, wg_ref, wu_ref, wd_ref, gfin_ref, out_ref, u_ref, acc_ref = refs
    else:
        hx_ref, gffn_ref, wg_ref, wu_ref, wd_ref, out_ref, u_ref, acc_ref = refs
    d = out_ref.shape[1]
    if sorted_rows:
        s = pl.program_id(0)
        grp = group_ref[s]
        is_first = first_ref[s] == 1
        is_last = last_ref[s] == 1
        lo = lo_ref[s]
        hi = hi_ref[s]
    else:
        grp = pl.program_id(1)
        is_first = grp == 0
        is_last = grp == pl.num_programs(1) - 1

    @pl.when(is_first)
    def _():
        u_ref[...] = _rms(hx_ref[:, :d], gffn_ref[...]).astype(BF16)
        acc_ref[...] = jnp.zeros_like(acc_ref)

    def accumulate():
        acc_ref[...] += _group_experts(u_ref[...], hx_ref[:, d:], grp, wg_ref, wu_ref, wd_ref)

    if sorted_rows:
        pl.when(hi > lo)(accumulate)
    else:
        accumulate()

    @pl.when(is_last)
    def _():
        h = hx_ref[:, :d] + acc_ref[...]
        out_ref[...] = _rms(h, gfin_ref[...]) if final_norm else h


def moe(hx, g_ffn, wg, wu, wd, g_final=None, items=None, *, tm):
    n = hx.shape[0]
    d = D_MODEL
    sorted_rows = items is not None
    final_norm = g_final is not None
    if sorted_rows:
        tile_of = lambda s, tile, group, *_: (tile[s], 0)
        group_of = lambda s, tile, group, *_: (group[s], 0, 0)
        const = lambda s, *_: (0, 0)
        grid = (items[0].shape[0],)
    else:
        tile_of = lambda i, g: (i, 0)
        group_of = lambda i, g: (g, 0, 0)
        const = lambda i, g: (0, 0)
        grid = (n // tm, N_GROUPS)
    in_specs = [
        pl.BlockSpec((tm, HX_WIDTH), tile_of),
        pl.BlockSpec((1, d), const),
        pl.BlockSpec((EXPERTS_PER_GROUP, d, D_EXPERT), group_of),
        pl.BlockSpec((EXPERTS_PER_GROUP, d, D_EXPERT), group_of),
        pl.BlockSpec((EXPERTS_PER_GROUP, D_EXPERT, d), group_of),
    ]
    args = [hx, g_ffn, wg, wu, wd]
    if final_norm:
        in_specs.append(pl.BlockSpec((1, d), const))
        args.append(g_final)
    return pl.pallas_call(
        functools.partial(_moe_kernel, sorted_rows=sorted_rows, final_norm=final_norm),
        grid_spec=pltpu.PrefetchScalarGridSpec(
            num_scalar_prefetch=6 if sorted_rows else 0,
            grid=grid,
            in_specs=in_specs,
            out_specs=pl.BlockSpec((tm, d), tile_of),
            scratch_shapes=[pltpu.VMEM((tm, d), BF16), pltpu.VMEM((tm, d), F32)],
        ),
        out_shape=jax.ShapeDtypeStruct((n, d), F32),
        compiler_params=_cparams(("arbitrary",) if sorted_rows else ("parallel", "arbitrary")),
        name="moe",
    )(*(tuple(items) if sorted_rows else ()), *args)


def _permute_kernel(idx_ref, src_ref, dst_ref, sem, *, ch, scatter):
    base = pl.program_id(0) * ch

    def row_copy(r):
        j = idx_ref[base + r]
        if scatter:
            return pltpu.make_async_copy(src_ref.at[pl.ds(r, 1), :], dst_ref.at[pl.ds(j, 1), :], sem)
        return pltpu.make_async_copy(src_ref.at[pl.ds(j, 1), :], dst_ref.at[pl.ds(r, 1), :], sem)

    def issue(r, carry):
        row_copy(r).start()
        return carry

    lax.fori_loop(0, ch, issue, 0, unroll=8)
    rows = pl.ds(base, ch)
    if scatter:
        pltpu.make_async_copy(src_ref, dst_ref.at[rows, :], sem).wait()
    else:
        pltpu.make_async_copy(src_ref.at[rows, :], dst_ref, sem).wait()


def permute_rows(src, idx, *, scatter, ch=2048):
    n, width = src.shape
    ch = min(ch, n)
    tile = pl.BlockSpec((ch, width), lambda i, idx: (i, 0))
    hbm = pl.BlockSpec(memory_space=pl.ANY)
    return pl.pallas_call(
        functools.partial(_permute_kernel, ch=ch, scatter=scatter),
        grid_spec=pltpu.PrefetchScalarGridSpec(
            num_scalar_prefetch=1,
            grid=(n // ch,),
            in_specs=[tile if scatter else hbm],
            out_specs=hbm if scatter else tile,
            scratch_shapes=[pltpu.SemaphoreType.DMA(())],
        ),
        out_shape=jax.ShapeDtypeStruct((n, width), src.dtype),
        compiler_params=_cparams(("arbitrary",)),
        name="permute_rows",
    )(idx, src)


def _rope_cols(y, cosf, sina, sinb):
    blocks = []
    for c in range(y.shape[1] // LANES):
        xb = y[:, c * LANES:(c + 1) * LANES]
        blocks.append(xb * cosf + pltpu.roll(xb, LANES - ROPE_DIM // 2, 1) * sina
                      + pltpu.roll(xb, ROPE_DIM // 2, 1) * sinb)
    return blocks


def _rope_rows(kt, cost, sint):
    half = ROPE_DIM // 2
    parts = []
    for g in range(kt.shape[0] // DIFF_DH):
        b0 = g * DIFF_DH
        x1 = kt[b0:b0 + half]
        x2 = kt[b0 + half:b0 + ROPE_DIM]
        parts += [x1 * cost - x2 * sint, x2 * cost + x1 * sint, kt[b0 + ROPE_DIM:b0 + DIFF_DH]]
    return jnp.concatenate(parts, axis=0)


def _qkv_kernel(*refs, tm, k_feature_major):
    if k_feature_major:
        (h_ref, gn_ref, wk_ref, wv_ref, wq_ref, cos_ref, sa_ref, sb_ref, cost_ref, sint_ref,
         k32_ref, v32_ref, kb_ref, vb_ref, qb_ref) = refs
    else:
        (h_ref, gn_ref, wk_ref, wv_ref, wq_ref, cos_ref, sa_ref, sb_ref,
         k32_ref, v32_ref, kb_ref, vb_ref, qb_ref) = refs
    cosf = cos_ref[...]
    sina = sa_ref[...]
    sinb = sb_ref[...]
    h = h_ref[...]
    inv = lax.rsqrt(jnp.mean(h * h, axis=-1, keepdims=True) + EPS)
    ukv = (h * inv * gn_ref[0:1, :]).astype(BF16)
    umix = (h * inv * gn_ref[1:2, :]).astype(BF16)
    if k_feature_major:
        kt = lax.dot_general(wk_ref[...], ukv, (((1,), (1,)), ((), ())), preferred_element_type=F32)
        kt = _rope_rows(kt, cost_ref[...], sint_ref[...])
        k32_ref[0] = kt
        kb_ref[0, 0] = kt.astype(BF16)
    else:
        k = jnp.dot(ukv, wk_ref[...], preferred_element_type=F32)
        for c, blk in enumerate(_rope_cols(k, cosf, sina, sinb)):
            k32_ref[:, c * LANES:(c + 1) * LANES] = blk
            kb_ref[:, c * LANES:(c + 1) * LANES] = blk.astype(BF16)
    v = jnp.dot(ukv, wv_ref[...], preferred_element_type=F32)
    vb_ref[...] = v.astype(BF16)
    for h in range(DIFF_HEADS):
        v32_ref[pl.ds(h, tm, stride=DIFF_HEADS), :] = v[:, h * DIFF_DV:(h + 1) * DIFF_DV]
    q = jnp.dot(umix, wq_ref[...], preferred_element_type=F32)
    for c, blk in enumerate(_rope_cols(q, cosf, sina, sinb)):
        qb_ref[:, c * LANES:(c + 1) * LANES] = (blk * QK_SCALE_LOG2).astype(BF16)


def qkv_proj(h, gains, wk, wv, wq, cosf, sina, sinb, cost=None, sint=None, *, tm, batch, seq):
    n, d = h.shape
    k_feature_major = cost is not None
    nt = max(seq // tm, 1)
    n_pos_tiles = cosf.shape[0] // tm
    tok = lambda i: (i, 0)
    wspec = pl.BlockSpec((d, d), lambda i: (0, 0))
    tspec = pl.BlockSpec((tm, LANES), lambda i: (i % n_pos_tiles, 0))
    in_specs = [pl.BlockSpec((tm, d), tok), pl.BlockSpec((2, d), lambda i: (0, 0)), wspec, wspec, wspec,
                tspec, tspec, tspec]
    args = [h, gains, wk, wv, wq, cosf, sina, sinb]
    rows_f32 = (pl.BlockSpec((tm, d), tok), jax.ShapeDtypeStruct((n, d), F32))
    rows_bf16 = (pl.BlockSpec((tm, d), tok), jax.ShapeDtypeStruct((n, d), BF16))
    v32 = (pl.BlockSpec((tm * DIFF_HEADS, DIFF_DV), tok), jax.ShapeDtypeStruct((n * DIFF_HEADS, DIFF_DV), F32))
    if k_feature_major:
        half = ROPE_DIM // 2
        rspec = pl.BlockSpec((half, tm), lambda i: (0, i % nt))
        in_specs += [rspec, rspec]
        args += [cost, sint]
        k32 = (pl.BlockSpec((1, d, tm), lambda i: (i // nt, 0, i % nt)),
               jax.ShapeDtypeStruct((batch, d, seq), F32))
        kb = (pl.BlockSpec((1, 1, d, tm), lambda i: (i // nt, i % nt, 0, 0)),
              jax.ShapeDtypeStruct((batch, nt, d, tm), BF16))
    else:
        k32, kb = rows_f32, rows_bf16
    outs = [k32, v32, kb, rows_bf16, rows_bf16]
    return pl.pallas_call(
        functools.partial(_qkv_kernel, tm=tm, k_feature_major=k_feature_major),
        grid=(n // tm,),
        in_specs=in_specs,
        out_specs=[o[0] for o in outs],
        out_shape=[o[1] for o in outs],
        compiler_params=_cparams(("parallel",)),
        name="qkv_proj",
    )(*args)


def _lambda(lam_ref, lam_init):
    lv = lam_ref[...]
    a = jnp.sum(lv[0:1, :] * lv[1:2, :], axis=-1, keepdims=True)
    b = jnp.sum(lv[2:3, :] * lv[3:4, :], axis=-1, keepdims=True)
    return jnp.exp(a) - jnp.exp(b) + lam_init


def _split_q(q):
    qf = q.astype(F32)
    lane = lax.broadcasted_iota(jnp.int32, qf.shape, 1)
    q0 = jnp.where(lane < DIFF_DH, qf, 0.0).astype(BF16)
    q1 = jnp.where(lane >= DIFF_DH, qf, 0.0).astype(BF16)
    return jnp.concatenate([q0, q1], axis=0)


def _online_update(s, v, m_ref, l_ref, acc_ref):
    m_old = m_ref[...]
    if s.shape[1] < LANES:
        m_new = jnp.maximum(m_old, jnp.max(s, axis=-1, keepdims=True))
        alpha = jnp.exp2(m_old - m_new)
        p = jnp.exp2(s - m_new[:, :s.shape[1]])
        lane = lax.broadcasted_iota(jnp.int32, m_old.shape, 1)
        lsum = jnp.where(lane == 0, jnp.sum(p, axis=-1, keepdims=True), 0.0)
        l_ref[...] = alpha * l_ref[...] + lsum
        acc_ref[...] = alpha * acc_ref[...] + jnp.dot(p.astype(BF16), v, preferred_element_type=F32)
        m_ref[...] = m_new
        return
    ncol = s.shape[1] // LANES
    cols = [s[:, c * LANES:(c + 1) * LANES] for c in range(ncol)]
    mx = cols[0]
    for c in range(1, ncol):
        mx = jnp.maximum(mx, cols[c])
    m_new = jnp.maximum(m_old, jnp.max(mx, axis=-1, keepdims=True))
    alpha = jnp.exp2(m_old - m_new)
    ps = [jnp.exp2(col - m_new) for col in cols]
    lsum = ps[0]
    for c in range(1, ncol):
        lsum = lsum + ps[c]
    p = jnp.concatenate([pc.astype(BF16) for pc in ps], axis=1) if ncol > 1 else ps[0].astype(BF16)
    l_ref[...] = alpha * l_ref[...] + lsum
    acc_ref[...] = alpha * acc_ref[...] + jnp.dot(p, v, preferred_element_type=F32)
    m_ref[...] = m_new


def _diff_finish(l_ref, acc_ref, lam, subln, lam_init, tq):
    o = acc_ref[...] / jnp.sum(l_ref[...], axis=-1, keepdims=True)
    a = o[:tq] - lam * o[tq:]
    return _rms(a, subln) * (1.0 - lam_init)


def _attn_prompt_kernel(q_ref, k_ref, v_ref, lam_ref, sub_ref, o_ref, q2_ref, m_ref, l_ref, acc_ref,
                        *, tq, kblk, qg, lam_init):
    i = pl.program_id(2)
    ng = tq // qg
    for g in range(ng):
        q2_ref[2 * g * qg:2 * (g + 1) * qg, :] = _split_q(q_ref[g * qg:(g + 1) * qg, :])
    m_ref[...] = jnp.full_like(m_ref, NEG_BIG)
    l_ref[...] = jnp.zeros_like(l_ref)
    acc_ref[...] = jnp.zeros_like(acc_ref)

    def step(j, nkeys, rows, mask_group=None):
        r0 = pl.multiple_of(j * kblk, kblk)
        v = v_ref[pl.ds(r0, nkeys), :]
        q2 = q2_ref[rows, :]
        parts = []
        for t in range(-(-nkeys // kblk)):
            kt = k_ref[0, j + t]
            if (t + 1) * kblk > nkeys:
                kt = kt[:, :nkeys - t * kblk]
            parts.append(jnp.dot(q2, kt, preferred_element_type=F32))
        s = parts[0] if len(parts) == 1 else jnp.concatenate(parts, axis=1)
        if mask_group is not None:
            qrow = mask_group * qg + lax.broadcasted_iota(jnp.int32, s.shape, 0) % qg
            col = lax.broadcasted_iota(jnp.int32, s.shape, 1)
            s = jnp.where(col < (qrow // CHUNK + 1) * CHUNK, s, NEG_BIG)
        _online_update(s, v, m_ref.at[rows, :], l_ref.at[rows, :], acc_ref.at[rows, :])

    all_rows = slice(0, 2 * tq)
    nb = tq // kblk
    n_full = i * nb

    def body(jj, carry):
        step(2 * jj, 2 * kblk, all_rows)
        return carry

    lax.fori_loop(0, n_full // 2, body, 0)
    if nb % 2 == 1:
        @pl.when(n_full % 2 == 1)
        def _():
            step(n_full - 1, kblk, all_rows)

    lam = _lambda(lam_ref, lam_init)
    for g in range(ng):
        rows = slice(2 * g * qg, 2 * (g + 1) * qg)
        step(n_full, (g + 1) * qg, rows, mask_group=g)
        o_ref[g * qg:(g + 1) * qg, :] = _diff_finish(
            l_ref.at[rows, :], acc_ref.at[rows, :], lam, sub_ref[...], lam_init, qg).astype(BF16)


def attn_prompt(qb, kb, vb, lamv, subln, *, batch, seq, tq, kblk, lam_init):
    n = batch * seq
    nq = seq // tq
    qg = tq // 2
    return pl.pallas_call(
        functools.partial(_attn_prompt_kernel, tq=tq, kblk=kblk, qg=qg, lam_init=lam_init),
        grid=(batch, DIFF_HEADS, nq),
        in_specs=[
            pl.BlockSpec((tq, LANES), lambda b, h, i: (b * nq + i, h)),
            pl.BlockSpec((1, seq // kblk, LANES, kblk), lambda b, h, i: (b, 0, h, 0)),
            pl.BlockSpec((seq, LANES), lambda b, h, i: (b, h)),
            pl.BlockSpec((8, LANES), lambda b, h, i: (0, 0)),
            pl.BlockSpec((1, LANES), lambda b, h, i: (0, 0)),
        ],
        out_specs=pl.BlockSpec((tq, LANES), lambda b, h, i: (b * nq + i, h)),
        out_shape=jax.ShapeDtypeStruct((n, DIFF_HEADS * DIFF_DV), BF16),
        scratch_shapes=[pltpu.VMEM((2 * tq, LANES), BF16), pltpu.VMEM((2 * tq, LANES), F32),
                        pltpu.VMEM((2 * tq, LANES), F32), pltpu.VMEM((2 * tq, LANES), F32)],
        compiler_params=_cparams(("parallel", "parallel", "arbitrary")),
        name="attn_prompt",
    )(qb, kb, vb, lamv, subln)


def _attn_sample_kernel(q_ref, ck_ref, cv_ref, kn_ref, vn_ref, lam_ref, sub_ref, o_ref,
                        q2_ref, m_ref, l_ref, acc_ref, *, tq, past, lam_init):
    j = pl.program_id(1)
    last = pl.num_programs(1) - 1

    @pl.when(j == 0)
    def _():
        for h in range(DIFF_HEADS):
            q2_ref[h] = _split_q(q_ref[:, h * LANES:(h + 1) * LANES])
        m_ref[...] = jnp.full_like(m_ref, NEG_BIG)
        l_ref[...] = jnp.zeros_like(l_ref)
        acc_ref[...] = jnp.zeros_like(acc_ref)

    @pl.when(j < last)
    def _():
        tk = ck_ref.shape[-1]
        for h in range(DIFF_HEADS):
            kt = ck_ref[0, h].astype(BF16)
            v = cv_ref[pl.ds(h, tk, stride=DIFF_HEADS), :].astype(BF16)
            s = jnp.dot(q2_ref[h], kt, preferred_element_type=F32)
            _online_update(s, v, m_ref.at[h], l_ref.at[h], acc_ref.at[h])

    @pl.when(j == last)
    def _():
        lam = _lambda(lam_ref, lam_init)
        for h in range(DIFF_HEADS):
            k = kn_ref[:, h * LANES:(h + 1) * LANES]
            v = vn_ref[:, h * LANES:(h + 1) * LANES]
            s = lax.dot_general(q2_ref[h], k, (((1,), (1,)), ((), ())), preferred_element_type=F32)
            qpos = past + lax.broadcasted_iota(jnp.int32, s.shape, 0) % tq
            kpos = past + lax.broadcasted_iota(jnp.int32, s.shape, 1)
            s = jnp.where(kpos < (qpos // CHUNK + 1) * CHUNK, s, NEG_BIG)
            _online_update(s, v, m_ref.at[h], l_ref.at[h], acc_ref.at[h])
            o_ref[:, h * LANES:(h + 1) * LANES] = _diff_finish(
                l_ref.at[h], acc_ref.at[h], lam, sub_ref[...], lam_init, tq).astype(BF16)


def attn_sample(qb, cache_k, cache_v, kb, vb, lamv, subln, *, batch, seq, tk, lam_init):
    past = cache_k.shape[-1]
    d = DIFF_HEADS * DIFF_DV
    nk = past // tk
    ck_spec = pl.BlockSpec((1, DIFF_HEADS, LANES, tk), lambda b, j: (b, 0, 0, jnp.minimum(j, nk - 1)))
    cv_spec = pl.BlockSpec((tk * DIFF_HEADS, DIFF_DV), lambda b, j: (b * nk + jnp.minimum(j, nk - 1), 0))
    row_spec = pl.BlockSpec((seq, d), lambda b, j: (b, 0))
    return pl.pallas_call(
        functools.partial(_attn_sample_kernel, tq=seq, past=past, lam_init=lam_init),
        grid=(batch, nk + 1),
        in_specs=[row_spec, ck_spec, cv_spec, row_spec, row_spec,
                  pl.BlockSpec((8, LANES), lambda b, j: (0, 0)),
                  pl.BlockSpec((1, LANES), lambda b, j: (0, 0))],
        out_specs=row_spec,
        out_shape=jax.ShapeDtypeStruct((batch * seq, d), BF16),
        scratch_shapes=[pltpu.VMEM((DIFF_HEADS, 2 * seq, LANES), BF16),
                        pltpu.VMEM((DIFF_HEADS, 2 * seq, LANES), F32),
                        pltpu.VMEM((DIFF_HEADS, 2 * seq, LANES), F32),
                        pltpu.VMEM((DIFF_HEADS, 2 * seq, LANES), F32)],
        compiler_params=_cparams(("parallel", "arbitrary")),
        name="attn_sample",
    )(qb, cache_k, cache_v, kb, vb, lamv, subln)


def _angles(pos, half, theta):
    pos = np.asarray(pos).astype(np.float64)
    inv_freq = np.power(np.float64(theta), -np.arange(half, dtype=np.float64) / half)
    return pos[:, None] * inv_freq[None, :]


def _ret_rope_tables(pos):
    ang = _angles(pos, RET_DK // 2, RET_THETA)
    return np.cos(ang).astype(np.float32), np.sin(ang).astype(np.float32)


def _diff_rope_tables(pos):
    half = ROPE_DIM // 2
    ang = _angles(pos, half, ROPE_THETA)
    cos, sin = np.cos(ang).astype(np.float32), np.sin(ang).astype(np.float32)
    t = cos.shape[0]
    pad = DIFF_DH - ROPE_DIM
    cos64 = np.concatenate([cos, cos, np.ones((t, pad), np.float32)], axis=1)
    sina64 = np.concatenate([-sin, np.zeros((t, DIFF_DH - half), np.float32)], axis=1)
    sinb64 = np.concatenate([np.zeros((t, half), np.float32), sin, np.zeros((t, pad), np.float32)], axis=1)
    rep = LANES // DIFF_DH
    return (np.tile(cos64, (1, rep)), np.tile(sina64, (1, rep)), np.tile(sinb64, (1, rep)),
            np.ascontiguousarray(cos.T), np.ascontiguousarray(sin.T))


def _prep_params(p):
    w = {}
    w['ret_w_in'] = p['ret_w_in'][0].astype(BF16)
    w['ret_w_out'] = p['ret_w_out'][0].astype(BF16)
    w['kv_w_k'] = p['kv_w_k'].astype(BF16)
    w['kv_w_k_t'] = p['kv_w_k'].T.astype(BF16)
    w['kv_w_v'] = p['kv_w_v'].astype(BF16)
    w['diff_w_q'] = p['diff_w_q'][0].astype(BF16)
    w['diff_w_o'] = p['diff_w_o'][0].astype(BF16)
    w['moe_w_gate'] = p['moe_w_gate'].astype(BF16)
    w['moe_w_up'] = p['moe_w_up'].astype(BF16)
    w['moe_w_down'] = p['moe_w_down'].astype(BF16)
    wr, br = [], []
    for layer in range(2):
        route = jnp.transpose(p['moe_w_route'][layer], (1, 0, 2)).reshape(D_MODEL, N_EXPERTS)
        cols = jnp.concatenate([route, p['moe_w_group'][layer]], axis=1)
        full = jnp.pad(cols, ((0, 0), (0, ROUTE_LANES - cols.shape[1])))
        hi = full.astype(BF16)
        wr.append((hi, (full - hi.astype(F32)).astype(BF16)))
        bias = jnp.concatenate([p['moe_b_route'][layer].reshape(-1), p['moe_b_group'][layer]])
        br.append(jnp.pad(bias, (0, ROUTE_LANES - bias.shape[0]))[None, :])
    w['route_w'] = wr
    w['route_b'] = br
    lamv = jnp.concatenate([p['diff_lam_q1'][0][None], p['diff_lam_k1'][0][None],
                            p['diff_lam_q2'][0][None], p['diff_lam_k2'][0][None]], axis=0)
    w['lamv'] = jnp.pad(lamv, ((0, 4), (0, LANES - DIFF_DH)))
    return w


def _group_sort_tables(hx, tm):
    n = hx.shape[0]
    n_tiles = n // tm
    i32 = jnp.int32
    g = hx[:, D_MODEL + GROUP_ID_LANE].astype(i32)
    onehot = (g[:, None] == jnp.arange(N_GROUPS, dtype=i32)[None, :]).astype(i32)
    csum = jnp.cumsum(onehot, axis=0)
    ends = jnp.cumsum(csum[-1])
    starts = ends - csum[-1]
    pos = jnp.sum(onehot * (starts[None, :] + csum - 1), axis=1)
    inner_ends = ends[:N_GROUPS - 1]
    lo = jnp.sort(jnp.concatenate([jnp.arange(n_tiles, dtype=i32) * tm, inner_ends]))
    hi = jnp.concatenate([lo[1:], jnp.full((1,), n, i32)])
    tile = jnp.minimum(lo // tm, n_tiles - 1)
    group = jnp.sum((inner_ends[None, :] <= lo[:, None]).astype(i32), axis=1)
    change = (tile[1:] != tile[:-1]).astype(i32)
    one = jnp.ones((1,), i32)
    return pos, (tile, group, lo, hi, jnp.concatenate([one, change]), jnp.concatenate([change, one]))


def _moe_layer(hx, g_ffn, wg, wu, wd, g_final, *, tm):
    if hx.shape[0] < 8 * tm:
        return moe(hx, g_ffn, wg, wu, wd, g_final, tm=tm)
    pos, items = _group_sort_tables(hx, tm)
    xs = permute_rows(hx, pos, scatter=True)
    ys = moe(xs, g_ffn, wg, wu, wd, g_final, items, tm=tm)
    return permute_rows(ys, pos, scatter=False)


def _trunk(x, pos, r0, past_k, past_v, p, w, *, tm, ret_chunk, ret_tb, attn_tq=None, attn_tk=None):
    batch, seq, d = x.shape
    n = batch * seq
    xf = x.reshape(n, d)
    row = lambda v: v.reshape(1, -1)

    cos, sin = _ret_rope_tables(pos)
    if seq < tm:
        cos, sin = np.tile(cos, (tm // seq, 1)), np.tile(sin, (tm // seq, 1))
    proj = ret_inproj(xf, row(p['norm_mix'][0]), w['ret_w_in'], cos, sin, tm=min(2 * tm, seq, n) if seq >= tm else tm)
    o_gated, r_new = retention(proj, r0, batch=batch, seq=seq, L=ret_chunk, tb=ret_tb)
    hx1 = outproj_route(o_gated, w['ret_w_out'], xf, row(p['norm_ffn'][0]),
                        w['route_w'][0], w['route_b'][0], tm=tm)
    h2 = _moe_layer(hx1, row(p['norm_ffn'][0]), w['moe_w_gate'][0], w['moe_w_up'][0], w['moe_w_down'][0],
                    None, tm=tm)
    gn = jnp.stack([p['kv_norm'], p['norm_mix'][1]])

    cosf, sina, sinb, cost, sint = _diff_rope_tables(pos)
    if seq < tm:
        rep = (tm // seq, 1)
        cosf, sina, sinb = np.tile(cosf, rep), np.tile(sina, rep), np.tile(sinb, rep)
    lam_init = 0.8 - 0.6 * math.exp(-0.3 * 1)
    subln = row(p['diff_subln'][0])
    if past_k is None:
        assert attn_tq % tm == 0
        k32, v32, kb, vb, qb = qkv_proj(h2, gn, w['kv_w_k_t'], w['kv_w_v'], w['diff_w_q'],
                                        cosf, sina, sinb, cost, sint, tm=tm, batch=batch, seq=seq)
        attn = attn_prompt(qb, kb, vb, w['lamv'], subln, batch=batch, seq=seq, tq=attn_tq, kblk=tm,
                           lam_init=lam_init)
        k_out = jnp.transpose(k32.reshape(batch, DIFF_HEADS, 2, DIFF_DH, seq), (0, 4, 1, 2, 3))
    else:
        past = past_k.shape[1]
        k32, v32, kb, vb, qb = qkv_proj(h2, gn, w['kv_w_k'], w['kv_w_v'], w['diff_w_q'],
                                        cosf, sina, sinb, tm=tm, batch=batch, seq=seq)
        ck = jnp.transpose(past_k, (0, 2, 3, 4, 1)).reshape(batch, DIFF_HEADS, 2 * DIFF_DH, past)
        cv = past_v.reshape(batch * past * DIFF_HEADS, DIFF_DV)
        attn = attn_sample(qb, ck, cv, kb, vb, w['lamv'], subln, batch=batch, seq=seq, tk=attn_tk,
                           lam_init=lam_init)
        k_out = k32.reshape(batch, seq, DIFF_HEADS, 2, DIFF_DH)
    hx3 = outproj_route(attn, w['diff_w_o'], h2, row(p['norm_ffn'][1]),
                        w['route_w'][1], w['route_b'][1], tm=tm)
    y = _moe_layer(hx3, row(p['norm_ffn'][1]), w['moe_w_gate'][1], w['moe_w_up'][1], w['moe_w_down'][1],
                   row(p['norm_final']), tm=tm)

    return (y.reshape(batch, seq, d), r_new[None], k_out, v32.reshape(batch, seq, DIFF_HEADS, DIFF_DV))


def kernel(x_prompt, x_sample, state_ret, cache_k, cache_v, norm_mix, norm_ffn, norm_final, ret_w_in, ret_w_out, kv_norm, kv_w_k, kv_w_v, diff_w_q, diff_lam_q1, diff_lam_k1, diff_lam_q2, diff_lam_k2, diff_subln, diff_w_o, moe_w_group, moe_b_group, moe_w_route, moe_b_route, moe_w_gate, moe_w_up, moe_w_down):
    p = {
        'norm_mix': norm_mix, 'norm_ffn': norm_ffn, 'norm_final': norm_final,
        'ret_w_in': ret_w_in, 'ret_w_out': ret_w_out,
        'kv_norm': kv_norm, 'kv_w_k': kv_w_k, 'kv_w_v': kv_w_v,
        'diff_w_q': diff_w_q, 'diff_lam_q1': diff_lam_q1, 'diff_lam_k1': diff_lam_k1,
        'diff_lam_q2': diff_lam_q2, 'diff_lam_k2': diff_lam_k2, 'diff_subln': diff_subln,
        'diff_w_o': diff_w_o,
        'moe_w_group': moe_w_group, 'moe_b_group': moe_b_group, 'moe_w_route': moe_w_route,
        'moe_b_route': moe_b_route, 'moe_w_gate': moe_w_gate, 'moe_w_up': moe_w_up,
        'moe_w_down': moe_w_down,
    }
    w = _prep_params(p)
    tp = x_prompt.shape[1]
    ts = x_sample.shape[1]
    past = cache_k.shape[1]
    y_p, r_p, k_p, v_p = _trunk(x_prompt, np.arange(tp), None, None, None, p, w,
                                tm=512, ret_chunk=256, ret_tb=512, attn_tq=512)
    y_s, r_s, k_s, v_s = _trunk(x_sample, past + np.arange(ts), state_ret[0], cache_k, cache_v, p, w,
                                tm=x_sample.shape[0] * ts, ret_chunk=ts, ret_tb=ts, attn_tk=1024)
    return (y_p, y_s, r_p, k_p, v_p, r_s, k_s, v_s)
```

```python
import functools
import math

import numpy as np
import jax
import jax.numpy as jnp
from jax import lax
from jax.experimental import pallas as pl
from jax.experimental.pallas import tpu as pltpu

F32 = jnp.float32
BF16 = jnp.bfloat16

D_MODEL = 1024
CHUNK = 64
RET_HEADS = 4
RET_DK = 256
RET_DV = 512
RET_THETA = 10000.0
DIFF_HEADS = 8
DIFF_DH = 64
DIFF_DV = 128
ROPE_THETA = 500000.0
ROPE_DIM = 16
N_GROUPS = 4
EXPERTS_PER_GROUP = 4
N_EXPERTS = 16
D_EXPERT = 256
EPS = 1e-6

LANES = 128
VMEM_LIMIT = 56 * 1024 * 1024
ROUTE_LANES = 128
GROUP_LANE0 = N_EXPERTS
GROUP_ID_LANE = GROUP_LANE0 + N_GROUPS
HX_WIDTH = D_MODEL + ROUTE_LANES
NEG_BIG = -1e30
QK_SCALE_LOG2 = DIFF_DH ** -0.5 * math.log2(math.e)
MAIN_BLOCKS = 4


def _cparams(sem):
    return pltpu.CompilerParams(dimension_semantics=sem, vmem_limit_bytes=VMEM_LIMIT)


def _rms(x, g):
    ms = jnp.mean(x * x, axis=-1, keepdims=True)
    return x * lax.rsqrt(ms + EPS) * g


def _silu(x):
    return x * (1.0 / (1.0 + jnp.exp(-x)))


def _ret_inproj_kernel(x_ref, g_ref, w_ref, cos_ref, sin_ref, o_ref, xn_ref, *, tn):
    j = pl.program_id(1)
    n_q = (RET_HEADS * RET_DK) // tn
    n_qk = 2 * n_q
    n_qkv = n_qk + (RET_HEADS * RET_DV) // tn

    @pl.when(j == 0)
    def _():
        xn_ref[...] = _rms(x_ref[...], g_ref[...]).astype(BF16)

    def proj():
        return jnp.dot(xn_ref[...], w_ref[...], preferred_element_type=F32)

    @pl.when(j < n_qk)
    def _():
        y = proj()
        cos = cos_ref[...]
        sin = sin_ref[...]
        scale = jnp.where(j >= n_q, RET_DK ** -0.5, 1.0).astype(F32)
        half = RET_DK // 2
        for h in range(tn // RET_DK):
            x1 = y[:, h * RET_DK:h * RET_DK + half]
            x2 = y[:, h * RET_DK + half:(h + 1) * RET_DK]
            o_ref[:, h * RET_DK:h * RET_DK + half] = ((x1 * cos - x2 * sin) * scale).astype(BF16)
            o_ref[:, h * RET_DK + half:(h + 1) * RET_DK] = ((x2 * cos + x1 * sin) * scale).astype(BF16)

    @pl.when(jnp.logical_and(j >= n_qk, j < n_qkv))
    def _():
        o_ref[...] = proj().astype(BF16)

    @pl.when(j >= n_qkv)
    def _():
        o_ref[...] = _silu(proj()).astype(BF16)


def ret_inproj(x, g, w, cos, sin, *, tm, tn=1024):
    n, d = x.shape
    m = w.shape[1]
    n_pos_tiles = cos.shape[0] // tm
    return pl.pallas_call(
        functools.partial(_ret_inproj_kernel, tn=tn),
        grid=(n // tm, m // tn),
        in_specs=[
            pl.BlockSpec((tm, d), lambda i, j: (i, 0)),
            pl.BlockSpec((1, d), lambda i, j: (0, 0)),
            pl.BlockSpec((d, tn), lambda i, j: (0, j)),
            pl.BlockSpec((tm, RET_DK // 2), lambda i, j: (i % n_pos_tiles, 0)),
            pl.BlockSpec((tm, RET_DK // 2), lambda i, j: (i % n_pos_tiles, 0)),
        ],
        out_specs=pl.BlockSpec((tm, tn), lambda i, j: (i, j)),
        out_shape=jax.ShapeDtypeStruct((n, m), BF16),
        scratch_shapes=[pltpu.VMEM((tm, d), BF16)],
        compiler_params=_cparams(("parallel", "arbitrary")),
        name="ret_inproj",
    )(x, g, w, cos, sin)


def _retention_kernel(*refs, L, n_chunks, has_r0):
    if has_r0:
        (q_ref, k_ref, v_ref, g_ref, dm_ref, xi_ref, zt_ref, r0_ref, o_ref, rout_ref, r_ref) = refs
    else:
        (q_ref, k_ref, v_ref, g_ref, dm_ref, xi_ref, zt_ref, o_ref, rout_ref, r_ref) = refs
    t = pl.program_id(1)

    @pl.when(t == 0)
    def _():
        if has_r0:
            r_ref[...] = r0_ref[0]
        else:
            r_ref[...] = jnp.zeros_like(r_ref)

    def chunk(c, carry):
        r0 = pl.multiple_of(c * L, L)
        for h in range(RET_HEADS):
            lg = math.log1p(-2.0 ** (-5.0 - h))
            q = q_ref[pl.ds(r0, L), h * RET_DK:(h + 1) * RET_DK]
            k = k_ref[pl.ds(r0, L), h * RET_DK:(h + 1) * RET_DK]
            v = v_ref[pl.ds(r0, L), h * RET_DV:(h + 1) * RET_DV]
            gate = g_ref[pl.ds(r0, L), h * RET_DV:(h + 1) * RET_DV]
            rh = r_ref[h]
            s = lax.dot_general(q, k, (((1,), (1,)), ((), ())), preferred_element_type=F32)
            s = s * dm_ref[h]
            intra = jnp.dot(s.astype(BF16), v, preferred_element_type=F32)
            inter = jnp.dot(q, rh.astype(BF16), preferred_element_type=F32) * xi_ref[h]
            o = intra + inter
            kz = (k.astype(F32) * zt_ref[h]).astype(BF16)
            upd = lax.dot_general(kz, v, (((0,), (0,)), ((), ())), preferred_element_type=F32)
            r_ref[h] = math.exp(lg * L) * rh + upd
            on = o * lax.rsqrt(jnp.mean(o * o, axis=-1, keepdims=True) + EPS)
            o_ref[pl.ds(r0, L), h * RET_DV:(h + 1) * RET_DV] = (on * gate.astype(F32)).astype(BF16)
        return carry

    lax.fori_loop(0, n_chunks, chunk, 0)

    @pl.when(t == pl.num_programs(1) - 1)
    def _():
        rout_ref[0] = r_ref[...]


def _retention_tables(L):
    lg = np.log1p(-np.exp2(-5.0 - np.arange(RET_HEADS, dtype=np.float64)))
    i = np.arange(L, dtype=np.float64)
    diff = i[:, None] - i[None, :]
    dmask = np.where(diff >= 0, np.exp(lg[:, None, None] * np.maximum(diff, 0.0)), 0.0)
    xi = np.exp(lg[:, None] * (i[None, :] + 1.0))
    zeta = np.exp(lg[:, None] * (L - 1.0 - i[None, :]))
    xi = np.broadcast_to(xi[:, :, None], (RET_HEADS, L, RET_DV))
    zeta = np.broadcast_to(zeta[:, :, None], (RET_HEADS, L, RET_DK))
    return (jnp.asarray(dmask, F32), jnp.asarray(xi, F32), jnp.asarray(zeta, F32))


def retention(proj, r0, *, batch, seq, L, tb):
    n = batch * seq
    nt = seq // tb
    dmask, xi, zeta = _retention_tables(L)
    dqk = RET_HEADS * RET_DK
    dv = RET_HEADS * RET_DV
    has_r0 = r0 is not None
    const3 = lambda b, t: (0, 0, 0)
    in_specs = [
        pl.BlockSpec((tb, dqk), lambda b, t: (b * nt + t, 0)),
        pl.BlockSpec((tb, dqk), lambda b, t: (b * nt + t, 1)),
        pl.BlockSpec((tb, dv), lambda b, t: (b * nt + t, 1)),
        pl.BlockSpec((tb, dv), lambda b, t: (b * nt + t, 2)),
        pl.BlockSpec((RET_HEADS, L, L), const3),
        pl.BlockSpec((RET_HEADS, L, RET_DV), const3),
        pl.BlockSpec((RET_HEADS, L, RET_DK), const3),
    ]
    args = [proj, proj, proj, proj, dmask, xi, zeta]
    state_spec = pl.BlockSpec((1, RET_HEADS, RET_DK, RET_DV), lambda b, t: (b, 0, 0, 0))
    if has_r0:
        in_specs.append(state_spec)
        args.append(r0)
    return pl.pallas_call(
        functools.partial(_retention_kernel, L=L, n_chunks=tb // L, has_r0=has_r0),
        grid=(batch, nt),
        in_specs=in_specs,
        out_specs=[pl.BlockSpec((tb, dv), lambda b, t: (b * nt + t, 0)), state_spec],
        out_shape=[jax.ShapeDtypeStruct((n, dv), BF16),
                   jax.ShapeDtypeStruct((batch, RET_HEADS, RET_DK, RET_DV), F32)],
        scratch_shapes=[pltpu.VMEM((RET_HEADS, RET_DK, RET_DV), F32)],
        compiler_params=_cparams(("parallel", "arbitrary")),
        name="retention",
    )(*args)


def _route(u, wr_hi, wr_lo, br):
    u_hi = u.astype(BF16)
    u_lo = (u - u_hi.astype(F32)).astype(BF16)
    logits = (jnp.dot(u_hi, wr_hi, preferred_element_type=F32)
              + jnp.dot(u_lo, wr_hi, preferred_element_type=F32)
              + jnp.dot(u_hi, wr_lo, preferred_element_type=F32)) + br
    lane_i = lax.broadcasted_iota(jnp.int32, logits.shape, 1)
    lane = lane_i.astype(F32)
    lane_grp = (lane_i // EXPERTS_PER_GROUP).astype(F32)
    big = jnp.float32(1 << 20)
    is_grp = jnp.logical_and(lane_i >= GROUP_LANE0, lane_i < GROUP_LANE0 + N_GROUPS)
    gl = jnp.where(is_grp, logits, -jnp.inf)
    gmax = jnp.max(gl, axis=-1, keepdims=True)
    gsum = jnp.sum(jnp.exp(gl - gmax), axis=-1, keepdims=True)
    p_top = 1.0 / gsum
    g_top = jnp.min(jnp.where(gl == gmax, lane, big), axis=-1, keepdims=True) - GROUP_LANE0
    sel = jnp.logical_and(lane_i < N_EXPERTS, lane_grp == g_top)
    el = jnp.where(sel, logits, -jnp.inf)
    emax = jnp.max(el, axis=-1, keepdims=True)
    ee = jnp.exp(el - emax)
    ep = ee / jnp.sum(ee, axis=-1, keepdims=True)
    epm = jnp.where(sel, ep, -1.0)
    m1 = jnp.max(epm, axis=-1, keepdims=True)
    i1 = jnp.min(jnp.where(epm == m1, lane, big), axis=-1, keepdims=True)
    epm2 = jnp.where(lane == i1, -1.0, epm)
    m2 = jnp.max(epm2, axis=-1, keepdims=True)
    i2 = jnp.min(jnp.where(epm2 == m2, lane, big), axis=-1, keepdims=True)
    denom = m1 + m2
    w1 = m1 / denom * p_top
    w2 = m2 / denom * p_top
    gate = jnp.where(lane == i1, w1, jnp.where(lane == i2, w2, 0.0))
    return jnp.where(lane_i == GROUP_ID_LANE, g_top, gate)


def _outproj_kernel(a_ref, w_ref, res_ref, g_ref, wrh_ref, wrl_ref, br_ref, hx_ref, *, rc):
    d = w_ref.shape[1]
    for c in range(a_ref.shape[0] // rc):
        rows = slice(c * rc, (c + 1) * rc)
        h = res_ref[rows, :] + jnp.dot(a_ref[rows, :], w_ref[...], preferred_element_type=F32)
        hx_ref[rows, :d] = h
        hx_ref[rows, d:] = _route(_rms(h, g_ref[...]), wrh_ref[...], wrl_ref[...], br_ref[...])


def outproj_route(a, w, res, g, wr, br, *, tm):
    n, kdim = a.shape
    d = w.shape[1]
    wr_hi, wr_lo = wr
    return pl.pallas_call(
        functools.partial(_outproj_kernel, rc=min(tm, 256)),
        grid=(n // tm,),
        in_specs=[
            pl.BlockSpec((tm, kdim), lambda i: (i, 0)),
            pl.BlockSpec((kdim, d), lambda i: (0, 0)),
            pl.BlockSpec((tm, d), lambda i: (i, 0)),
            pl.BlockSpec((1, d), lambda i: (0, 0)),
            pl.BlockSpec((d, ROUTE_LANES), lambda i: (0, 0)),
            pl.BlockSpec((d, ROUTE_LANES), lambda i: (0, 0)),
            pl.BlockSpec((1, ROUTE_LANES), lambda i: (0, 0)),
        ],
        out_specs=pl.BlockSpec((tm, HX_WIDTH), lambda i: (i, 0)),
        out_shape=jax.ShapeDtypeStruct((n, HX_WIDTH), F32),
        compiler_params=_cparams(("parallel",)),
        name="outproj_route",
    )(a, w, res, g, wr_hi, wr_lo, br)


def _group_experts(u, gate, grp, wg_ref, wu_ref, wd_ref):
    lane = lax.broadcasted_iota(jnp.int32, gate.shape, 1)
    contrib = None
    for e in range(EXPERTS_PER_GROUP):
        ge = jnp.sum(jnp.where(lane == grp * EXPERTS_PER_GROUP + e, gate, 0.0), axis=-1, keepdims=True)
        hg = jnp.dot(u, wg_ref[e], preferred_element_type=F32)
        hu = jnp.dot(u, wu_ref[e], preferred_element_type=F32)
        hid = (_silu(hg) * hu * ge).astype(BF16)
        d = jnp.dot(hid, wd_ref[e], preferred_element_type=F32)
        contrib = d if contrib is None else contrib + d
    return contrib


def _moe_kernel(*refs, sorted_rows, final_norm):
    if sorted_rows:
        tile_ref, group_ref, lo_ref, hi_ref, first_ref, last_ref = refs[:6]
        refs = refs[6:]
    if final_norm:
        hx_ref, gffn_ref, wg_ref, wu_ref, wd_ref, gfin_ref, out_ref, u_ref, acc_ref = refs
    else:
        hx_ref, gffn_ref, wg_ref, wu_ref, wd_ref, out_ref, u_ref, acc_ref = refs
    d = out_ref.shape[1]
    if sorted_rows:
        s = pl.program_id(0)
        grp = group_ref[s]
        is_first = first_ref[s] == 1
        is_last = last_ref[s] == 1
        lo = lo_ref[s]
        hi = hi_ref[s]
    else:
        grp = pl.program_id(1)
        is_first = grp == 0
        is_last = grp == pl.num_programs(1) - 1

    @pl.when(is_first)
    def _():
        u_ref[...] = _rms(hx_ref[:, :d], gffn_ref[...]).astype(BF16)
        acc_ref[...] = jnp.zeros_like(acc_ref)

    def accumulate():
        acc_ref[...] += _group_experts(u_ref[...], hx_ref[:, d:], grp, wg_ref, wu_ref, wd_ref)

    if sorted_rows:
        pl.when(hi > lo)(accumulate)
    else:
        accumulate()

    @pl.when(is_last)
    def _():
        h = hx_ref[:, :d] + acc_ref[...]
        out_ref[...] = _rms(h, gfin_ref[...]) if final_norm else h


def moe(hx, g_ffn, wg, wu, wd, g_final=None, items=None, *, tm):
    n = hx.shape[0]
    d = D_MODEL
    sorted_rows = items is not None
    final_norm = g_final is not None
    if sorted_rows:
        tile_of = lambda s, tile, group, *_: (tile[s], 0)
        group_of = lambda s, tile, group, *_: (group[s], 0, 0)
        const = lambda s, *_: (0, 0)
        grid = (items[0].shape[0],)
    else:
        tile_of = lambda i, g: (i, 0)
        group_of = lambda i, g: (g, 0, 0)
        const = lambda i, g: (0, 0)
        grid = (n // tm, N_GROUPS)
    in_specs = [
        pl.BlockSpec((tm, HX_WIDTH), tile_of),
        pl.BlockSpec((1, d), const),
        pl.BlockSpec((EXPERTS_PER_GROUP, d, D_EXPERT), group_of),
        pl.BlockSpec((EXPERTS_PER_GROUP, d, D_EXPERT), group_of),
        pl.BlockSpec((EXPERTS_PER_GROUP, D_EXPERT, d), group_of),
    ]
    args = [hx, g_ffn, wg, wu, wd]
    if final_norm:
        in_specs.append(pl.BlockSpec((1, d), const))
        args.append(g_final)
    return pl.pallas_call(
        functools.partial(_moe_kernel, sorted_rows=sorted_rows, final_norm=final_norm),
        grid_spec=pltpu.PrefetchScalarGridSpec(
            num_scalar_prefetch=6 if sorted_rows else 0,
            grid=grid,
            in_specs=in_specs,
            out_specs=pl.BlockSpec((tm, d), tile_of),
            scratch_shapes=[pltpu.VMEM((tm, d), BF16), pltpu.VMEM((tm, d), F32)],
        ),
        out_shape=jax.ShapeDtypeStruct((n, d), F32),
        compiler_params=_cparams(("arbitrary",) if sorted_rows else ("parallel", "arbitrary")),
        name="moe",
    )(*(tuple(items) if sorted_rows else ()), *args)


def _permute_kernel(idx_ref, src_ref, dst_ref, sem, *, ch, scatter):
    base = pl.program_id(0) * ch

    def row_copy(r):
        j = idx_ref[base + r]
        if scatter:
            return pltpu.make_async_copy(src_ref.at[pl.ds(r, 1), :], dst_ref.at[pl.ds(j, 1), :], sem)
        return pltpu.make_async_copy(src_ref.at[pl.ds(j, 1), :], dst_ref.at[pl.ds(r, 1), :], sem)

    def issue(r, carry):
        row_copy(r).start()
        return carry

    lax.fori_loop(0, ch, issue, 0, unroll=8)
    rows = pl.ds(base, ch)
    if scatter:
        pltpu.make_async_copy(src_ref, dst_ref.at[rows, :], sem).wait()
    else:
        pltpu.make_async_copy(src_ref.at[rows, :], dst_ref, sem).wait()


def permute_rows(src, idx, *, scatter, ch=2048):
    n, width = src.shape
    ch = min(ch, n)
    tile = pl.BlockSpec((ch, width), lambda i, idx: (i, 0))
    hbm = pl.BlockSpec(memory_space=pl.ANY)
    return pl.pallas_call(
        functools.partial(_permute_kernel, ch=ch, scatter=scatter),
        grid_spec=pltpu.PrefetchScalarGridSpec(
            num_scalar_prefetch=1,
            grid=(n // ch,),
            in_specs=[tile if scatter else hbm],
            out_specs=hbm if scatter else tile,
            scratch_shapes=[pltpu.SemaphoreType.DMA(())],
        ),
        out_shape=jax.ShapeDtypeStruct((n, width), src.dtype),
        compiler_params=_cparams(("arbitrary",)),
        name="permute_rows",
    )(idx, src)


def _rope_cols(y, cosf, sina, sinb):
    blocks = []
    for c in range(y.shape[1] // LANES):
        xb = y[:, c * LANES:(c + 1) * LANES]
        blocks.append(xb * cosf + pltpu.roll(xb, LANES - ROPE_DIM // 2, 1) * sina
                      + pltpu.roll(xb, ROPE_DIM // 2, 1) * sinb)
    return blocks


def _rope_rows(kt, cost, sint):
    half = ROPE_DIM // 2
    parts = []
    for g in range(kt.shape[0] // DIFF_DH):
        b0 = g * DIFF_DH
        x1 = kt[b0:b0 + half]
        x2 = kt[b0 + half:b0 + ROPE_DIM]
        parts += [x1 * cost - x2 * sint, x2 * cost + x1 * sint, kt[b0 + ROPE_DIM:b0 + DIFF_DH]]
    return jnp.concatenate(parts, axis=0)


def _qkv_kernel(*refs, tm, k_feature_major):
    if k_feature_major:
        (h_ref, gn_ref, wk_ref, wv_ref, wq_ref, cos_ref, sa_ref, sb_ref, cost_ref, sint_ref,
         k32_ref, v32_ref, kb_ref, vb_ref, qb_ref) = refs
    else:
        (h_ref, gn_ref, wk_ref, wv_ref, wq_ref, cos_ref, sa_ref, sb_ref,
         k32_ref, v32_ref, kb_ref, vb_ref, qb_ref) = refs
    cosf = cos_ref[...]
    sina = sa_ref[...]
    sinb = sb_ref[...]
    h = h_ref[...]
    inv = lax.rsqrt(jnp.mean(h * h, axis=-1, keepdims=True) + EPS)
    ukv = (h * inv * gn_ref[0:1, :]).astype(BF16)
    umix = (h * inv * gn_ref[1:2, :]).astype(BF16)
    if k_feature_major:
        kt = lax.dot_general(wk_ref[...], ukv, (((1,), (1,)), ((), ())), preferred_element_type=F32)
        kt = _rope_rows(kt, cost_ref[...], sint_ref[...])
        k32_ref[0] = kt
        kb_ref[0, 0] = kt.astype(BF16)
    else:
        k = jnp.dot(ukv, wk_ref[...], preferred_element_type=F32)
        for c, blk in enumerate(_rope_cols(k, cosf, sina, sinb)):
            k32_ref[:, c * LANES:(c + 1) * LANES] = blk
            kb_ref[:, c * LANES:(c + 1) * LANES] = blk.astype(BF16)
    v = jnp.dot(ukv, wv_ref[...], preferred_element_type=F32)
    vb_ref[...] = v.astype(BF16)
    for h in range(DIFF_HEADS):
        v32_ref[pl.ds(h, tm, stride=DIFF_HEADS), :] = v[:, h * DIFF_DV:(h + 1) * DIFF_DV]
    q = jnp.dot(umix, wq_ref[...], preferred_element_type=F32)
    for c, blk in enumerate(_rope_cols(q, cosf, sina, sinb)):
        qb_ref[:, c * LANES:(c + 1) * LANES] = (blk * QK_SCALE_LOG2).astype(BF16)


def qkv_proj(h, gains, wk, wv, wq, cosf, sina, sinb, cost=None, sint=None, *, tm, batch, seq):
    n, d = h.shape
    k_feature_major = cost is not None
    nt = max(seq // tm, 1)
    n_pos_tiles = cosf.shape[0] // tm
    tok = lambda i: (i, 0)
    wspec = pl.BlockSpec((d, d), lambda i: (0, 0))
    tspec = pl.BlockSpec((tm, LANES), lambda i: (i % n_pos_tiles, 0))
    in_specs = [pl.BlockSpec((tm, d), tok), pl.BlockSpec((2, d), lambda i: (0, 0)), wspec, wspec, wspec,
                tspec, tspec, tspec]
    args = [h, gains, wk, wv, wq, cosf, sina, sinb]
    rows_f32 = (pl.BlockSpec((tm, d), tok), jax.ShapeDtypeStruct((n, d), F32))
    rows_bf16 = (pl.BlockSpec((tm, d), tok), jax.ShapeDtypeStruct((n, d), BF16))
    v32 = (pl.BlockSpec((tm * DIFF_HEADS, DIFF_DV), tok), jax.ShapeDtypeStruct((n * DIFF_HEADS, DIFF_DV), F32))
    if k_feature_major:
        half = ROPE_DIM // 2
        rspec = pl.BlockSpec((half, tm), lambda i: (0, i % nt))
        in_specs += [rspec, rspec]
        args += [cost, sint]
        k32 = (pl.BlockSpec((1, d, tm), lambda i: (i // nt, 0, i % nt)),
               jax.ShapeDtypeStruct((batch, d, seq), F32))
        kb = (pl.BlockSpec((1, 1, d, tm), lambda i: (i // nt, i % nt, 0, 0)),
              jax.ShapeDtypeStruct((batch, nt, d, tm), BF16))
    else:
        k32, kb = rows_f32, rows_bf16
    outs = [k32, v32, kb, rows_bf16, rows_bf16]
    return pl.pallas_call(
        functools.partial(_qkv_kernel, tm=tm, k_feature_major=k_feature_major),
        grid=(n // tm,),
        in_specs=in_specs,
        out_specs=[o[0] for o in outs],
        out_shape=[o[1] for o in outs],
        compiler_params=_cparams(("parallel",)),
        name="qkv_proj",
    )(*args)


def _lambda(lam_ref, lam_init):
    lv = lam_ref[...]
    a = jnp.sum(lv[0:1, :] * lv[1:2, :], axis=-1, keepdims=True)
    b = jnp.sum(lv[2:3, :] * lv[3:4, :], axis=-1, keepdims=True)
    return jnp.exp(a) - jnp.exp(b) + lam_init


def _split_q(q):
    qf = q.astype(F32)
    lane = lax.broadcasted_iota(jnp.int32, qf.shape, 1)
    q0 = jnp.where(lane < DIFF_DH, qf, 0.0).astype(BF16)
    q1 = jnp.where(lane >= DIFF_DH, qf, 0.0).astype(BF16)
    return jnp.concatenate([q0, q1], axis=0)


def _online_update(s, v, m_ref, l_ref, acc_ref):
    m_old = m_ref[...]
    if s.shape[1] < LANES:
        m_new = jnp.maximum(m_old, jnp.max(s, axis=-1, keepdims=True))
        alpha = jnp.exp2(m_old - m_new)
        p = jnp.exp2(s - m_new[:, :s.shape[1]])
        lane = lax.broadcasted_iota(jnp.int32, m_old.shape, 1)
        lsum = jnp.where(lane == 0, jnp.sum(p, axis=-1, keepdims=True), 0.0)
        l_ref[...] = alpha * l_ref[...] + lsum
        acc_ref[...] = alpha * acc_ref[...] + jnp.dot(p.astype(BF16), v, preferred_element_type=F32)
        m_ref[...] = m_new
        return
    ncol = s.shape[1] // LANES
    cols = [s[:, c * LANES:(c + 1) * LANES] for c in range(ncol)]
    mx = cols[0]
    for c in range(1, ncol):
        mx = jnp.maximum(mx, cols[c])
    m_new = jnp.maximum(m_old, jnp.max(mx, axis=-1, keepdims=True))
    alpha = jnp.exp2(m_old - m_new)
    ps = [jnp.exp2(col - m_new) for col in cols]
    lsum = ps[0]
    for c in range(1, ncol):
        lsum = lsum + ps[c]
    p = jnp.concatenate([pc.astype(BF16) for pc in ps], axis=1) if ncol > 1 else ps[0].astype(BF16)
    l_ref[...] = alpha * l_ref[...] + lsum
    acc_ref[...] = alpha * acc_ref[...] + jnp.dot(p, v, preferred_element_type=F32)
    m_ref[...] = m_new


def _diff_finish(l_ref, acc_ref, lam, subln, lam_init, tq):
    o = acc_ref[...] / jnp.sum(l_ref[...], axis=-1, keepdims=True)
    a = o[:tq] - lam * o[tq:]
    return _rms(a, subln) * (1.0 - lam_init)


def _attn_prompt_kernel(q_ref, k_ref, v_ref, lam_ref, sub_ref, o_ref, q2_ref, m_ref, l_ref, acc_ref,
                        *, tq, kblk, qg, lam_init):
    i = pl.program_id(2)
    ng = tq // qg
    for g in range(ng):
        q2_ref[2 * g * qg:2 * (g + 1) * qg, :] = _split_q(q_ref[g * qg:(g + 1) * qg, :])
    m_ref[...] = jnp.full_like(m_ref, NEG_BIG)
    l_ref[...] = jnp.zeros_like(l_ref)
    acc_ref[...] = jnp.zeros_like(acc_ref)

    def step(j, nkeys, rows, mask_group=None):
        r0 = pl.multiple_of(j * kblk, kblk)
        v = v_ref[pl.ds(r0, nkeys), :]
        q2 = q2_ref[rows, :]
        parts = []
        for t in range(-(-nkeys // kblk)):
            kt = k_ref[0, j + t]
            if (t + 1) * kblk > nkeys:
                kt = kt[:, :nkeys - t * kblk]
            parts.append(jnp.dot(q2, kt, preferred_element_type=F32))
        s = parts[0] if len(parts) == 1 else jnp.concatenate(parts, axis=1)
        if mask_group is not None:
            qrow = mask_group * qg + lax.broadcasted_iota(jnp.int32, s.shape, 0) % qg
            col = lax.broadcasted_iota(jnp.int32, s.shape, 1)
            s = jnp.where(col < (qrow // CHUNK + 1) * CHUNK, s, NEG_BIG)
        _online_update(s, v, m_ref.at[rows, :], l_ref.at[rows, :], acc_ref.at[rows, :])

    all_rows = slice(0, 2 * tq)
    nb = tq // kblk
    n_full = i * nb

    def body(jj, carry):
        step(MAIN_BLOCKS * jj, MAIN_BLOCKS * kblk, all_rows)
        return carry

    lax.fori_loop(0, n_full // MAIN_BLOCKS, body, 0)
    done = (n_full // MAIN_BLOCKS) * MAIN_BLOCKS
    size = MAIN_BLOCKS // 2
    while size >= 1:
        take = (n_full - done) >= size
        pl.when(take)(functools.partial(step, done, size * kblk, all_rows))
        done = done + jnp.where(take, size, 0)
        size //= 2

    lam = _lambda(lam_ref, lam_init)
    for g in range(ng):
        rows = slice(2 * g * qg, 2 * (g + 1) * qg)
        step(n_full, (g + 1) * qg, rows, mask_group=g)
        o_ref[g * qg:(g + 1) * qg, :] = _diff_finish(
            l_ref.at[rows, :], acc_ref.at[rows, :], lam, sub_ref[...], lam_init, qg).astype(BF16)


def attn_prompt(qb, kb, vb, lamv, subln, *, batch, seq, tq, kblk, lam_init):
    n = batch * seq
    nq = seq // tq
    qg = tq // 2
    return pl.pallas_call(
        functools.partial(_attn_prompt_kernel, tq=tq, kblk=kblk, qg=qg, lam_init=lam_init),
        grid=(batch, DIFF_HEADS, nq),
        in_specs=[
            pl.BlockSpec((tq, LANES), lambda b, h, i: (b * nq + i, h)),
            pl.BlockSpec((1, seq // kblk, LANES, kblk), lambda b, h, i: (b, 0, h, 0)),
            pl.BlockSpec((seq, LANES), lambda b, h, i: (b, h)),
            pl.BlockSpec((8, LANES), lambda b, h, i: (0, 0)),
            pl.BlockSpec((1, LANES), lambda b, h, i: (0, 0)),
        ],
        out_specs=pl.BlockSpec((tq, LANES), lambda b, h, i: (b * nq + i, h)),
        out_shape=jax.ShapeDtypeStruct((n, DIFF_HEADS * DIFF_DV), BF16),
        scratch_shapes=[pltpu.VMEM((2 * tq, LANES), BF16), pltpu.VMEM((2 * tq, LANES), F32),
                        pltpu.VMEM((2 * tq, LANES), F32), pltpu.VMEM((2 * tq, LANES), F32)],
        compiler_params=_cparams(("parallel", "parallel", "arbitrary")),
        name="attn_prompt",
    )(qb, kb, vb, lamv, subln)


def _attn_sample_kernel(q_ref, ck_ref, cv_ref, kn_ref, vn_ref, lam_ref, sub_ref, o_ref,
                        q2_ref, m_ref, l_ref, acc_ref, *, tq, past, lam_init):
    j = pl.program_id(1)
    last = pl.num_programs(1) - 1

    @pl.when(j == 0)
    def _():
        for h in range(DIFF_HEADS):
            q2_ref[h] = _split_q(q_ref[:, h * LANES:(h + 1) * LANES])
        m_ref[...] = jnp.full_like(m_ref, NEG_BIG)
        l_ref[...] = jnp.zeros_like(l_ref)
        acc_ref[...] = jnp.zeros_like(acc_ref)

    @pl.when(j < last)
    def _():
        tk = ck_ref.shape[-1]
        scores = [jnp.dot(q2_ref[h], ck_ref[0, h].astype(BF16), preferred_element_type=F32)
                  for h in range(DIFF_HEADS)]
        for h in range(DIFF_HEADS):
            v = cv_ref[pl.ds(h, tk, stride=DIFF_HEADS), :].astype(BF16)
            _online_update(scores[h], v, m_ref.at[h], l_ref.at[h], acc_ref.at[h])

    @pl.when(j == last)
    def _():
        lam = _lambda(lam_ref, lam_init)
        for h in range(DIFF_HEADS):
            k = kn_ref[:, h * LANES:(h + 1) * LANES]
            v = vn_ref[:, h * LANES:(h + 1) * LANES]
            s = lax.dot_general(q2_ref[h], k, (((1,), (1,)), ((), ())), preferred_element_type=F32)
            qpos = past + lax.broadcasted_iota(jnp.int32, s.shape, 0) % tq
            kpos = past + lax.broadcasted_iota(jnp.int32, s.shape, 1)
            s = jnp.where(kpos < (qpos // CHUNK + 1) * CHUNK, s, NEG_BIG)
            _online_update(s, v, m_ref.at[h], l_ref.at[h], acc_ref.at[h])
            o_ref[:, h * LANES:(h + 1) * LANES] = _diff_finish(
                l_ref.at[h], acc_ref.at[h], lam, sub_ref[...], lam_init, tq).astype(BF16)


def attn_sample(qb, cache_k, cache_v, kb, vb, lamv, subln, *, batch, seq, tk, lam_init):
    past = cache_k.shape[-1]
    d = DIFF_HEADS * DIFF_DV
    nk = past // tk
    ck_spec = pl.BlockSpec((1, DIFF_HEADS, LANES, tk), lambda b, j: (b, 0, 0, jnp.minimum(j, nk - 1)))
    cv_spec = pl.BlockSpec((tk * DIFF_HEADS, DIFF_DV), lambda b, j: (b * nk + jnp.minimum(j, nk - 1), 0))
    row_spec = pl.BlockSpec((seq, d), lambda b, j: (b, 0))
    return pl.pallas_call(
        functools.partial(_attn_sample_kernel, tq=seq, past=past, lam_init=lam_init),
        grid=(batch, nk + 1),
        in_specs=[row_spec, ck_spec, cv_spec, row_spec, row_spec,
                  pl.BlockSpec((8, LANES), lambda b, j: (0, 0)),
                  pl.BlockSpec((1, LANES), lambda b, j: (0, 0))],
        out_specs=row_spec,
        out_shape=jax.ShapeDtypeStruct((batch * seq, d), BF16),
        scratch_shapes=[pltpu.VMEM((DIFF_HEADS, 2 * seq, LANES), BF16),
                        pltpu.VMEM((DIFF_HEADS, 2 * seq, LANES), F32),
                        pltpu.VMEM((DIFF_HEADS, 2 * seq, LANES), F32),
                        pltpu.VMEM((DIFF_HEADS, 2 * seq, LANES), F32)],
        compiler_params=_cparams(("parallel", "arbitrary")),
        name="attn_sample",
    )(qb, cache_k, cache_v, kb, vb, lamv, subln)


def _angles(pos, half, theta):
    pos = np.asarray(pos).astype(np.float64)
    inv_freq = np.power(np.float64(theta), -np.arange(half, dtype=np.float64) / half)
    return pos[:, None] * inv_freq[None, :]


def _ret_rope_tables(pos):
    ang = _angles(pos, RET_DK // 2, RET_THETA)
    return np.cos(ang).astype(np.float32), np.sin(ang).astype(np.float32)


def _diff_rope_tables(pos):
    half = ROPE_DIM // 2
    ang = _angles(pos, half, ROPE_THETA)
    cos, sin = np.cos(ang).astype(np.float32), np.sin(ang).astype(np.float32)
    t = cos.shape[0]
    pad = DIFF_DH - ROPE_DIM
    cos64 = np.concatenate([cos, cos, np.ones((t, pad), np.float32)], axis=1)
    sina64 = np.concatenate([-sin, np.zeros((t, DIFF_DH - half), np.float32)], axis=1)
    sinb64 = np.concatenate([np.zeros((t, half), np.float32), sin, np.zeros((t, pad), np.float32)], axis=1)
    rep = LANES // DIFF_DH
    return (np.tile(cos64, (1, rep)), np.tile(sina64, (1, rep)), np.tile(sinb64, (1, rep)),
            np.ascontiguousarray(cos.T), np.ascontiguousarray(sin.T))


def _prep_params(p):
    w = {}
    w['ret_w_in'] = p['ret_w_in'][0].astype(BF16)
    w['ret_w_out'] = p['ret_w_out'][0].astype(BF16)
    w['kv_w_k'] = p['kv_w_k'].astype(BF16)
    w['kv_w_k_t'] = p['kv_w_k'].T.astype(BF16)
    w['kv_w_v'] = p['kv_w_v'].astype(BF16)
    w['diff_w_q'] = p['diff_w_q'][0].astype(BF16)
    w['diff_w_o'] = p['diff_w_o'][0].astype(BF16)
    w['moe_w_gate'] = p['moe_w_gate'].astype(BF16)
    w['moe_w_up'] = p['moe_w_up'].astype(BF16)
    w['moe_w_down'] = p['moe_w_down'].astype(BF16)
    wr, br = [], []
    for layer in range(2):
        route = jnp.transpose(p['moe_w_route'][layer], (1, 0, 2)).reshape(D_MODEL, N_EXPERTS)
        cols = jnp.concatenate([route, p['moe_w_group'][layer]], axis=1)
        full = jnp.pad(cols, ((0, 0), (0, ROUTE_LANES - cols.shape[1])))
        hi = full.astype(BF16)
        wr.append((hi, (full - hi.astype(F32)).astype(BF16)))
        bias = jnp.concatenate([p['moe_b_route'][layer].reshape(-1), p['moe_b_group'][layer]])
        br.append(jnp.pad(bias, (0, ROUTE_LANES - bias.shape[0]))[None, :])
    w['route_w'] = wr
    w['route_b'] = br
    lamv = jnp.concatenate([p['diff_lam_q1'][0][None], p['diff_lam_k1'][0][None],
                            p['diff_lam_q2'][0][None], p['diff_lam_k2'][0][None]], axis=0)
    w['lamv'] = jnp.pad(lamv, ((0, 4), (0, LANES - DIFF_DH)))
    return w


def _group_sort_tables(hx, tm):
    n = hx.shape[0]
    n_tiles = n // tm
    i32 = jnp.int32
    g = hx[:, D_MODEL + GROUP_ID_LANE].astype(i32)
    onehot = (g[:, None] == jnp.arange(N_GROUPS, dtype=i32)[None, :]).astype(i32)
    csum = jnp.cumsum(onehot, axis=0)
    ends = jnp.cumsum(csum[-1])
    starts = ends - csum[-1]
    pos = jnp.sum(onehot * (starts[None, :] + csum - 1), axis=1)
    inner_ends = ends[:N_GROUPS - 1]
    lo = jnp.sort(jnp.concatenate([jnp.arange(n_tiles, dtype=i32) * tm, inner_ends]))
    hi = jnp.concatenate([lo[1:], jnp.full((1,), n, i32)])
    tile = jnp.minimum(lo // tm, n_tiles - 1)
    group = jnp.sum((inner_ends[None, :] <= lo[:, None]).astype(i32), axis=1)
    change = (tile[1:] != tile[:-1]).astype(i32)
    one = jnp.ones((1,), i32)
    return pos, (tile, group, lo, hi, jnp.concatenate([one, change]), jnp.concatenate([change, one]))


def _moe_layer(hx, g_ffn, wg, wu, wd, g_final, *, tm):
    if hx.shape[0] < 8 * tm:
        return moe(hx, g_ffn, wg, wu, wd, g_final, tm=tm)
    pos, items = _group_sort_tables(hx, tm)
    xs = permute_rows(hx, pos, scatter=True)
    ys = moe(xs, g_ffn, wg, wu, wd, g_final, items, tm=tm)
    return permute_rows(ys, pos, scatter=False)


def _trunk(x, pos, r0, past_k, past_v, p, w, *, tm, ret_chunk, ret_tb, attn_tq=None, attn_tk=None):
    batch, seq, d = x.shape
    n = batch * seq
    xf = x.reshape(n, d)
    row = lambda v: v.reshape(1, -1)

    cos, sin = _ret_rope_tables(pos)
    if seq < tm:
        cos, sin = np.tile(cos, (tm // seq, 1)), np.tile(sin, (tm // seq, 1))
    proj = ret_inproj(xf, row(p['norm_mix'][0]), w['ret_w_in'], cos, sin, tm=min(2 * tm, seq, n) if seq >= tm else tm)
    o_gated, r_new = retention(proj, r0, batch=batch, seq=seq, L=ret_chunk, tb=ret_tb)
    hx1 = outproj_route(o_gated, w['ret_w_out'], xf, row(p['norm_ffn'][0]),
                        w['route_w'][0], w['route_b'][0], tm=tm)
    h2 = _moe_layer(hx1, row(p['norm_ffn'][0]), w['moe_w_gate'][0], w['moe_w_up'][0], w['moe_w_down'][0],
                    None, tm=tm)
    gn = jnp.stack([p['kv_norm'], p['norm_mix'][1]])

    cosf, sina, sinb, cost, sint = _diff_rope_tables(pos)
    if seq < tm:
        rep = (tm // seq, 1)
        cosf, sina, sinb = np.tile(cosf, rep), np.tile(sina, rep), np.tile(sinb, rep)
    lam_init = 0.8 - 0.6 * math.exp(-0.3 * 1)
    subln = row(p['diff_subln'][0])
    if past_k is None:
        assert attn_tq % tm == 0
        k32, v32, kb, vb, qb = qkv_proj(h2, gn, w['kv_w_k_t'], w['kv_w_v'], w['diff_w_q'],
                                        cosf, sina, sinb, cost, sint, tm=tm, batch=batch, seq=seq)
        attn = attn_prompt(qb, kb, vb, w['lamv'], subln, batch=batch, seq=seq, tq=attn_tq, kblk=tm,
                           lam_init=lam_init)
        k_out = jnp.transpose(k32.reshape(batch, DIFF_HEADS, 2, DIFF_DH, seq), (0, 4, 1, 2, 3))
    else:
        past = past_k.shape[1]
        k32, v32, kb, vb, qb = qkv_proj(h2, gn, w['kv_w_k'], w['kv_w_v'], w['diff_w_q'],
                                        cosf, sina, sinb, tm=tm, batch=batch, seq=seq)
        ck = jnp.transpose(past_k, (0, 2, 3, 4, 1)).reshape(batch, DIFF_HEADS, 2 * DIFF_DH, past)
        cv = past_v.reshape(batch * past * DIFF_HEADS, DIFF_DV)
        attn = attn_sample(qb, ck, cv, kb, vb, w['lamv'], subln, batch=batch, seq=seq, tk=attn_tk,
                           lam_init=lam_init)
        k_out = k32.reshape(batch, seq, DIFF_HEADS, 2, DIFF_DH)
    hx3 = outproj_route(attn, w['diff_w_o'], h2, row(p['norm_ffn'][1]),
                        w['route_w'][1], w['route_b'][1], tm=tm)
    y = _moe_layer(hx3, row(p['norm_ffn'][1]), w['moe_w_gate'][1], w['moe_w_up'][1], w['moe_w_down'][1],
                   row(p['norm_final']), tm=tm)

    return (y.reshape(batch, seq, d), r_new[None], k_out, v32.reshape(batch, seq, DIFF_HEADS, DIFF_DV))


def kernel(x_prompt, x_sample, state_ret, cache_k, cache_v, norm_mix, norm_ffn, norm_final, ret_w_in, ret_w_out, kv_norm, kv_w_k, kv_w_v, diff_w_q, diff_lam_q1, diff_lam_k1, diff_lam_q2, diff_lam_k2, diff_subln, diff_w_o, moe_w_group, moe_b_group, moe_w_route, moe_b_route, moe_w_gate, moe_w_up, moe_w_down):
    p = {
        'norm_mix': norm_mix, 'norm_ffn': norm_ffn, 'norm_final': norm_final,
        'ret_w_in': ret_w_in, 'ret_w_out': ret_w_out,
        'kv_norm': kv_norm, 'kv_w_k': kv_w_k, 'kv_w_v': kv_w_v,
        'diff_w_q': diff_w_q, 'diff_lam_q1': diff_lam_q1, 'diff_lam_k1': diff_lam_k1,
        'diff_lam_q2': diff_lam_q2, 'diff_lam_k2': diff_lam_k2, 'diff_subln': diff_subln,
        'diff_w_o': diff_w_o,
        'moe_w_group': moe_w_group, 'moe_b_group': moe_b_group, 'moe_w_route': moe_w_route,
        'moe_b_route': moe_b_route, 'moe_w_gate': moe_w_gate, 'moe_w_up': moe_w_up,
        'moe_w_down': moe_w_down,
    }
    w = _prep_params(p)
    tp = x_prompt.shape[1]
    ts = x_sample.shape[1]
    past = cache_k.shape[1]
    y_p, r_p, k_p, v_p = _trunk(x_prompt, np.arange(tp), None, None, None, p, w,
                                tm=512, ret_chunk=256, ret_tb=512, attn_tq=512)
    y_s, r_s, k_s, v_s = _trunk(x_sample, past + np.arange(ts), state_ret[0], cache_k, cache_v, p, w,
                                tm=x_sample.shape[0] * ts, ret_chunk=ts, ret_tb=ts, attn_tk=1024)
    return (y_p, y_s, r_p, k_p, v_p, r_s, k_s, v_s)
```

```python
import functools
import math

import numpy as np
import jax
import jax.numpy as jnp
from jax import lax
from jax.experimental import pallas as pl
from jax.experimental.pallas import tpu as pltpu

F32 = jnp.float32
BF16 = jnp.bfloat16

D_MODEL = 1024
CHUNK = 64
RET_HEADS = 4
RET_DK = 256
RET_DV = 512
RET_THETA = 10000.0
DIFF_HEADS = 8
DIFF_DH = 64
DIFF_DV = 128
ROPE_THETA = 500000.0
ROPE_DIM = 16
N_GROUPS = 4
EXPERTS_PER_GROUP = 4
N_EXPERTS = 16
D_EXPERT = 256
EPS = 1e-6

LANES = 128
VMEM_LIMIT = 56 * 1024 * 1024
ROUTE_LANES = 128
GROUP_LANE0 = N_EXPERTS
GROUP_ID_LANE = GROUP_LANE0 + N_GROUPS
HX_WIDTH = D_MODEL + ROUTE_LANES
NEG_BIG = -1e30
QK_SCALE_LOG2 = DIFF_DH ** -0.5 * math.log2(math.e)
MAIN_BLOCKS = 4


def _cparams(sem):
    return pltpu.CompilerParams(dimension_semantics=sem, vmem_limit_bytes=VMEM_LIMIT)


def _rms(x, g):
    ms = jnp.mean(x * x, axis=-1, keepdims=True)
    return x * lax.rsqrt(ms + EPS) * g


def _silu(x):
    return x * (1.0 / (1.0 + jnp.exp(-x)))


def _ret_inproj_kernel(x_ref, g_ref, w_ref, cos_ref, sin_ref, o_ref, xn_ref, *, tn):
    j = pl.program_id(1)
    n_q = (RET_HEADS * RET_DK) // tn
    n_qk = 2 * n_q
    n_qkv = n_qk + (RET_HEADS * RET_DV) // tn

    @pl.when(j == 0)
    def _():
        xn_ref[...] = _rms(x_ref[...], g_ref[...]).astype(BF16)

    def proj():
        return jnp.dot(xn_ref[...], w_ref[...], preferred_element_type=F32)

    @pl.when(j < n_qk)
    def _():
        y = proj()
        cos = cos_ref[...]
        sin = sin_ref[...]
        scale = jnp.where(j >= n_q, RET_DK ** -0.5, 1.0).astype(F32)
        half = RET_DK // 2
        for h in range(tn // RET_DK):
            x1 = y[:, h * RET_DK:h * RET_DK + half]
            x2 = y[:, h * RET_DK + half:(h + 1) * RET_DK]
            o_ref[:, h * RET_DK:h * RET_DK + half] = ((x1 * cos - x2 * sin) * scale).astype(BF16)
            o_ref[:, h * RET_DK + half:(h + 1) * RET_DK] = ((x2 * cos + x1 * sin) * scale).astype(BF16)

    @pl.when(jnp.logical_and(j >= n_qk, j < n_qkv))
    def _():
        o_ref[...] = proj().astype(BF16)

    @pl.when(j >= n_qkv)
    def _():
        o_ref[...] = _silu(proj()).astype(BF16)


def ret_inproj(x, g, w, cos, sin, *, tm, tn=1024):
    n, d = x.shape
    m = w.shape[1]
    n_pos_tiles = cos.shape[0] // tm
    return pl.pallas_call(
        functools.partial(_ret_inproj_kernel, tn=tn),
        grid=(n // tm, m // tn),
        in_specs=[
            pl.BlockSpec((tm, d), lambda i, j: (i, 0)),
            pl.BlockSpec((1, d), lambda i, j: (0, 0)),
            pl.BlockSpec((d, tn), lambda i, j: (0, j)),
            pl.BlockSpec((tm, RET_DK // 2), lambda i, j: (i % n_pos_tiles, 0)),
            pl.BlockSpec((tm, RET_DK // 2), lambda i, j: (i % n_pos_tiles, 0)),
        ],
        out_specs=pl.BlockSpec((tm, tn), lambda i, j: (i, j)),
        out_shape=jax.ShapeDtypeStruct((n, m), BF16),
        scratch_shapes=[pltpu.VMEM((tm, d), BF16)],
        compiler_params=_cparams(("parallel", "arbitrary")),
        name="ret_inproj",
    )(x, g, w, cos, sin)


def _retention_kernel(*refs, L, n_chunks, has_r0):
    if has_r0:
        (q_ref, k_ref, v_ref, g_ref, dm_ref, xi_ref, zt_ref, r0_ref, o_ref, rout_ref, r_ref) = refs
    else:
        (q_ref, k_ref, v_ref, g_ref, dm_ref, xi_ref, zt_ref, o_ref, rout_ref, r_ref) = refs
    t = pl.program_id(1)

    @pl.when(t == 0)
    def _():
        if has_r0:
            r_ref[...] = r0_ref[0]
        else:
            r_ref[...] = jnp.zeros_like(r_ref)

    def chunk(c, carry):
        r0 = pl.multiple_of(c * L, L)
        for h in range(RET_HEADS):
            lg = math.log1p(-2.0 ** (-5.0 - h))
            q = q_ref[pl.ds(r0, L), h * RET_DK:(h + 1) * RET_DK]
            k = k_ref[pl.ds(r0, L), h * RET_DK:(h + 1) * RET_DK]
            v = v_ref[pl.ds(r0, L), h * RET_DV:(h + 1) * RET_DV]
            gate = g_ref[pl.ds(r0, L), h * RET_DV:(h + 1) * RET_DV]
            rh = r_ref[h]
            s = lax.dot_general(q, k, (((1,), (1,)), ((), ())), preferred_element_type=F32)
            s = s * dm_ref[h]
            intra = jnp.dot(s.astype(BF16), v, preferred_element_type=F32)
            inter = jnp.dot(q, rh.astype(BF16), preferred_element_type=F32) * xi_ref[h]
            o = intra + inter
            kz = (k.astype(F32) * zt_ref[h]).astype(BF16)
            upd = lax.dot_general(kz, v, (((0,), (0,)), ((), ())), preferred_element_type=F32)
            r_ref[h] = math.exp(lg * L) * rh + upd
            on = o * lax.rsqrt(jnp.mean(o * o, axis=-1, keepdims=True) + EPS)
            o_ref[pl.ds(r0, L), h * RET_DV:(h + 1) * RET_DV] = (on * gate.astype(F32)).astype(BF16)
        return carry

    lax.fori_loop(0, n_chunks, chunk, 0)

    @pl.when(t == pl.num_programs(1) - 1)
    def _():
        rout_ref[0] = r_ref[...]


def _retention_tables(L):
    lg = np.log1p(-np.exp2(-5.0 - np.arange(RET_HEADS, dtype=np.float64)))
    i = np.arange(L, dtype=np.float64)
    diff = i[:, None] - i[None, :]
    dmask = np.where(diff >= 0, np.exp(lg[:, None, None] * np.maximum(diff, 0.0)), 0.0)
    xi = np.exp(lg[:, None] * (i[None, :] + 1.0))
    zeta = np.exp(lg[:, None] * (L - 1.0 - i[None, :]))
    xi = np.broadcast_to(xi[:, :, None], (RET_HEADS, L, RET_DV))
    zeta = np.broadcast_to(zeta[:, :, None], (RET_HEADS, L, RET_DK))
    return (jnp.asarray(dmask, F32), jnp.asarray(xi, F32), jnp.asarray(zeta, F32))


def retention(proj, r0, *, batch, seq, L, tb):
    n = batch * seq
    nt = seq // tb
    dmask, xi, zeta = _retention_tables(L)
    dqk = RET_HEADS * RET_DK
    dv = RET_HEADS * RET_DV
    has_r0 = r0 is not None
    const3 = lambda b, t: (0, 0, 0)
    in_specs = [
        pl.BlockSpec((tb, dqk), lambda b, t: (b * nt + t, 0)),
        pl.BlockSpec((tb, dqk), lambda b, t: (b * nt + t, 1)),
        pl.BlockSpec((tb, dv), lambda b, t: (b * nt + t, 1)),
        pl.BlockSpec((tb, dv), lambda b, t: (b * nt + t, 2)),
        pl.BlockSpec((RET_HEADS, L, L), const3),
        pl.BlockSpec((RET_HEADS, L, RET_DV), const3),
        pl.BlockSpec((RET_HEADS, L, RET_DK), const3),
    ]
    args = [proj, proj, proj, proj, dmask, xi, zeta]
    state_spec = pl.BlockSpec((1, RET_HEADS, RET_DK, RET_DV), lambda b, t: (b, 0, 0, 0))
    if has_r0:
        in_specs.append(state_spec)
        args.append(r0)
    return pl.pallas_call(
        functools.partial(_retention_kernel, L=L, n_chunks=tb // L, has_r0=has_r0),
        grid=(batch, nt),
        in_specs=in_specs,
        out_specs=[pl.BlockSpec((tb, dv), lambda b, t: (b * nt + t, 0)), state_spec],
        out_shape=[jax.ShapeDtypeStruct((n, dv), BF16),
                   jax.ShapeDtypeStruct((batch, RET_HEADS, RET_DK, RET_DV), F32)],
        scratch_shapes=[pltpu.VMEM((RET_HEADS, RET_DK, RET_DV), F32)],
        compiler_params=_cparams(("parallel", "arbitrary")),
        name="retention",
    )(*args)


def _route(u, wr_hi, wr_lo, br):
    u_hi = u.astype(BF16)
    u_lo = (u - u_hi.astype(F32)).astype(BF16)
    logits = (jnp.dot(u_hi, wr_hi, preferred_element_type=F32)
              + jnp.dot(u_lo, wr_hi, preferred_element_type=F32)
              + jnp.dot(u_hi, wr_lo, preferred_element_type=F32)) + br
    lane_i = lax.broadcasted_iota(jnp.int32, logits.shape, 1)
    lane = lane_i.astype(F32)
    lane_grp = (lane_i // EXPERTS_PER_GROUP).astype(F32)
    big = jnp.float32(1 << 20)
    is_grp = jnp.logical_and(lane_i >= GROUP_LANE0, lane_i < GROUP_LANE0 + N_GROUPS)
    gl = jnp.where(is_grp, logits, -jnp.inf)
    gmax = jnp.max(gl, axis=-1, keepdims=True)
    gsum = jnp.sum(jnp.exp(gl - gmax), axis=-1, keepdims=True)
    p_top = 1.0 / gsum
    g_top = jnp.min(jnp.where(gl == gmax, lane, big), axis=-1, keepdims=True) - GROUP_LANE0
    sel = jnp.logical_and(lane_i < N_EXPERTS, lane_grp == g_top)
    el = jnp.where(sel, logits, -jnp.inf)
    emax = jnp.max(el, axis=-1, keepdims=True)
    ee = jnp.exp(el - emax)
    ep = ee / jnp.sum(ee, axis=-1, keepdims=True)
    epm = jnp.where(sel, ep, -1.0)
    m1 = jnp.max(epm, axis=-1, keepdims=True)
    i1 = jnp.min(jnp.where(epm == m1, lane, big), axis=-1, keepdims=True)
    epm2 = jnp.where(lane == i1, -1.0, epm)
    m2 = jnp.max(epm2, axis=-1, keepdims=True)
    i2 = jnp.min(jnp.where(epm2 == m2, lane, big), axis=-1, keepdims=True)
    denom = m1 + m2
    w1 = m1 / denom * p_top
    w2 = m2 / denom * p_top
    gate = jnp.where(lane == i1, w1, jnp.where(lane == i2, w2, 0.0))
    return jnp.where(lane_i == GROUP_ID_LANE, g_top, gate)


def _outproj_kernel(a_ref, w_ref, res_ref, g_ref, wrh_ref, wrl_ref, br_ref, hx_ref, *, rc):
    d = w_ref.shape[1]
    for c in range(a_ref.shape[0] // rc):
        rows = slice(c * rc, (c + 1) * rc)
        h = res_ref[rows, :] + jnp.dot(a_ref[rows, :], w_ref[...], preferred_element_type=F32)
        hx_ref[rows, :d] = h
        hx_ref[rows, d:] = _route(_rms(h, g_ref[...]), wrh_ref[...], wrl_ref[...], br_ref[...])


def outproj_route(a, w, res, g, wr, br, *, tm):
    n, kdim = a.shape
    d = w.shape[1]
    wr_hi, wr_lo = wr
    return pl.pallas_call(
        functools.partial(_outproj_kernel, rc=min(tm, 256)),
        grid=(n // tm,),
        in_specs=[
            pl.BlockSpec((tm, kdim), lambda i: (i, 0)),
            pl.BlockSpec((kdim, d), lambda i: (0, 0)),
            pl.BlockSpec((tm, d), lambda i: (i, 0)),
            pl.BlockSpec((1, d), lambda i: (0, 0)),
            pl.BlockSpec((d, ROUTE_LANES), lambda i: (0, 0)),
            pl.BlockSpec((d, ROUTE_LANES), lambda i: (0, 0)),
            pl.BlockSpec((1, ROUTE_LANES), lambda i: (0, 0)),
        ],
        out_specs=pl.BlockSpec((tm, HX_WIDTH), lambda i: (i, 0)),
        out_shape=jax.ShapeDtypeStruct((n, HX_WIDTH), F32),
        compiler_params=_cparams(("parallel",)),
        name="outproj_route",
    )(a, w, res, g, wr_hi, wr_lo, br)


def _group_experts(u, gate, grp, wg_ref, wu_ref, wd_ref):
    lane = lax.broadcasted_iota(jnp.int32, gate.shape, 1)
    contrib = None
    for e in range(EXPERTS_PER_GROUP):
        ge = jnp.sum(jnp.where(lane == grp * EXPERTS_PER_GROUP + e, gate, 0.0), axis=-1, keepdims=True)
        hg = jnp.dot(u, wg_ref[e], preferred_element_type=F32)
        hu = jnp.dot(u, wu_ref[e], preferred_element_type=F32)
        hid = (_silu(hg) * hu * ge).astype(BF16)
        d = jnp.dot(hid, wd_ref[e], preferred_element_type=F32)
        contrib = d if contrib is None else contrib + d
    return contrib


def _moe_kernel(*refs, sorted_rows, final_norm):
    if sorted_rows:
        tile_ref, group_ref, lo_ref, hi_ref, first_ref, last_ref = refs[:6]
        refs = refs[6:]
    if final_norm:
        hx_ref, gffn_ref, wg_ref, wu_ref, wd_ref, gfin_ref, out_ref, u_ref, acc_ref = refs
    else:
        hx_ref, gffn_ref, wg_ref, wu_ref, wd_ref, out_ref, u_ref, acc_ref = refs
    d = out_ref.shape[1]
    if sorted_rows:
        s = pl.program_id(0)
        grp = group_ref[s]
        is_first = first_ref[s] == 1
        is_last = last_ref[s] == 1
        lo = lo_ref[s]
        hi = hi_ref[s]
    else:
        grp = pl.program_id(1)
        is_first = grp == 0
        is_last = grp == pl.num_programs(1) - 1

    @pl.when(is_first)
    def _():
        u_ref[...] = _rms(hx_ref[:, :d], gffn_ref[...]).astype(BF16)
        acc_ref[...] = jnp.zeros_like(acc_ref)

    def accumulate():
        acc_ref[...] += _group_experts(u_ref[...], hx_ref[:, d:], grp, wg_ref, wu_ref, wd_ref)

    if sorted_rows:
        pl.when(hi > lo)(accumulate)
    else:
        accumulate()

    @pl.when(is_last)
    def _():
        h = hx_ref[:, :d] + acc_ref[...]
        out_ref[...] = _rms(h, gfin_ref[...]) if final_norm else h


def moe(hx, g_ffn, wg, wu, wd, g_final=None, items=None, *, tm):
    n = hx.shape[0]
    d = D_MODEL
    sorted_rows = items is not None
    final_norm = g_final is not None
    if sorted_rows:
        tile_of = lambda s, tile, group, *_: (tile[s], 0)
        group_of = lambda s, tile, group, *_: (group[s], 0, 0)
        const = lambda s, *_: (0, 0)
        grid = (items[0].shape[0],)
    else:
        tile_of = lambda i, g: (i, 0)
        group_of = lambda i, g: (g, 0, 0)
        const = lambda i, g: (0, 0)
        grid = (n // tm, N_GROUPS)
    in_specs = [
        pl.BlockSpec((tm, HX_WIDTH), tile_of),
        pl.BlockSpec((1, d), const),
        pl.BlockSpec((EXPERTS_PER_GROUP, d, D_EXPERT), group_of),
        pl.BlockSpec((EXPERTS_PER_GROUP, d, D_EXPERT), group_of),
        pl.BlockSpec((EXPERTS_PER_GROUP, D_EXPERT, d), group_of),
    ]
    args = [hx, g_ffn, wg, wu, wd]
    if final_norm:
        in_specs.append(pl.BlockSpec((1, d), const))
        args.append(g_final)
    return pl.pallas_call(
        functools.partial(_moe_kernel, sorted_rows=sorted_rows, final_norm=final_norm),
        grid_spec=pltpu.PrefetchScalarGridSpec(
            num_scalar_prefetch=6 if sorted_rows else 0,
            grid=grid,
            in_specs=in_specs,
            out_specs=pl.BlockSpec((tm, d), tile_of),
            scratch_shapes=[pltpu.VMEM((tm, d), BF16), pltpu.VMEM((tm, d), F32)],
        ),
        out_shape=jax.ShapeDtypeStruct((n, d), F32),
        compiler_params=_cparams(("arbitrary",) if sorted_rows else ("parallel", "arbitrary")),
        name="moe",
    )(*(tuple(items) if sorted_rows else ()), *args)


def _permute_kernel(idx_ref, src_ref, dst_ref, sem, *, ch, scatter):
    base = pl.program_id(0) * ch

    def row_copy(r):
        j = idx_ref[base + r]
        if scatter:
            return pltpu.make_async_copy(src_ref.at[pl.ds(r, 1), :], dst_ref.at[pl.ds(j, 1), :], sem)
        return pltpu.make_async_copy(src_ref.at[pl.ds(j, 1), :], dst_ref.at[pl.ds(r, 1), :], sem)

    def issue(r, carry):
        row_copy(r).start()
        return carry

    lax.fori_loop(0, ch, issue, 0, unroll=8)
    rows = pl.ds(base, ch)
    if scatter:
        pltpu.make_async_copy(src_ref, dst_ref.at[rows, :], sem).wait()
    else:
        pltpu.make_async_copy(src_ref.at[rows, :], dst_ref, sem).wait()


def permute_rows(src, idx, *, scatter, ch=2048):
    n, width = src.shape
    ch = min(ch, n)
    tile = pl.BlockSpec((ch, width), lambda i, idx: (i, 0))
    hbm = pl.BlockSpec(memory_space=pl.ANY)
    return pl.pallas_call(
        functools.partial(_permute_kernel, ch=ch, scatter=scatter),
        grid_spec=pltpu.PrefetchScalarGridSpec(
            num_scalar_prefetch=1,
            grid=(n // ch,),
            in_specs=[tile if scatter else hbm],
            out_specs=hbm if scatter else tile,
            scratch_shapes=[pltpu.SemaphoreType.DMA(())],
        ),
        out_shape=jax.ShapeDtypeStruct((n, width), src.dtype),
        compiler_params=_cparams(("arbitrary",)),
        name="permute_rows",
    )(idx, src)


def _rope_cols(y, cosf, sina, sinb):
    blocks = []
    for c in range(y.shape[1] // LANES):
        xb = y[:, c * LANES:(c + 1) * LANES]
        blocks.append(xb * cosf + pltpu.roll(xb, LANES - ROPE_DIM // 2, 1) * sina
                      + pltpu.roll(xb, ROPE_DIM // 2, 1) * sinb)
    return blocks


def _rope_rows(kt, cost, sint):
    half = ROPE_DIM // 2
    parts = []
    for g in range(kt.shape[0] // DIFF_DH):
        b0 = g * DIFF_DH
        x1 = kt[b0:b0 + half]
        x2 = kt[b0 + half:b0 + ROPE_DIM]
        parts += [x1 * cost - x2 * sint, x2 * cost + x1 * sint, kt[b0 + ROPE_DIM:b0 + DIFF_DH]]
    return jnp.concatenate(parts, axis=0)


def _qkv_kernel(*refs, tm, k_feature_major):
    if k_feature_major:
        (h_ref, gn_ref, wk_ref, wv_ref, wq_ref, cos_ref, sa_ref, sb_ref, cost_ref, sint_ref,
         k32_ref, v32_ref, kb_ref, vb_ref, qb_ref) = refs
    else:
        (h_ref, gn_ref, wk_ref, wv_ref, wq_ref, cos_ref, sa_ref, sb_ref,
         k32_ref, v32_ref, kb_ref, vb_ref, qb_ref) = refs
    cosf = cos_ref[...]
    sina = sa_ref[...]
    sinb = sb_ref[...]
    h = h_ref[...]
    inv = lax.rsqrt(jnp.mean(h * h, axis=-1, keepdims=True) + EPS)
    ukv = (h * inv * gn_ref[0:1, :]).astype(BF16)
    umix = (h * inv * gn_ref[1:2, :]).astype(BF16)
    if k_feature_major:
        kt = lax.dot_general(wk_ref[...], ukv, (((1,), (1,)), ((), ())), preferred_element_type=F32)
        kt = _rope_rows(kt, cost_ref[...], sint_ref[...])
        k32_ref[0] = kt
        kb_ref[0, 0] = kt.astype(BF16)
    else:
        k = jnp.dot(ukv, wk_ref[...], preferred_element_type=F32)
        for c, blk in enumerate(_rope_cols(k, cosf, sina, sinb)):
            k32_ref[:, c * LANES:(c + 1) * LANES] = blk
            kb_ref[:, c * LANES:(c + 1) * LANES] = blk.astype(BF16)
    v = jnp.dot(ukv, wv_ref[...], preferred_element_type=F32)
    vb_ref[...] = v.astype(BF16)
    for h in range(DIFF_HEADS):
        v32_ref[pl.ds(h, tm, stride=DIFF_HEADS), :] = v[:, h * DIFF_DV:(h + 1) * DIFF_DV]
    q = jnp.dot(umix, wq_ref[...], preferred_element_type=F32)
    for c, blk in enumerate(_rope_cols(q, cosf, sina, sinb)):
        qb_ref[:, c * LANES:(c + 1) * LANES] = (blk * QK_SCALE_LOG2).astype(BF16)


def qkv_proj(h, gains, wk, wv, wq, cosf, sina, sinb, cost=None, sint=None, *, tm, batch, seq):
    n, d = h.shape
    k_feature_major = cost is not None
    nt = max(seq // tm, 1)
    n_pos_tiles = cosf.shape[0] // tm
    tok = lambda i: (i, 0)
    wspec = pl.BlockSpec((d, d), lambda i: (0, 0))
    tspec = pl.BlockSpec((tm, LANES), lambda i: (i % n_pos_tiles, 0))
    in_specs = [pl.BlockSpec((tm, d), tok), pl.BlockSpec((2, d), lambda i: (0, 0)), wspec, wspec, wspec,
                tspec, tspec, tspec]
    args = [h, gains, wk, wv, wq, cosf, sina, sinb]
    rows_f32 = (pl.BlockSpec((tm, d), tok), jax.ShapeDtypeStruct((n, d), F32))
    rows_bf16 = (pl.BlockSpec((tm, d), tok), jax.ShapeDtypeStruct((n, d), BF16))
    v32 = (pl.BlockSpec((tm * DIFF_HEADS, DIFF_DV), tok), jax.ShapeDtypeStruct((n * DIFF_HEADS, DIFF_DV), F32))
    if k_feature_major:
        half = ROPE_DIM // 2
        rspec = pl.BlockSpec((half, tm), lambda i: (0, i % nt))
        in_specs += [rspec, rspec]
        args += [cost, sint]
        k32 = (pl.BlockSpec((1, d, tm), lambda i: (i // nt, 0, i % nt)),
               jax.ShapeDtypeStruct((batch, d, seq), F32))
        kb = (pl.BlockSpec((1, 1, d, tm), lambda i: (i // nt, i % nt, 0, 0)),
              jax.ShapeDtypeStruct((batch, nt, d, tm), BF16))
    else:
        k32, kb = rows_f32, rows_bf16
    outs = [k32, v32, kb, rows_bf16, rows_bf16]
    return pl.pallas_call(
        functools.partial(_qkv_kernel, tm=tm, k_feature_major=k_feature_major),
        grid=(n // tm,),
        in_specs=in_specs,
        out_specs=[o[0] for o in outs],
        out_shape=[o[1] for o in outs],
        compiler_params=_cparams(("parallel",)),
        name="qkv_proj",
    )(*args)


def _lambda(lam_ref, lam_init):
    lv = lam_ref[...]
    a = jnp.sum(lv[0:1, :] * lv[1:2, :], axis=-1, keepdims=True)
    b = jnp.sum(lv[2:3, :] * lv[3:4, :], axis=-1, keepdims=True)
    return jnp.exp(a) - jnp.exp(b) + lam_init


def _split_q(q):
    qf = q.astype(F32)
    lane = lax.broadcasted_iota(jnp.int32, qf.shape, 1)
    q0 = jnp.where(lane < DIFF_DH, qf, 0.0).astype(BF16)
    q1 = jnp.where(lane >= DIFF_DH, qf, 0.0).astype(BF16)
    return jnp.concatenate([q0, q1], axis=0)


def _online_update(s, v, m_ref, l_ref, acc_ref):
    m_old = m_ref[...]
    if s.shape[1] < LANES:
        m_new = jnp.maximum(m_old, jnp.max(s, axis=-1, keepdims=True))
        alpha = jnp.exp2(m_old - m_new)
        p = jnp.exp2(s - m_new[:, :s.shape[1]])
        lane = lax.broadcasted_iota(jnp.int32, m_old.shape, 1)
        lsum = jnp.where(lane == 0, jnp.sum(p, axis=-1, keepdims=True), 0.0)
        l_ref[...] = alpha * l_ref[...] + lsum
        acc_ref[...] = alpha * acc_ref[...] + jnp.dot(p.astype(BF16), v, preferred_element_type=F32)
        m_ref[...] = m_new
        return
    ncol = s.shape[1] // LANES
    cols = [s[:, c * LANES:(c + 1) * LANES] for c in range(ncol)]
    mx = cols[0]
    for c in range(1, ncol):
        mx = jnp.maximum(mx, cols[c])
    m_new = jnp.maximum(m_old, jnp.max(mx, axis=-1, keepdims=True))
    alpha = jnp.exp2(m_old - m_new)
    ps = [jnp.exp2(col - m_new) for col in cols]
    lsum = ps[0]
    for c in range(1, ncol):
        lsum = lsum + ps[c]
    p = jnp.concatenate([pc.astype(BF16) for pc in ps], axis=1) if ncol > 1 else ps[0].astype(BF16)
    l_ref[...] = alpha * l_ref[...] + lsum
    acc_ref[...] = alpha * acc_ref[...] + jnp.dot(p, v, preferred_element_type=F32)
    m_ref[...] = m_new


def _diff_finish(l_ref, acc_ref, lam, subln, lam_init, tq):
    o = acc_ref[...] / jnp.sum(l_ref[...], axis=-1, keepdims=True)
    a = o[:tq] - lam * o[tq:]
    return _rms(a, subln) * (1.0 - lam_init)


def _attn_prompt_kernel(q_ref, k_ref, v_ref, lam_ref, sub_ref, o_ref, q2_ref, m_ref, l_ref, acc_ref,
                        *, tq, lam_init):
    i = pl.program_id(2)
    q2_ref[...] = _split_q(q_ref[...])
    m_ref[...] = jnp.full_like(m_ref, NEG_BIG)
    l_ref[...] = jnp.zeros_like(l_ref)
    acc_ref[...] = jnp.zeros_like(acc_ref)

    def step(j, nblk, diagonal_last):
        r0 = pl.multiple_of(j * tq, tq)
        v = v_ref[pl.ds(r0, nblk * tq), :]
        q2 = q2_ref[...]
        parts = [jnp.dot(q2, k_ref[0, j + t], preferred_element_type=F32) for t in range(nblk)]
        if diagonal_last:
            d = parts[-1]
            qrow = lax.broadcasted_iota(jnp.int32, d.shape, 0) % tq
            col = lax.broadcasted_iota(jnp.int32, d.shape, 1)
            parts[-1] = jnp.where(col < (qrow // CHUNK + 1) * CHUNK, d, NEG_BIG)
        s = parts[0] if nblk == 1 else jnp.concatenate(parts, axis=1)
        _online_update(s, v, m_ref, l_ref, acc_ref)

    def body(jj, carry):
        step(MAIN_BLOCKS * jj, MAIN_BLOCKS, False)
        return carry

    n_main = i // MAIN_BLOCKS
    lax.fori_loop(0, n_main, body, 0)
    for rem in range(MAIN_BLOCKS):
        pl.when(i % MAIN_BLOCKS == rem)(
            functools.partial(step, n_main * MAIN_BLOCKS, rem + 1, True))

    lam = _lambda(lam_ref, lam_init)
    o_ref[...] = _diff_finish(l_ref, acc_ref, lam, sub_ref[...], lam_init, tq).astype(BF16)


def attn_prompt(qb, kb, vb, lamv, subln, *, batch, seq, tq, lam_init):
    n = batch * seq
    nq = seq // tq
    kblk = tq
    return pl.pallas_call(
        functools.partial(_attn_prompt_kernel, tq=tq, lam_init=lam_init),
        grid=(batch, DIFF_HEADS, nq),
        in_specs=[
            pl.BlockSpec((tq, LANES), lambda b, h, i: (b * nq + i, h)),
            pl.BlockSpec((1, seq // kblk, LANES, kblk), lambda b, h, i: (b, 0, h, 0)),
            pl.BlockSpec((seq, LANES), lambda b, h, i: (b, h)),
            pl.BlockSpec((8, LANES), lambda b, h, i: (0, 0)),
            pl.BlockSpec((1, LANES), lambda b, h, i: (0, 0)),
        ],
        out_specs=pl.BlockSpec((tq, LANES), lambda b, h, i: (b * nq + i, h)),
        out_shape=jax.ShapeDtypeStruct((n, DIFF_HEADS * DIFF_DV), BF16),
        scratch_shapes=[pltpu.VMEM((2 * tq, LANES), BF16), pltpu.VMEM((2 * tq, LANES), F32),
                        pltpu.VMEM((2 * tq, LANES), F32), pltpu.VMEM((2 * tq, LANES), F32)],
        compiler_params=_cparams(("parallel", "parallel", "arbitrary")),
        name="attn_prompt",
    )(qb, kb, vb, lamv, subln)


def _attn_sample_kernel(q_ref, ck_ref, cv_ref, kn_ref, vn_ref, lam_ref, sub_ref, o_ref,
                        q2_ref, m_ref, l_ref, acc_ref, *, tq, past, lam_init):
    j = pl.program_id(1)
    last = pl.num_programs(1) - 1

    @pl.when(j == 0)
    def _():
        for h in range(DIFF_HEADS):
            q2_ref[h] = _split_q(q_ref[:, h * LANES:(h + 1) * LANES])
        m_ref[...] = jnp.full_like(m_ref, NEG_BIG)
        l_ref[...] = jnp.zeros_like(l_ref)
        acc_ref[...] = jnp.zeros_like(acc_ref)

    @pl.when(j < last)
    def _():
        tk = ck_ref.shape[-1]
        scores = [jnp.dot(q2_ref[h], ck_ref[0, h].astype(BF16), preferred_element_type=F32)
                  for h in range(DIFF_HEADS)]
        for h in range(DIFF_HEADS):
            v = cv_ref[pl.ds(h, tk, stride=DIFF_HEADS), :].astype(BF16)
            _online_update(scores[h], v, m_ref.at[h], l_ref.at[h], acc_ref.at[h])

    @pl.when(j == last)
    def _():
        lam = _lambda(lam_ref, lam_init)
        for h in range(DIFF_HEADS):
            k = kn_ref[:, h * LANES:(h + 1) * LANES]
            v = vn_ref[:, h * LANES:(h + 1) * LANES]
            s = lax.dot_general(q2_ref[h], k, (((1,), (1,)), ((), ())), preferred_element_type=F32)
            qpos = past + lax.broadcasted_iota(jnp.int32, s.shape, 0) % tq
            kpos = past + lax.broadcasted_iota(jnp.int32, s.shape, 1)
            s = jnp.where(kpos < (qpos // CHUNK + 1) * CHUNK, s, NEG_BIG)
            _online_update(s, v, m_ref.at[h], l_ref.at[h], acc_ref.at[h])
            o_ref[:, h * LANES:(h + 1) * LANES] = _diff_finish(
                l_ref.at[h], acc_ref.at[h], lam, sub_ref[...], lam_init, tq).astype(BF16)


def attn_sample(qb, cache_k, cache_v, kb, vb, lamv, subln, *, batch, seq, tk, lam_init):
    past = cache_k.shape[-1]
    d = DIFF_HEADS * DIFF_DV
    nk = past // tk
    ck_spec = pl.BlockSpec((1, DIFF_HEADS, LANES, tk), lambda b, j: (b, 0, 0, jnp.minimum(j, nk - 1)))
    cv_spec = pl.BlockSpec((tk * DIFF_HEADS, DIFF_DV), lambda b, j: (b * nk + jnp.minimum(j, nk - 1), 0))
    row_spec = pl.BlockSpec((seq, d), lambda b, j: (b, 0))
    return pl.pallas_call(
        functools.partial(_attn_sample_kernel, tq=seq, past=past, lam_init=lam_init),
        grid=(batch, nk + 1),
        in_specs=[row_spec, ck_spec, cv_spec, row_spec, row_spec,
                  pl.BlockSpec((8, LANES), lambda b, j: (0, 0)),
                  pl.BlockSpec((1, LANES), lambda b, j: (0, 0))],
        out_specs=row_spec,
        out_shape=jax.ShapeDtypeStruct((batch * seq, d), BF16),
        scratch_shapes=[pltpu.VMEM((DIFF_HEADS, 2 * seq, LANES), BF16),
                        pltpu.VMEM((DIFF_HEADS, 2 * seq, LANES), F32),
                        pltpu.VMEM((DIFF_HEADS, 2 * seq, LANES), F32),
                        pltpu.VMEM((DIFF_HEADS, 2 * seq, LANES), F32)],
        compiler_params=_cparams(("parallel", "arbitrary")),
        name="attn_sample",
    )(qb, cache_k, cache_v, kb, vb, lamv, subln)


def _angles(pos, half, theta):
    pos = np.asarray(pos).astype(np.float64)
    inv_freq = np.power(np.float64(theta), -np.arange(half, dtype=np.float64) / half)
    return pos[:, None] * inv_freq[None, :]


def _ret_rope_tables(pos):
    ang = _angles(pos, RET_DK // 2, RET_THETA)
    return np.cos(ang).astype(np.float32), np.sin(ang).astype(np.float32)


def _diff_rope_tables(pos):
    half = ROPE_DIM // 2
    ang = _angles(pos, half, ROPE_THETA)
    cos, sin = np.cos(ang).astype(np.float32), np.sin(ang).astype(np.float32)
    t = cos.shape[0]
    pad = DIFF_DH - ROPE_DIM
    cos64 = np.concatenate([cos, cos, np.ones((t, pad), np.float32)], axis=1)
    sina64 = np.concatenate([-sin, np.zeros((t, DIFF_DH - half), np.float32)], axis=1)
    sinb64 = np.concatenate([np.zeros((t, half), np.float32), sin, np.zeros((t, pad), np.float32)], axis=1)
    rep = LANES // DIFF_DH
    return (np.tile(cos64, (1, rep)), np.tile(sina64, (1, rep)), np.tile(sinb64, (1, rep)),
            np.ascontiguousarray(cos.T), np.ascontiguousarray(sin.T))


def _prep_params(p):
    w = {}
    w['ret_w_in'] = p['ret_w_in'][0].astype(BF16)
    w['ret_w_out'] = p['ret_w_out'][0].astype(BF16)
    w['kv_w_k'] = p['kv_w_k'].astype(BF16)
    w['kv_w_k_t'] = p['kv_w_k'].T.astype(BF16)
    w['kv_w_v'] = p['kv_w_v'].astype(BF16)
    w['diff_w_q'] = p['diff_w_q'][0].astype(BF16)
    w['diff_w_o'] = p['diff_w_o'][0].astype(BF16)
    w['moe_w_gate'] = p['moe_w_gate'].astype(BF16)
    w['moe_w_up'] = p['moe_w_up'].astype(BF16)
    w['moe_w_down'] = p['moe_w_down'].astype(BF16)
    wr, br = [], []
    for layer in range(2):
        route = jnp.transpose(p['moe_w_route'][layer], (1, 0, 2)).reshape(D_MODEL, N_EXPERTS)
        cols = jnp.concatenate([route, p['moe_w_group'][layer]], axis=1)
        full = jnp.pad(cols, ((0, 0), (0, ROUTE_LANES - cols.shape[1])))
        hi = full.astype(BF16)
        wr.append((hi, (full - hi.astype(F32)).astype(BF16)))
        bias = jnp.concatenate([p['moe_b_route'][layer].reshape(-1), p['moe_b_group'][layer]])
        br.append(jnp.pad(bias, (0, ROUTE_LANES - bias.shape[0]))[None, :])
    w['route_w'] = wr
    w['route_b'] = br
    lamv = jnp.concatenate([p['diff_lam_q1'][0][None], p['diff_lam_k1'][0][None],
                            p['diff_lam_q2'][0][None], p['diff_lam_k2'][0][None]], axis=0)
    w['lamv'] = jnp.pad(lamv, ((0, 4), (0, LANES - DIFF_DH)))
    return w


def _group_sort_tables(hx, tm):
    n = hx.shape[0]
    n_tiles = n // tm
    i32 = jnp.int32
    g = hx[:, D_MODEL + GROUP_ID_LANE].astype(i32)
    onehot = (g[:, None] == jnp.arange(N_GROUPS, dtype=i32)[None, :]).astype(i32)
    csum = jnp.cumsum(onehot, axis=0)
    ends = jnp.cumsum(csum[-1])
    starts = ends - csum[-1]
    pos = jnp.sum(onehot * (starts[None, :] + csum - 1), axis=1)
    inner_ends = ends[:N_GROUPS - 1]
    lo = jnp.sort(jnp.concatenate([jnp.arange(n_tiles, dtype=i32) * tm, inner_ends]))
    hi = jnp.concatenate([lo[1:], jnp.full((1,), n, i32)])
    tile = jnp.minimum(lo // tm, n_tiles - 1)
    group = jnp.sum((inner_ends[None, :] <= lo[:, None]).astype(i32), axis=1)
    change = (tile[1:] != tile[:-1]).astype(i32)
    one = jnp.ones((1,), i32)
    return pos, (tile, group, lo, hi, jnp.concatenate([one, change]), jnp.concatenate([change, one]))


def _moe_layer(hx, g_ffn, wg, wu, wd, g_final, *, tm):
    if hx.shape[0] < 8 * tm:
        return moe(hx, g_ffn, wg, wu, wd, g_final, tm=tm)
    pos, items = _group_sort_tables(hx, tm)
    xs = permute_rows(hx, pos, scatter=True)
    ys = moe(xs, g_ffn, wg, wu, wd, g_final, items, tm=tm)
    return permute_rows(ys, pos, scatter=False)


def _trunk(x, pos, r0, past_k, past_v, p, w, *, tm, ret_chunk, ret_tb, attn_tq=None, attn_tk=None):
    batch, seq, d = x.shape
    n = batch * seq
    xf = x.reshape(n, d)
    row = lambda v: v.reshape(1, -1)

    cos, sin = _ret_rope_tables(pos)
    if seq < tm:
        cos, sin = np.tile(cos, (tm // seq, 1)), np.tile(sin, (tm // seq, 1))
    proj = ret_inproj(xf, row(p['norm_mix'][0]), w['ret_w_in'], cos, sin, tm=min(2 * tm, seq, n) if seq >= tm else tm)
    o_gated, r_new = retention(proj, r0, batch=batch, seq=seq, L=ret_chunk, tb=ret_tb)
    hx1 = outproj_route(o_gated, w['ret_w_out'], xf, row(p['norm_ffn'][0]),
                        w['route_w'][0], w['route_b'][0], tm=tm)
    h2 = _moe_layer(hx1, row(p['norm_ffn'][0]), w['moe_w_gate'][0], w['moe_w_up'][0], w['moe_w_down'][0],
                    None, tm=tm)
    gn = jnp.stack([p['kv_norm'], p['norm_mix'][1]])

    cosf, sina, sinb, cost, sint = _diff_rope_tables(pos)
    if seq < tm:
        rep = (tm // seq, 1)
        cosf, sina, sinb = np.tile(cosf, rep), np.tile(sina, rep), np.tile(sinb, rep)
    lam_init = 0.8 - 0.6 * math.exp(-0.3 * 1)
    subln = row(p['diff_subln'][0])
    if past_k is None:
        assert attn_tq == tm
        k32, v32, kb, vb, qb = qkv_proj(h2, gn, w['kv_w_k_t'], w['kv_w_v'], w['diff_w_q'],
                                        cosf, sina, sinb, cost, sint, tm=tm, batch=batch, seq=seq)
        attn = attn_prompt(qb, kb, vb, w['lamv'], subln, batch=batch, seq=seq, tq=attn_tq,
                           lam_init=lam_init)
        k_out = jnp.transpose(k32.reshape(batch, DIFF_HEADS, 2, DIFF_DH, seq), (0, 4, 1, 2, 3))
    else:
        past = past_k.shape[1]
        k32, v32, kb, vb, qb = qkv_proj(h2, gn, w['kv_w_k'], w['kv_w_v'], w['diff_w_q'],
                                        cosf, sina, sinb, tm=tm, batch=batch, seq=seq)
        ck = jnp.transpose(past_k, (0, 2, 3, 4, 1)).reshape(batch, DIFF_HEADS, 2 * DIFF_DH, past)
        cv = past_v.reshape(batch * past * DIFF_HEADS, DIFF_DV)
        attn = attn_sample(qb, ck, cv, kb, vb, w['lamv'], subln, batch=batch, seq=seq, tk=attn_tk,
                           lam_init=lam_init)
        k_out = k32.reshape(batch, seq, DIFF_HEADS, 2, DIFF_DH)
    hx3 = outproj_route(attn, w['diff_w_o'], h2, row(p['norm_ffn'][1]),
                        w['route_w'][1], w['route_b'][1], tm=tm)
    y = _moe_layer(hx3, row(p['norm_ffn'][1]), w['moe_w_gate'][1], w['moe_w_up'][1], w['moe_w_down'][1],
                   row(p['norm_final']), tm=tm)

    return (y.reshape(batch, seq, d), r_new[None], k_out, v32.reshape(batch, seq, DIFF_HEADS, DIFF_DV))


def kernel(x_prompt, x_sample, state_ret, cache_k, cache_v, norm_mix, norm_ffn, norm_final, ret_w_in, ret_w_out, kv_norm, kv_w_k, kv_w_v, diff_w_q, diff_lam_q1, diff_lam_k1, diff_lam_q2, diff_lam_k2, diff_subln, diff_w_o, moe_w_group, moe_b_group, moe_w_route, moe_b_route, moe_w_gate, moe_w_up, moe_w_down):
    p = {
        'norm_mix': norm_mix, 'norm_ffn': norm_ffn, 'norm_final': norm_final,
        'ret_w_in': ret_w_in, 'ret_w_out': ret_w_out,
        'kv_norm': kv_norm, 'kv_w_k': kv_w_k, 'kv_w_v': kv_w_v,
        'diff_w_q': diff_w_q, 'diff_lam_q1': diff_lam_q1, 'diff_lam_k1': diff_lam_k1,
        'diff_lam_q2': diff_lam_q2, 'diff_lam_k2': diff_lam_k2, 'diff_subln': diff_subln,
        'diff_w_o': diff_w_o,
        'moe_w_group': moe_w_group, 'moe_b_group': moe_b_group, 'moe_w_route': moe_w_route,
        'moe_b_route': moe_b_route, 'moe_w_gate': moe_w_gate, 'moe_w_up': moe_w_up,
        'moe_w_down': moe_w_down,
    }
    w = _prep_params(p)
    tp = x_prompt.shape[1]
    ts = x_sample.shape[1]
    past = cache_k.shape[1]
    y_p, r_p, k_p, v_p = _trunk(x_prompt, np.arange(tp), None, None, None, p, w,
                                tm=512, ret_chunk=256, ret_tb=512, attn_tq=512)
    y_s, r_s, k_s, v_s = _trunk(x_sample, past + np.arange(ts), state_ret[0], cache_k, cache_v, p, w,
                                tm=x_sample.shape[0] * ts, ret_chunk=ts, ret_tb=ts, attn_tk=1024)
    return (y_p, y_s, r_p, k_p, v_p, r_s, k_s, v_s)
```

```python
import functools
import math

import numpy as np
import jax
import jax.numpy as jnp
from jax import lax
from jax.experimental import pallas as pl
from jax.experimental.pallas import tpu as pltpu

F32 = jnp.float32
BF16 = jnp.bfloat16

D_MODEL = 1024
CHUNK = 64
RET_HEADS = 4
RET_DK = 256
RET_DV = 512
RET_THETA = 10000.0
DIFF_HEADS = 8
DIFF_DH = 64
DIFF_DV = 128
ROPE_THETA = 500000.0
ROPE_DIM = 16
N_GROUPS = 4
EXPERTS_PER_GROUP = 4
N_EXPERTS = 16
D_EXPERT = 256
EPS = 1e-6

LANES = 128
VMEM_LIMIT = 56 * 1024 * 1024
ROUTE_LANES = 128
GROUP_LANE0 = N_EXPERTS
GROUP_ID_LANE = GROUP_LANE0 + N_GROUPS
HX_WIDTH = D_MODEL + ROUTE_LANES
NEG_BIG = -1e30
QK_SCALE_LOG2 = DIFF_DH ** -0.5 * math.log2(math.e)
MAIN_BLOCKS = 4


def _cparams(sem):
    return pltpu.CompilerParams(dimension_semantics=sem, vmem_limit_bytes=VMEM_LIMIT)


def _rms(x, g):
    ms = jnp.mean(x * x, axis=-1, keepdims=True)
    return x * lax.rsqrt(ms + EPS) * g


def _silu(x):
    return x * (1.0 / (1.0 + jnp.exp(-x)))


def _ret_inproj_kernel(x_ref, g_ref, w_ref, cos_ref, sin_ref, o_ref, xn_ref, *, tn):
    j = pl.program_id(1)
    n_q = (RET_HEADS * RET_DK) // tn
    n_qk = 2 * n_q
    n_qkv = n_qk + (RET_HEADS * RET_DV) // tn

    @pl.when(j == 0)
    def _():
        xn_ref[...] = _rms(x_ref[...], g_ref[...]).astype(BF16)

    def proj():
        return jnp.dot(xn_ref[...], w_ref[...], preferred_element_type=F32)

    @pl.when(j < n_qk)
    def _():
        y = proj()
        cos = cos_ref[...]
        sin = sin_ref[...]
        scale = jnp.where(j >= n_q, RET_DK ** -0.5, 1.0).astype(F32)
        half = RET_DK // 2
        for h in range(tn // RET_DK):
            x1 = y[:, h * RET_DK:h * RET_DK + half]
            x2 = y[:, h * RET_DK + half:(h + 1) * RET_DK]
            o_ref[:, h * RET_DK:h * RET_DK + half] = ((x1 * cos - x2 * sin) * scale).astype(BF16)
            o_ref[:, h * RET_DK + half:(h + 1) * RET_DK] = ((x2 * cos + x1 * sin) * scale).astype(BF16)

    @pl.when(jnp.logical_and(j >= n_qk, j < n_qkv))
    def _():
        o_ref[...] = proj().astype(BF16)

    @pl.when(j >= n_qkv)
    def _():
        o_ref[...] = _silu(proj()).astype(BF16)


def ret_inproj(x, g, w, cos, sin, *, tm, tn=1024):
    n, d = x.shape
    m = w.shape[1]
    n_pos_tiles = cos.shape[0] // tm
    return pl.pallas_call(
        functools.partial(_ret_inproj_kernel, tn=tn),
        grid=(n // tm, m // tn),
        in_specs=[
            pl.BlockSpec((tm, d), lambda i, j: (i, 0)),
            pl.BlockSpec((1, d), lambda i, j: (0, 0)),
            pl.BlockSpec((d, tn), lambda i, j: (0, j)),
            pl.BlockSpec((tm, RET_DK // 2), lambda i, j: (i % n_pos_tiles, 0)),
            pl.BlockSpec((tm, RET_DK // 2), lambda i, j: (i % n_pos_tiles, 0)),
        ],
        out_specs=pl.BlockSpec((tm, tn), lambda i, j: (i, j)),
        out_shape=jax.ShapeDtypeStruct((n, m), BF16),
        scratch_shapes=[pltpu.VMEM((tm, d), BF16)],
        compiler_params=_cparams(("parallel", "arbitrary")),
        name="ret_inproj",
    )(x, g, w, cos, sin)


def _retention_kernel(*refs, L, n_chunks, has_r0):
    if has_r0:
        (q_ref, k_ref, v_ref, g_ref, dm_ref, xi_ref, zt_ref, r0_ref, o_ref, rout_ref, r_ref) = refs
    else:
        (q_ref, k_ref, v_ref, g_ref, dm_ref, xi_ref, zt_ref, o_ref, rout_ref, r_ref) = refs
    t = pl.program_id(1)

    @pl.when(t == 0)
    def _():
        if has_r0:
            r_ref[...] = r0_ref[0]
        else:
            r_ref[...] = jnp.zeros_like(r_ref)

    def chunk(c, carry):
        r0 = pl.multiple_of(c * L, L)
        for h in range(RET_HEADS):
            lg = math.log1p(-2.0 ** (-5.0 - h))
            q = q_ref[pl.ds(r0, L), h * RET_DK:(h + 1) * RET_DK]
            k = k_ref[pl.ds(r0, L), h * RET_DK:(h + 1) * RET_DK]
            v = v_ref[pl.ds(r0, L), h * RET_DV:(h + 1) * RET_DV]
            gate = g_ref[pl.ds(r0, L), h * RET_DV:(h + 1) * RET_DV]
            rh = r_ref[h]
            s = lax.dot_general(q, k, (((1,), (1,)), ((), ())), preferred_element_type=F32)
            s = s * dm_ref[h]
            intra = jnp.dot(s.astype(BF16), v, preferred_element_type=F32)
            inter = jnp.dot(q, rh.astype(BF16), preferred_element_type=F32) * xi_ref[h]
            o = intra + inter
            kz = (k.astype(F32) * zt_ref[h]).astype(BF16)
            upd = lax.dot_general(kz, v, (((0,), (0,)), ((), ())), preferred_element_type=F32)
            r_ref[h] = math.exp(lg * L) * rh + upd
            on = o * lax.rsqrt(jnp.mean(o * o, axis=-1, keepdims=True) + EPS)
            o_ref[pl.ds(r0, L), h * RET_DV:(h + 1) * RET_DV] = (on * gate.astype(F32)).astype(BF16)
        return carry

    lax.fori_loop(0, n_chunks, chunk, 0)

    @pl.when(t == pl.num_programs(1) - 1)
    def _():
        rout_ref[0] = r_ref[...]


def _retention_tables(L):
    lg = np.log1p(-np.exp2(-5.0 - np.arange(RET_HEADS, dtype=np.float64)))
    i = np.arange(L, dtype=np.float64)
    diff = i[:, None] - i[None, :]
    dmask = np.where(diff >= 0, np.exp(lg[:, None, None] * np.maximum(diff, 0.0)), 0.0)
    xi = np.exp(lg[:, None] * (i[None, :] + 1.0))
    zeta = np.exp(lg[:, None] * (L - 1.0 - i[None, :]))
    xi = np.broadcast_to(xi[:, :, None], (RET_HEADS, L, RET_DV))
    zeta = np.broadcast_to(zeta[:, :, None], (RET_HEADS, L, RET_DK))
    return (jnp.asarray(dmask, F32), jnp.asarray(xi, F32), jnp.asarray(zeta, F32))


def retention(proj, r0, *, batch, seq, L, tb):
    n = batch * seq
    nt = seq // tb
    dmask, xi, zeta = _retention_tables(L)
    dqk = RET_HEADS * RET_DK
    dv = RET_HEADS * RET_DV
    has_r0 = r0 is not None
    const3 = lambda b, t: (0, 0, 0)
    in_specs = [
        pl.BlockSpec((tb, dqk), lambda b, t: (b * nt + t, 0)),
        pl.BlockSpec((tb, dqk), lambda b, t: (b * nt + t, 1)),
        pl.BlockSpec((tb, dv), lambda b, t: (b * nt + t, 1)),
        pl.BlockSpec((tb, dv), lambda b, t: (b * nt + t, 2)),
        pl.BlockSpec((RET_HEADS, L, L), const3),
        pl.BlockSpec((RET_HEADS, L, RET_DV), const3),
        pl.BlockSpec((RET_HEADS, L, RET_DK), const3),
    ]
    args = [proj, proj, proj, proj, dmask, xi, zeta]
    state_spec = pl.BlockSpec((1, RET_HEADS, RET_DK, RET_DV), lambda b, t: (b, 0, 0, 0))
    if has_r0:
        in_specs.append(state_spec)
        args.append(r0)
    return pl.pallas_call(
        functools.partial(_retention_kernel, L=L, n_chunks=tb // L, has_r0=has_r0),
        grid=(batch, nt),
        in_specs=in_specs,
        out_specs=[pl.BlockSpec((tb, dv), lambda b, t: (b * nt + t, 0)), state_spec],
        out_shape=[jax.ShapeDtypeStruct((n, dv), BF16),
                   jax.ShapeDtypeStruct((batch, RET_HEADS, RET_DK, RET_DV), F32)],
        scratch_shapes=[pltpu.VMEM((RET_HEADS, RET_DK, RET_DV), F32)],
        compiler_params=_cparams(("parallel", "arbitrary")),
        name="retention",
    )(*args)


def _route(u, wr_hi, wr_lo, br):
    u_hi = u.astype(BF16)
    u_lo = (u - u_hi.astype(F32)).astype(BF16)
    logits = (jnp.dot(u_hi, wr_hi, preferred_element_type=F32)
              + jnp.dot(u_lo, wr_hi, preferred_element_type=F32)
              + jnp.dot(u_hi, wr_lo, preferred_element_type=F32)) + br
    lane_i = lax.broadcasted_iota(jnp.int32, logits.shape, 1)
    lane = lane_i.astype(F32)
    lane_grp = (lane_i // EXPERTS_PER_GROUP).astype(F32)
    big = jnp.float32(1 << 20)
    is_grp = jnp.logical_and(lane_i >= GROUP_LANE0, lane_i < GROUP_LANE0 + N_GROUPS)
    gl = jnp.where(is_grp, logits, -jnp.inf)
    gmax = jnp.max(gl, axis=-1, keepdims=True)
    gsum = jnp.sum(jnp.exp(gl - gmax), axis=-1, keepdims=True)
    p_top = 1.0 / gsum
    g_top = jnp.min(jnp.where(gl == gmax, lane, big), axis=-1, keepdims=True) - GROUP_LANE0
    sel = jnp.logical_and(lane_i < N_EXPERTS, lane_grp == g_top)
    el = jnp.where(sel, logits, -jnp.inf)
    emax = jnp.max(el, axis=-1, keepdims=True)
    ee = jnp.exp(el - emax)
    ep = ee / jnp.sum(ee, axis=-1, keepdims=True)
    epm = jnp.where(sel, ep, -1.0)
    m1 = jnp.max(epm, axis=-1, keepdims=True)
    i1 = jnp.min(jnp.where(epm == m1, lane, big), axis=-1, keepdims=True)
    epm2 = jnp.where(lane == i1, -1.0, epm)
    m2 = jnp.max(epm2, axis=-1, keepdims=True)
    i2 = jnp.min(jnp.where(epm2 == m2, lane, big), axis=-1, keepdims=True)
    denom = m1 + m2
    w1 = m1 / denom * p_top
    w2 = m2 / denom * p_top
    gate = jnp.where(lane == i1, w1, jnp.where(lane == i2, w2, 0.0))
    return jnp.where(lane_i == GROUP_ID_LANE, g_top, gate)


def _outproj_kernel(a_ref, w_ref, res_ref, g_ref, wrh_ref, wrl_ref, br_ref, hx_ref, *, rc):
    d = w_ref.shape[1]
    for c in range(a_ref.shape[0] // rc):
        rows = slice(c * rc, (c + 1) * rc)
        h = res_ref[rows, :] + jnp.dot(a_ref[rows, :], w_ref[...], preferred_element_type=F32)
        hx_ref[rows, :d] = h
        hx_ref[rows, d:] = _route(_rms(h, g_ref[...]), wrh_ref[...], wrl_ref[...], br_ref[...])


def outproj_route(a, w, res, g, wr, br, *, tm):
    n, kdim = a.shape
    d = w.shape[1]
    wr_hi, wr_lo = wr
    return pl.pallas_call(
        functools.partial(_outproj_kernel, rc=tm),
        grid=(n // tm,),
        in_specs=[
            pl.BlockSpec((tm, kdim), lambda i: (i, 0)),
            pl.BlockSpec((kdim, d), lambda i: (0, 0)),
            pl.BlockSpec((tm, d), lambda i: (i, 0)),
            pl.BlockSpec((1, d), lambda i: (0, 0)),
            pl.BlockSpec((d, ROUTE_LANES), lambda i: (0, 0)),
            pl.BlockSpec((d, ROUTE_LANES), lambda i: (0, 0)),
            pl.BlockSpec((1, ROUTE_LANES), lambda i: (0, 0)),
        ],
        out_specs=pl.BlockSpec((tm, HX_WIDTH), lambda i: (i, 0)),
        out_shape=jax.ShapeDtypeStruct((n, HX_WIDTH), F32),
        compiler_params=_cparams(("parallel",)),
        name="outproj_route",
    )(a, w, res, g, wr_hi, wr_lo, br)


def _group_experts(u, gate, grp, wg_ref, wu_ref, wd_ref):
    lane = lax.broadcasted_iota(jnp.int32, gate.shape, 1)
    contrib = None
    for e in range(EXPERTS_PER_GROUP):
        ge = jnp.sum(jnp.where(lane == grp * EXPERTS_PER_GROUP + e, gate, 0.0), axis=-1, keepdims=True)
        hg = jnp.dot(u, wg_ref[e], preferred_element_type=F32)
        hu = jnp.dot(u, wu_ref[e], preferred_element_type=F32)
        hid = (_silu(hg) * hu * ge).astype(BF16)
        d = jnp.dot(hid, wd_ref[e], preferred_element_type=F32)
        contrib = d if contrib is None else contrib + d
    return contrib


def _moe_kernel(*refs, sorted_rows, final_norm):
    if sorted_rows:
        tile_ref, group_ref, lo_ref, hi_ref, first_ref, last_ref = refs[:6]
        refs = refs[6:]
    if final_norm:
        hx_ref, gffn_ref, wg_ref, wu_ref, wd_ref, gfin_ref, out_ref, u_ref, acc_ref = refs
    else:
        hx_ref, gffn_ref, wg_ref, wu_ref, wd_ref, out_ref, u_ref, acc_ref = refs
    d = out_ref.shape[1]
    if sorted_rows:
        s = pl.program_id(0)
        grp = group_ref[s]
        is_first = first_ref[s] == 1
        is_last = last_ref[s] == 1
        lo = lo_ref[s]
        hi = hi_ref[s]
    else:
        grp = pl.program_id(1)
        is_first = grp == 0
        is_last = grp == pl.num_programs(1) - 1

    @pl.when(is_first)
    def _():
        u_ref[...] = _rms(hx_ref[:, :d], gffn_ref[...]).astype(BF16)
        acc_ref[...] = jnp.zeros_like(acc_ref)

    def accumulate():
        acc_ref[...] += _group_experts(u_ref[...], hx_ref[:, d:], grp, wg_ref, wu_ref, wd_ref)

    if sorted_rows:
        pl.when(hi > lo)(accumulate)
    else:
        accumulate()

    @pl.when(is_last)
    def _():
        h = hx_ref[:, :d] + acc_ref[...]
        out_ref[...] = _rms(h, gfin_ref[...]) if final_norm else h


def moe(hx, g_ffn, wg, wu, wd, g_final=None, items=None, *, tm):
    n = hx.shape[0]
    d = D_MODEL
    sorted_rows = items is not None
    final_norm = g_final is not None
    if sorted_rows:
        tile_of = lambda s, tile, group, *_: (tile[s], 0)
        group_of = lambda s, tile, group, *_: (group[s], 0, 0)
        const = lambda s, *_: (0, 0)
        grid = (items[0].shape[0],)
    else:
        tile_of = lambda i, g: (i, 0)
        group_of = lambda i, g: (g, 0, 0)
        const = lambda i, g: (0, 0)
        grid = (n // tm, N_GROUPS)
    in_specs = [
        pl.BlockSpec((tm, HX_WIDTH), tile_of),
        pl.BlockSpec((1, d), const),
        pl.BlockSpec((EXPERTS_PER_GROUP, d, D_EXPERT), group_of),
        pl.BlockSpec((EXPERTS_PER_GROUP, d, D_EXPERT), group_of),
        pl.BlockSpec((EXPERTS_PER_GROUP, D_EXPERT, d), group_of),
    ]
    args = [hx, g_ffn, wg, wu, wd]
    if final_norm:
        in_specs.append(pl.BlockSpec((1, d), const))
        args.append(g_final)
    return pl.pallas_call(
        functools.partial(_moe_kernel, sorted_rows=sorted_rows, final_norm=final_norm),
        grid_spec=pltpu.PrefetchScalarGridSpec(
            num_scalar_prefetch=6 if sorted_rows else 0,
            grid=grid,
            in_specs=in_specs,
            out_specs=pl.BlockSpec((tm, d), tile_of),
            scratch_shapes=[pltpu.VMEM((tm, d), BF16), pltpu.VMEM((tm, d), F32)],
        ),
        out_shape=jax.ShapeDtypeStruct((n, d), F32),
        compiler_params=_cparams(("arbitrary",) if sorted_rows else ("parallel", "arbitrary")),
        name="moe",
    )(*(tuple(items) if sorted_rows else ()), *args)


def _permute_kernel(idx_ref, src_ref, dst_ref, sem, *, ch, scatter):
    base = pl.program_id(0) * ch

    def row_copy(r):
        j = idx_ref[base + r]
        if scatter:
            return pltpu.make_async_copy(src_ref.at[pl.ds(r, 1), :], dst_ref.at[pl.ds(j, 1), :], sem)
        return pltpu.make_async_copy(src_ref.at[pl.ds(j, 1), :], dst_ref.at[pl.ds(r, 1), :], sem)

    def issue(r, carry):
        row_copy(r).start()
        return carry

    lax.fori_loop(0, ch, issue, 0, unroll=8)
    rows = pl.ds(base, ch)
    if scatter:
        pltpu.make_async_copy(src_ref, dst_ref.at[rows, :], sem).wait()
    else:
        pltpu.make_async_copy(src_ref.at[rows, :], dst_ref, sem).wait()


def permute_rows(src, idx, *, scatter, ch=2048):
    n, width = src.shape
    ch = min(ch, n)
    tile = pl.BlockSpec((ch, width), lambda i, idx: (i, 0))
    hbm = pl.BlockSpec(memory_space=pl.ANY)
    return pl.pallas_call(
        functools.partial(_permute_kernel, ch=ch, scatter=scatter),
        grid_spec=pltpu.PrefetchScalarGridSpec(
            num_scalar_prefetch=1,
            grid=(n // ch,),
            in_specs=[tile if scatter else hbm],
            out_specs=hbm if scatter else tile,
            scratch_shapes=[pltpu.SemaphoreType.DMA(())],
        ),
        out_shape=jax.ShapeDtypeStruct((n, width), src.dtype),
        compiler_params=_cparams(("arbitrary",)),
        name="permute_rows",
    )(idx, src)


def _rope_cols(y, cosf, sina, sinb):
    blocks = []
    for c in range(y.shape[1] // LANES):
        xb = y[:, c * LANES:(c + 1) * LANES]
        blocks.append(xb * cosf + pltpu.roll(xb, LANES - ROPE_DIM // 2, 1) * sina
                      + pltpu.roll(xb, ROPE_DIM // 2, 1) * sinb)
    return blocks


def _rope_rows(kt, cost, sint):
    half = ROPE_DIM // 2
    parts = []
    for g in range(kt.shape[0] // DIFF_DH):
        b0 = g * DIFF_DH
        x1 = kt[b0:b0 + half]
        x2 = kt[b0 + half:b0 + ROPE_DIM]
        parts += [x1 * cost - x2 * sint, x2 * cost + x1 * sint, kt[b0 + ROPE_DIM:b0 + DIFF_DH]]
    return jnp.concatenate(parts, axis=0)


def _qkv_kernel(*refs, tm, k_feature_major):
    if k_feature_major:
        (h_ref, gn_ref, wk_ref, wv_ref, wq_ref, cos_ref, sa_ref, sb_ref, cost_ref, sint_ref,
         k32_ref, v32_ref, kb_ref, vb_ref, qb_ref) = refs
    else:
        (h_ref, gn_ref, wk_ref, wv_ref, wq_ref, cos_ref, sa_ref, sb_ref,
         k32_ref, v32_ref, kb_ref, vb_ref, qb_ref) = refs
    cosf = cos_ref[...]
    sina = sa_ref[...]
    sinb = sb_ref[...]
    h = h_ref[...]
    inv = lax.rsqrt(jnp.mean(h * h, axis=-1, keepdims=True) + EPS)
    ukv = (h * inv * gn_ref[0:1, :]).astype(BF16)
    umix = (h * inv * gn_ref[1:2, :]).astype(BF16)
    if k_feature_major:
        kt = lax.dot_general(wk_ref[...], ukv, (((1,), (1,)), ((), ())), preferred_element_type=F32)
        kt = _rope_rows(kt, cost_ref[...], sint_ref[...])
        k32_ref[0] = kt
        kb_ref[0, 0] = kt.astype(BF16)
    else:
        k = jnp.dot(ukv, wk_ref[...], preferred_element_type=F32)
        for c, blk in enumerate(_rope_cols(k, cosf, sina, sinb)):
            k32_ref[:, c * LANES:(c + 1) * LANES] = blk
            kb_ref[:, c * LANES:(c + 1) * LANES] = blk.astype(BF16)
    v = jnp.dot(ukv, wv_ref[...], preferred_element_type=F32)
    vb_ref[...] = v.astype(BF16)
    for h in range(DIFF_HEADS):
        v32_ref[pl.ds(h, tm, stride=DIFF_HEADS), :] = v[:, h * DIFF_DV:(h + 1) * DIFF_DV]
    q = jnp.dot(umix, wq_ref[...], preferred_element_type=F32)
    for c, blk in enumerate(_rope_cols(q, cosf, sina, sinb)):
        qb_ref[:, c * LANES:(c + 1) * LANES] = (blk * QK_SCALE_LOG2).astype(BF16)


def qkv_proj(h, gains, wk, wv, wq, cosf, sina, sinb, cost=None, sint=None, *, tm, batch, seq):
    n, d = h.shape
    k_feature_major = cost is not None
    nt = max(seq // tm, 1)
    n_pos_tiles = cosf.shape[0] // tm
    tok = lambda i: (i, 0)
    wspec = pl.BlockSpec((d, d), lambda i: (0, 0))
    tspec = pl.BlockSpec((tm, LANES), lambda i: (i % n_pos_tiles, 0))
    in_specs = [pl.BlockSpec((tm, d), tok), pl.BlockSpec((2, d), lambda i: (0, 0)), wspec, wspec, wspec,
                tspec, tspec, tspec]
    args = [h, gains, wk, wv, wq, cosf, sina, sinb]
    rows_f32 = (pl.BlockSpec((tm, d), tok), jax.ShapeDtypeStruct((n, d), F32))
    rows_bf16 = (pl.BlockSpec((tm, d), tok), jax.ShapeDtypeStruct((n, d), BF16))
    v32 = (pl.BlockSpec((tm * DIFF_HEADS, DIFF_DV), tok), jax.ShapeDtypeStruct((n * DIFF_HEADS, DIFF_DV), F32))
    if k_feature_major:
        half = ROPE_DIM // 2
        rspec = pl.BlockSpec((half, tm), lambda i: (0, i % nt))
        in_specs += [rspec, rspec]
        args += [cost, sint]
        k32 = (pl.BlockSpec((1, d, tm), lambda i: (i // nt, 0, i % nt)),
               jax.ShapeDtypeStruct((batch, d, seq), F32))
        kb = (pl.BlockSpec((1, 1, d, tm), lambda i: (i // nt, i % nt, 0, 0)),
              jax.ShapeDtypeStruct((batch, nt, d, tm), BF16))
    else:
        k32, kb = rows_f32, rows_bf16
    outs = [k32, v32, kb, rows_bf16, rows_bf16]
    return pl.pallas_call(
        functools.partial(_qkv_kernel, tm=tm, k_feature_major=k_feature_major),
        grid=(n // tm,),
        in_specs=in_specs,
        out_specs=[o[0] for o in outs],
        out_shape=[o[1] for o in outs],
        compiler_params=_cparams(("parallel",)),
        name="qkv_proj",
    )(*args)


def _lambda(lam_ref, lam_init):
    lv = lam_ref[...]
    a = jnp.sum(lv[0:1, :] * lv[1:2, :], axis=-1, keepdims=True)
    b = jnp.sum(lv[2:3, :] * lv[3:4, :], axis=-1, keepdims=True)
    return jnp.exp(a) - jnp.exp(b) + lam_init


def _split_q(q):
    qf = q.astype(F32)
    lane = lax.broadcasted_iota(jnp.int32, qf.shape, 1)
    q0 = jnp.where(lane < DIFF_DH, qf, 0.0).astype(BF16)
    q1 = jnp.where(lane >= DIFF_DH, qf, 0.0).astype(BF16)
    return jnp.concatenate([q0, q1], axis=0)


def _online_update(s, v, m_ref, l_ref, acc_ref):
    m_old = m_ref[...]
    if s.shape[1] < LANES:
        m_new = jnp.maximum(m_old, jnp.max(s, axis=-1, keepdims=True))
        alpha = jnp.exp2(m_old - m_new)
        p = jnp.exp2(s - m_new[:, :s.shape[1]])
        lane = lax.broadcasted_iota(jnp.int32, m_old.shape, 1)
        lsum = jnp.where(lane == 0, jnp.sum(p, axis=-1, keepdims=True), 0.0)
        l_ref[...] = alpha * l_ref[...] + lsum
        acc_ref[...] = alpha * acc_ref[...] + jnp.dot(p.astype(BF16), v, preferred_element_type=F32)
        m_ref[...] = m_new
        return
    ncol = s.shape[1] // LANES
    cols = [s[:, c * LANES:(c + 1) * LANES] for c in range(ncol)]
    mx = cols[0]
    for c in range(1, ncol):
        mx = jnp.maximum(mx, cols[c])
    m_new = jnp.maximum(m_old, jnp.max(mx, axis=-1, keepdims=True))
    alpha = jnp.exp2(m_old - m_new)
    ps = [jnp.exp2(col - m_new) for col in cols]
    lsum = ps[0]
    for c in range(1, ncol):
        lsum = lsum + ps[c]
    p = jnp.concatenate([pc.astype(BF16) for pc in ps], axis=1) if ncol > 1 else ps[0].astype(BF16)
    l_ref[...] = alpha * l_ref[...] + lsum
    acc_ref[...] = alpha * acc_ref[...] + jnp.dot(p, v, preferred_element_type=F32)
    m_ref[...] = m_new


def _diff_finish(l_ref, acc_ref, lam, subln, lam_init, tq):
    o = acc_ref[...] / jnp.sum(l_ref[...], axis=-1, keepdims=True)
    a = o[:tq] - lam * o[tq:]
    return _rms(a, subln) * (1.0 - lam_init)


def _attn_prompt_kernel(q_ref, k_ref, v_ref, lam_ref, sub_ref, o_ref, q2_ref, m_ref, l_ref, acc_ref,
                        *, tq, lam_init):
    i = pl.program_id(2)
    q2_ref[...] = _split_q(q_ref[...])
    m_ref[...] = jnp.full_like(m_ref, NEG_BIG)
    l_ref[...] = jnp.zeros_like(l_ref)
    acc_ref[...] = jnp.zeros_like(acc_ref)

    def step(j, nblk, diagonal_last):
        r0 = pl.multiple_of(j * tq, tq)
        v = v_ref[pl.ds(r0, nblk * tq), :]
        q2 = q2_ref[...]
        parts = [jnp.dot(q2, k_ref[0, j + t], preferred_element_type=F32) for t in range(nblk)]
        if diagonal_last:
            d = parts[-1]
            qrow = lax.broadcasted_iota(jnp.int32, d.shape, 0) % tq
            col = lax.broadcasted_iota(jnp.int32, d.shape, 1)
            parts[-1] = jnp.where(col < (qrow // CHUNK + 1) * CHUNK, d, NEG_BIG)
        s = parts[0] if nblk == 1 else jnp.concatenate(parts, axis=1)
        _online_update(s, v, m_ref, l_ref, acc_ref)

    def body(jj, carry):
        step(MAIN_BLOCKS * jj, MAIN_BLOCKS, False)
        return carry

    n_main = i // MAIN_BLOCKS
    lax.fori_loop(0, n_main, body, 0)
    for rem in range(MAIN_BLOCKS):
        pl.when(i % MAIN_BLOCKS == rem)(
            functools.partial(step, n_main * MAIN_BLOCKS, rem + 1, True))

    lam = _lambda(lam_ref, lam_init)
    o_ref[...] = _diff_finish(l_ref, acc_ref, lam, sub_ref[...], lam_init, tq).astype(BF16)


def attn_prompt(qb, kb, vb, lamv, subln, *, batch, seq, tq, lam_init):
    n = batch * seq
    nq = seq // tq
    kblk = tq
    return pl.pallas_call(
        functools.partial(_attn_prompt_kernel, tq=tq, lam_init=lam_init),
        grid=(batch, DIFF_HEADS, nq),
        in_specs=[
            pl.BlockSpec((tq, LANES), lambda b, h, i: (b * nq + i, h)),
            pl.BlockSpec((1, seq // kblk, LANES, kblk), lambda b, h, i: (b, 0, h, 0)),
            pl.BlockSpec((seq, LANES), lambda b, h, i: (b, h)),
            pl.BlockSpec((8, LANES), lambda b, h, i: (0, 0)),
            pl.BlockSpec((1, LANES), lambda b, h, i: (0, 0)),
        ],
        out_specs=pl.BlockSpec((tq, LANES), lambda b, h, i: (b * nq + i, h)),
        out_shape=jax.ShapeDtypeStruct((n, DIFF_HEADS * DIFF_DV), BF16),
        scratch_shapes=[pltpu.VMEM((2 * tq, LANES), BF16), pltpu.VMEM((2 * tq, LANES), F32),
                        pltpu.VMEM((2 * tq, LANES), F32), pltpu.VMEM((2 * tq, LANES), F32)],
        compiler_params=_cparams(("parallel", "parallel", "arbitrary")),
        name="attn_prompt",
    )(qb, kb, vb, lamv, subln)


def _attn_sample_kernel(q_ref, ck_ref, cv_ref, kn_ref, vn_ref, lam_ref, sub_ref, o_ref,
                        q2_ref, m_ref, l_ref, acc_ref, *, tq, past, lam_init):
    j = pl.program_id(1)
    last = pl.num_programs(1) - 1

    @pl.when(j == 0)
    def _():
        for h in range(DIFF_HEADS):
            q2_ref[h] = _split_q(q_ref[:, h * LANES:(h + 1) * LANES])
        m_ref[...] = jnp.full_like(m_ref, NEG_BIG)
        l_ref[...] = jnp.zeros_like(l_ref)
        acc_ref[...] = jnp.zeros_like(acc_ref)

    @pl.when(j < last)
    def _():
        tk = ck_ref.shape[-1]
        scores = [jnp.dot(q2_ref[h], ck_ref[0, h].astype(BF16), preferred_element_type=F32)
                  for h in range(DIFF_HEADS)]
        for h in range(DIFF_HEADS):
            v = cv_ref[pl.ds(h, tk, stride=DIFF_HEADS), :].astype(BF16)
            _online_update(scores[h], v, m_ref.at[h], l_ref.at[h], acc_ref.at[h])

    @pl.when(j == last)
    def _():
        lam = _lambda(lam_ref, lam_init)
        for h in range(DIFF_HEADS):
            k = kn_ref[:, h * LANES:(h + 1) * LANES]
            v = vn_ref[:, h * LANES:(h + 1) * LANES]
            s = lax.dot_general(q2_ref[h], k, (((1,), (1,)), ((), ())), preferred_element_type=F32)
            qpos = past + lax.broadcasted_iota(jnp.int32, s.shape, 0) % tq
            kpos = past + lax.broadcasted_iota(jnp.int32, s.shape, 1)
            s = jnp.where(kpos < (qpos // CHUNK + 1) * CHUNK, s, NEG_BIG)
            _online_update(s, v, m_ref.at[h], l_ref.at[h], acc_ref.at[h])
            o_ref[:, h * LANES:(h + 1) * LANES] = _diff_finish(
                l_ref.at[h], acc_ref.at[h], lam, sub_ref[...], lam_init, tq).astype(BF16)


def attn_sample(qb, cache_k, cache_v, kb, vb, lamv, subln, *, batch, seq, tk, lam_init):
    past = cache_k.shape[-1]
    d = DIFF_HEADS * DIFF_DV
    nk = past // tk
    ck_spec = pl.BlockSpec((1, DIFF_HEADS, LANES, tk), lambda b, j: (b, 0, 0, jnp.minimum(j, nk - 1)))
    cv_spec = pl.BlockSpec((tk * DIFF_HEADS, DIFF_DV), lambda b, j: (b * nk + jnp.minimum(j, nk - 1), 0))
    row_spec = pl.BlockSpec((seq, d), lambda b, j: (b, 0))
    return pl.pallas_call(
        functools.partial(_attn_sample_kernel, tq=seq, past=past, lam_init=lam_init),
        grid=(batch, nk + 1),
        in_specs=[row_spec, ck_spec, cv_spec, row_spec, row_spec,
                  pl.BlockSpec((8, LANES), lambda b, j: (0, 0)),
                  pl.BlockSpec((1, LANES), lambda b, j: (0, 0))],
        out_specs=row_spec,
        out_shape=jax.ShapeDtypeStruct((batch * seq, d), BF16),
        scratch_shapes=[pltpu.VMEM((DIFF_HEADS, 2 * seq, LANES), BF16),
                        pltpu.VMEM((DIFF_HEADS, 2 * seq, LANES), F32),
                        pltpu.VMEM((DIFF_HEADS, 2 * seq, LANES), F32),
                        pltpu.VMEM((DIFF_HEADS, 2 * seq, LANES), F32)],
        compiler_params=_cparams(("parallel", "arbitrary")),
        name="attn_sample",
    )(qb, cache_k, cache_v, kb, vb, lamv, subln)


def _angles(pos, half, theta):
    pos = np.asarray(pos).astype(np.float64)
    inv_freq = np.power(np.float64(theta), -np.arange(half, dtype=np.float64) / half)
    return pos[:, None] * inv_freq[None, :]


def _ret_rope_tables(pos):
    ang = _angles(pos, RET_DK // 2, RET_THETA)
    return np.cos(ang).astype(np.float32), np.sin(ang).astype(np.float32)


def _diff_rope_tables(pos):
    half = ROPE_DIM // 2
    ang = _angles(pos, half, ROPE_THETA)
    cos, sin = np.cos(ang).astype(np.float32), np.sin(ang).astype(np.float32)
    t = cos.shape[0]
    pad = DIFF_DH - ROPE_DIM
    cos64 = np.concatenate([cos, cos, np.ones((t, pad), np.float32)], axis=1)
    sina64 = np.concatenate([-sin, np.zeros((t, DIFF_DH - half), np.float32)], axis=1)
    sinb64 = np.concatenate([np.zeros((t, half), np.float32), sin, np.zeros((t, pad), np.float32)], axis=1)
    rep = LANES // DIFF_DH
    return (np.tile(cos64, (1, rep)), np.tile(sina64, (1, rep)), np.tile(sinb64, (1, rep)),
            np.ascontiguousarray(cos.T), np.ascontiguousarray(sin.T))


def _prep_params(p):
    w = {}
    w['ret_w_in'] = p['ret_w_in'][0].astype(BF16)
    w['ret_w_out'] = p['ret_w_out'][0].astype(BF16)
    w['kv_w_k'] = p['kv_w_k'].astype(BF16)
    w['kv_w_k_t'] = p['kv_w_k'].T.astype(BF16)
    w['kv_w_v'] = p['kv_w_v'].astype(BF16)
    w['diff_w_q'] = p['diff_w_q'][0].astype(BF16)
    w['diff_w_o'] = p['diff_w_o'][0].astype(BF16)
    w['moe_w_gate'] = p['moe_w_gate'].astype(BF16)
    w['moe_w_up'] = p['moe_w_up'].astype(BF16)
    w['moe_w_down'] = p['moe_w_down'].astype(BF16)
    wr, br = [], []
    for layer in range(2):
        route = jnp.transpose(p['moe_w_route'][layer], (1, 0, 2)).reshape(D_MODEL, N_EXPERTS)
        cols = jnp.concatenate([route, p['moe_w_group'][layer]], axis=1)
        full = jnp.pad(cols, ((0, 0), (0, ROUTE_LANES - cols.shape[1])))
        hi = full.astype(BF16)
        wr.append((hi, (full - hi.astype(F32)).astype(BF16)))
        bias = jnp.concatenate([p['moe_b_route'][layer].reshape(-1), p['moe_b_group'][layer]])
        br.append(jnp.pad(bias, (0, ROUTE_LANES - bias.shape[0]))[None, :])
    w['route_w'] = wr
    w['route_b'] = br
    lamv = jnp.concatenate([p['diff_lam_q1'][0][None], p['diff_lam_k1'][0][None],
                            p['diff_lam_q2'][0][None], p['diff_lam_k2'][0][None]], axis=0)
    w['lamv'] = jnp.pad(lamv, ((0, 4), (0, LANES - DIFF_DH)))
    return w


def _group_sort_tables(hx, tm):
    n = hx.shape[0]
    n_tiles = n // tm
    i32 = jnp.int32
    g = hx[:, D_MODEL + GROUP_ID_LANE].astype(i32)
    onehot = (g[:, None] == jnp.arange(N_GROUPS, dtype=i32)[None, :]).astype(i32)
    csum = jnp.cumsum(onehot, axis=0)
    ends = jnp.cumsum(csum[-1])
    starts = ends - csum[-1]
    pos = jnp.sum(onehot * (starts[None, :] + csum - 1), axis=1)
    inner_ends = ends[:N_GROUPS - 1]
    lo = jnp.sort(jnp.concatenate([jnp.arange(n_tiles, dtype=i32) * tm, inner_ends]))
    hi = jnp.concatenate([lo[1:], jnp.full((1,), n, i32)])
    tile = jnp.minimum(lo // tm, n_tiles - 1)
    group = jnp.sum((inner_ends[None, :] <= lo[:, None]).astype(i32), axis=1)
    change = (tile[1:] != tile[:-1]).astype(i32)
    one = jnp.ones((1,), i32)
    return pos, (tile, group, lo, hi, jnp.concatenate([one, change]), jnp.concatenate([change, one]))


def _moe_layer(hx, g_ffn, wg, wu, wd, g_final, *, tm):
    if hx.shape[0] < 8 * tm:
        return moe(hx, g_ffn, wg, wu, wd, g_final, tm=tm)
    pos, items = _group_sort_tables(hx, tm)
    xs = permute_rows(hx, pos, scatter=True)
    ys = moe(xs, g_ffn, wg, wu, wd, g_final, items, tm=tm)
    return permute_rows(ys, pos, scatter=False)


def _trunk(x, pos, r0, past_k, past_v, p, w, *, tm, ret_chunk, ret_tb, attn_tq=None, attn_tk=None):
    batch, seq, d = x.shape
    n = batch * seq
    xf = x.reshape(n, d)
    row = lambda v: v.reshape(1, -1)

    cos, sin = _ret_rope_tables(pos)
    if seq < tm:
        cos, sin = np.tile(cos, (tm // seq, 1)), np.tile(sin, (tm // seq, 1))
    proj = ret_inproj(xf, row(p['norm_mix'][0]), w['ret_w_in'], cos, sin, tm=min(2 * tm, seq, n) if seq >= tm else tm)
    o_gated, r_new = retention(proj, r0, batch=batch, seq=seq, L=ret_chunk, tb=ret_tb)
    hx1 = outproj_route(o_gated, w['ret_w_out'], xf, row(p['norm_ffn'][0]),
                        w['route_w'][0], w['route_b'][0], tm=min(2 * tm, n))
    h2 = _moe_layer(hx1, row(p['norm_ffn'][0]), w['moe_w_gate'][0], w['moe_w_up'][0], w['moe_w_down'][0],
                    None, tm=tm)
    gn = jnp.stack([p['kv_norm'], p['norm_mix'][1]])

    cosf, sina, sinb, cost, sint = _diff_rope_tables(pos)
    if seq < tm:
        rep = (tm // seq, 1)
        cosf, sina, sinb = np.tile(cosf, rep), np.tile(sina, rep), np.tile(sinb, rep)
    lam_init = 0.8 - 0.6 * math.exp(-0.3 * 1)
    subln = row(p['diff_subln'][0])
    if past_k is None:
        assert attn_tq == tm
        k32, v32, kb, vb, qb = qkv_proj(h2, gn, w['kv_w_k_t'], w['kv_w_v'], w['diff_w_q'],
                                        cosf, sina, sinb, cost, sint, tm=tm, batch=batch, seq=seq)
        attn = attn_prompt(qb, kb, vb, w['lamv'], subln, batch=batch, seq=seq, tq=attn_tq,
                           lam_init=lam_init)
        k_out = jnp.transpose(k32.reshape(batch, DIFF_HEADS, 2, DIFF_DH, seq), (0, 4, 1, 2, 3))
    else:
        past = past_k.shape[1]
        k32, v32, kb, vb, qb = qkv_proj(h2, gn, w['kv_w_k'], w['kv_w_v'], w['diff_w_q'],
                                        cosf, sina, sinb, tm=tm, batch=batch, seq=seq)
        ck = jnp.transpose(past_k, (0, 2, 3, 4, 1)).reshape(batch, DIFF_HEADS, 2 * DIFF_DH, past)
        cv = past_v.reshape(batch * past * DIFF_HEADS, DIFF_DV)
        attn = attn_sample(qb, ck, cv, kb, vb, w['lamv'], subln, batch=batch, seq=seq, tk=attn_tk,
                           lam_init=lam_init)
        k_out = k32.reshape(batch, seq, DIFF_HEADS, 2, DIFF_DH)
    hx3 = outproj_route(attn, w['diff_w_o'], h2, row(p['norm_ffn'][1]),
                        w['route_w'][1], w['route_b'][1], tm=min(2 * tm, n))
    y = _moe_layer(hx3, row(p['norm_ffn'][1]), w['moe_w_gate'][1], w['moe_w_up'][1], w['moe_w_down'][1],
                   row(p['norm_final']), tm=tm)

    return (y.reshape(batch, seq, d), r_new[None], k_out, v32.reshape(batch, seq, DIFF_HEADS, DIFF_DV))


def kernel(x_prompt, x_sample, state_ret, cache_k, cache_v, norm_mix, norm_ffn, norm_final, ret_w_in, ret_w_out, kv_norm, kv_w_k, kv_w_v, diff_w_q, diff_lam_q1, diff_lam_k1, diff_lam_q2, diff_lam_k2, diff_subln, diff_w_o, moe_w_group, moe_b_group, moe_w_route, moe_b_route, moe_w_gate, moe_w_up, moe_w_down):
    p = {
        'norm_mix': norm_mix, 'norm_ffn': norm_ffn, 'norm_final': norm_final,
        'ret_w_in': ret_w_in, 'ret_w_out': ret_w_out,
        'kv_norm': kv_norm, 'kv_w_k': kv_w_k, 'kv_w_v': kv_w_v,
        'diff_w_q': diff_w_q, 'diff_lam_q1': diff_lam_q1, 'diff_lam_k1': diff_lam_k1,
        'diff_lam_q2': diff_lam_q2, 'diff_lam_k2': diff_lam_k2, 'diff_subln': diff_subln,
        'diff_w_o': diff_w_o,
        'moe_w_group': moe_w_group, 'moe_b_group': moe_b_group, 'moe_w_route': moe_w_route,
        'moe_b_route': moe_b_route, 'moe_w_gate': moe_w_gate, 'moe_w_up': moe_w_up,
        'moe_w_down': moe_w_down,
    }
    w = _prep_params(p)
    tp = x_prompt.shape[1]
    ts = x_sample.shape[1]
    past = cache_k.shape[1]
    y_p, r_p, k_p, v_p = _trunk(x_prompt, np.arange(tp), None, None, None, p, w,
                                tm=512, ret_chunk=256, ret_tb=512, attn_tq=512)
    y_s, r_s, k_s, v_s = _trunk(x_sample, past + np.arange(ts), state_ret[0], cache_k, cache_v, p, w,
                                tm=x_sample.shape[0] * ts, ret_chunk=ts, ret_tb=ts, attn_tk=2048)
    return (y_p, y_s, r_p, k_p, v_p, r_s, k_s, v_s)
```

```python
import functools
import math

import numpy as np
import jax
import jax.numpy as jnp
from jax import lax
from jax.experimental import pallas as pl
from jax.experimental.pallas import tpu as pltpu

F32 = jnp.float32
BF16 = jnp.bfloat16

D_MODEL = 1024
CHUNK = 64
RET_HEADS = 4
RET_DK = 256
RET_DV = 512
RET_THETA = 10000.0
DIFF_HEADS = 8
DIFF_DH = 64
DIFF_DV = 128
ROPE_THETA = 500000.0
ROPE_DIM = 16
N_GROUPS = 4
EXPERTS_PER_GROUP = 4
N_EXPERTS = 16
D_EXPERT = 256
EPS = 1e-6

LANES = 128
VMEM_LIMIT = 56 * 1024 * 1024
ROUTE_LANES = 128
GROUP_LANE0 = N_EXPERTS
GROUP_ID_LANE = GROUP_LANE0 + N_GROUPS
HX_WIDTH = D_MODEL + ROUTE_LANES
NEG_BIG = -1e30
QK_SCALE_LOG2 = DIFF_DH ** -0.5 * math.log2(math.e)
MAIN_BLOCKS = 4


def _cparams(sem):
    return pltpu.CompilerParams(dimension_semantics=sem, vmem_limit_bytes=VMEM_LIMIT)


def _rms(x, g):
    ms = jnp.mean(x * x, axis=-1, keepdims=True)
    return x * lax.rsqrt(ms + EPS) * g


def _silu(x):
    return x * (1.0 / (1.0 + jnp.exp(-x)))


def _ret_inproj_kernel(x_ref, g_ref, w_ref, cos_ref, sin_ref, o_ref, xn_ref, *, tn):
    j = pl.program_id(1)
    n_q = (RET_HEADS * RET_DK) // tn
    n_qk = 2 * n_q
    n_qkv = n_qk + (RET_HEADS * RET_DV) // tn

    @pl.when(j == 0)
    def _():
        xn_ref[...] = _rms(x_ref[...], g_ref[...]).astype(BF16)

    def proj():
        return jnp.dot(xn_ref[...], w_ref[...], preferred_element_type=F32)

    @pl.when(j < n_qk)
    def _():
        y = proj()
        cos = cos_ref[...]
        sin = sin_ref[...]
        scale = jnp.where(j >= n_q, RET_DK ** -0.5, 1.0).astype(F32)
        half = RET_DK // 2
        for h in range(tn // RET_DK):
            x1 = y[:, h * RET_DK:h * RET_DK + half]
            x2 = y[:, h * RET_DK + half:(h + 1) * RET_DK]
            o_ref[:, h * RET_DK:h * RET_DK + half] = ((x1 * cos - x2 * sin) * scale).astype(BF16)
            o_ref[:, h * RET_DK + half:(h + 1) * RET_DK] = ((x2 * cos + x1 * sin) * scale).astype(BF16)

    @pl.when(jnp.logical_and(j >= n_qk, j < n_qkv))
    def _():
        o_ref[...] = proj().astype(BF16)

    @pl.when(j >= n_qkv)
    def _():
        o_ref[...] = _silu(proj()).astype(BF16)


def ret_inproj(x, g, w, cos, sin, *, tm, tn=1024):
    n, d = x.shape
    m = w.shape[1]
    n_pos_tiles = cos.shape[0] // tm
    return pl.pallas_call(
        functools.partial(_ret_inproj_kernel, tn=tn),
        grid=(n // tm, m // tn),
        in_specs=[
            pl.BlockSpec((tm, d), lambda i, j: (i, 0)),
            pl.BlockSpec((1, d), lambda i, j: (0, 0)),
            pl.BlockSpec((d, tn), lambda i, j: (0, j)),
            pl.BlockSpec((tm, RET_DK // 2), lambda i, j: (i % n_pos_tiles, 0)),
            pl.BlockSpec((tm, RET_DK // 2), lambda i, j: (i % n_pos_tiles, 0)),
        ],
        out_specs=pl.BlockSpec((tm, tn), lambda i, j: (i, j)),
        out_shape=jax.ShapeDtypeStruct((n, m), BF16),
        scratch_shapes=[pltpu.VMEM((tm, d), BF16)],
        compiler_params=_cparams(("parallel", "arbitrary")),
        name="ret_inproj",
    )(x, g, w, cos, sin)


def _retention_kernel(*refs, L, n_chunks, has_r0):
    if has_r0:
        (q_ref, k_ref, v_ref, g_ref, dm_ref, xi_ref, zt_ref, r0_ref, o_ref, rout_ref, r_ref) = refs
    else:
        (q_ref, k_ref, v_ref, g_ref, dm_ref, xi_ref, zt_ref, o_ref, rout_ref, r_ref) = refs
    t = pl.program_id(1)

    @pl.when(t == 0)
    def _():
        if has_r0:
            r_ref[...] = r0_ref[0]
        else:
            r_ref[...] = jnp.zeros_like(r_ref)

    def chunk(c, carry):
        r0 = pl.multiple_of(c * L, L)
        for h in range(RET_HEADS):
            lg = math.log1p(-2.0 ** (-5.0 - h))
            q = q_ref[pl.ds(r0, L), h * RET_DK:(h + 1) * RET_DK]
            k = k_ref[pl.ds(r0, L), h * RET_DK:(h + 1) * RET_DK]
            v = v_ref[pl.ds(r0, L), h * RET_DV:(h + 1) * RET_DV]
            gate = g_ref[pl.ds(r0, L), h * RET_DV:(h + 1) * RET_DV]
            rh = r_ref[h]
            s = lax.dot_general(q, k, (((1,), (1,)), ((), ())), preferred_element_type=F32)
            s = s * dm_ref[h]
            intra = jnp.dot(s.astype(BF16), v, preferred_element_type=F32)
            inter = jnp.dot(q, rh.astype(BF16), preferred_element_type=F32) * xi_ref[h]
            o = intra + inter
            kz = (k.astype(F32) * zt_ref[h]).astype(BF16)
            upd = lax.dot_general(kz, v, (((0,), (0,)), ((), ())), preferred_element_type=F32)
            r_ref[h] = math.exp(lg * L) * rh + upd
            on = o * lax.rsqrt(jnp.mean(o * o, axis=-1, keepdims=True) + EPS)
            o_ref[pl.ds(r0, L), h * RET_DV:(h + 1) * RET_DV] = (on * gate.astype(F32)).astype(BF16)
        return carry

    lax.fori_loop(0, n_chunks, chunk, 0)

    @pl.when(t == pl.num_programs(1) - 1)
    def _():
        rout_ref[0] = r_ref[...]


def _retention_tables(L):
    lg = np.log1p(-np.exp2(-5.0 - np.arange(RET_HEADS, dtype=np.float64)))
    i = np.arange(L, dtype=np.float64)
    diff = i[:, None] - i[None, :]
    dmask = np.where(diff >= 0, np.exp(lg[:, None, None] * np.maximum(diff, 0.0)), 0.0)
    xi = np.exp(lg[:, None] * (i[None, :] + 1.0))
    zeta = np.exp(lg[:, None] * (L - 1.0 - i[None, :]))
    xi = np.broadcast_to(xi[:, :, None], (RET_HEADS, L, RET_DV))
    zeta = np.broadcast_to(zeta[:, :, None], (RET_HEADS, L, RET_DK))
    return (jnp.asarray(dmask, F32), jnp.asarray(xi, F32), jnp.asarray(zeta, F32))


def retention(proj, r0, *, batch, seq, L, tb):
    n = batch * seq
    nt = seq // tb
    dmask, xi, zeta = _retention_tables(L)
    dqk = RET_HEADS * RET_DK
    dv = RET_HEADS * RET_DV
    has_r0 = r0 is not None
    const3 = lambda b, t: (0, 0, 0)
    in_specs = [
        pl.BlockSpec((tb, dqk), lambda b, t: (b * nt + t, 0)),
        pl.BlockSpec((tb, dqk), lambda b, t: (b * nt + t, 1)),
        pl.BlockSpec((tb, dv), lambda b, t: (b * nt + t, 1)),
        pl.BlockSpec((tb, dv), lambda b, t: (b * nt + t, 2)),
        pl.BlockSpec((RET_HEADS, L, L), const3),
        pl.BlockSpec((RET_HEADS, L, RET_DV), const3),
        pl.BlockSpec((RET_HEADS, L, RET_DK), const3),
    ]
    args = [proj, proj, proj, proj, dmask, xi, zeta]
    state_spec = pl.BlockSpec((1, RET_HEADS, RET_DK, RET_DV), lambda b, t: (b, 0, 0, 0))
    if has_r0:
        in_specs.append(state_spec)
        args.append(r0)
    return pl.pallas_call(
        functools.partial(_retention_kernel, L=L, n_chunks=tb // L, has_r0=has_r0),
        grid=(batch, nt),
        in_specs=in_specs,
        out_specs=[pl.BlockSpec((tb, dv), lambda b, t: (b * nt + t, 0)), state_spec],
        out_shape=[jax.ShapeDtypeStruct((n, dv), BF16),
                   jax.ShapeDtypeStruct((batch, RET_HEADS, RET_DK, RET_DV), F32)],
        scratch_shapes=[pltpu.VMEM((RET_HEADS, RET_DK, RET_DV), F32)],
        compiler_params=_cparams(("parallel", "arbitrary")),
        name="retention",
    )(*args)


def _route(u, wr_hi, wr_lo, br):
    u_hi = u.astype(BF16)
    u_lo = (u - u_hi.astype(F32)).astype(BF16)
    logits = (jnp.dot(u_hi, wr_hi, preferred_element_type=F32)
              + jnp.dot(u_lo, wr_hi, preferred_element_type=F32)
              + jnp.dot(u_hi, wr_lo, preferred_element_type=F32)) + br
    lane_i = lax.broadcasted_iota(jnp.int32, logits.shape, 1)
    lane = lane_i.astype(F32)
    lane_grp = (lane_i // EXPERTS_PER_GROUP).astype(F32)
    big = jnp.float32(1 << 20)
    is_grp = jnp.logical_and(lane_i >= GROUP_LANE0, lane_i < GROUP_LANE0 + N_GROUPS)
    gl = jnp.where(is_grp, logits, -jnp.inf)
    gmax = jnp.max(gl, axis=-1, keepdims=True)
    gsum = jnp.sum(jnp.exp(gl - gmax), axis=-1, keepdims=True)
    p_top = 1.0 / gsum
    g_top = jnp.min(jnp.where(gl == gmax, lane, big), axis=-1, keepdims=True) - GROUP_LANE0
    sel = jnp.logical_and(lane_i < N_EXPERTS, lane_grp == g_top)
    el = jnp.where(sel, logits, -jnp.inf)
    emax = jnp.max(el, axis=-1, keepdims=True)
    ee = jnp.exp(el - emax)
    ep = ee / jnp.sum(ee, axis=-1, keepdims=True)
    epm = jnp.where(sel, ep, -1.0)
    m1 = jnp.max(epm, axis=-1, keepdims=True)
    i1 = jnp.min(jnp.where(epm == m1, lane, big), axis=-1, keepdims=True)
    epm2 = jnp.where(lane == i1, -1.0, epm)
    m2 = jnp.max(epm2, axis=-1, keepdims=True)
    i2 = jnp.min(jnp.where(epm2 == m2, lane, big), axis=-1, keepdims=True)
    denom = m1 + m2
    w1 = m1 / denom * p_top
    w2 = m2 / denom * p_top
    gate = jnp.where(lane == i1, w1, jnp.where(lane == i2, w2, 0.0))
    return jnp.where(lane_i == GROUP_ID_LANE, g_top, gate)


def _outproj_kernel(a_ref, w_ref, res_ref, g_ref, wrh_ref, wrl_ref, br_ref, hx_ref, *, rc):
    d = w_ref.shape[1]
    for c in range(a_ref.shape[0] // rc):
        rows = slice(c * rc, (c + 1) * rc)
        h = res_ref[rows, :] + jnp.dot(a_ref[rows, :], w_ref[...], preferred_element_type=F32)
        hx_ref[rows, :d] = h
        hx_ref[rows, d:] = _route(_rms(h, g_ref[...]), wrh_ref[...], wrl_ref[...], br_ref[...])


def outproj_route(a, w, res, g, wr, br, *, tm):
    n, kdim = a.shape
    d = w.shape[1]
    wr_hi, wr_lo = wr
    return pl.pallas_call(
        functools.partial(_outproj_kernel, rc=tm),
        grid=(n // tm,),
        in_specs=[
            pl.BlockSpec((tm, kdim), lambda i: (i, 0)),
            pl.BlockSpec((kdim, d), lambda i: (0, 0)),
            pl.BlockSpec((tm, d), lambda i: (i, 0)),
            pl.BlockSpec((1, d), lambda i: (0, 0)),
            pl.BlockSpec((d, ROUTE_LANES), lambda i: (0, 0)),
            pl.BlockSpec((d, ROUTE_LANES), lambda i: (0, 0)),
            pl.BlockSpec((1, ROUTE_LANES), lambda i: (0, 0)),
        ],
        out_specs=pl.BlockSpec((tm, HX_WIDTH), lambda i: (i, 0)),
        out_shape=jax.ShapeDtypeStruct((n, HX_WIDTH), F32),
        compiler_params=_cparams(("parallel",)),
        name="outproj_route",
    )(a, w, res, g, wr_hi, wr_lo, br)


def _group_experts(u, gate, grp, wg_ref, wu_ref, wd_ref):
    lane = lax.broadcasted_iota(jnp.int32, gate.shape, 1)
    contrib = None
    for e in range(EXPERTS_PER_GROUP):
        ge = jnp.sum(jnp.where(lane == grp * EXPERTS_PER_GROUP + e, gate, 0.0), axis=-1, keepdims=True)
        hg = jnp.dot(u, wg_ref[e], preferred_element_type=F32)
        hu = jnp.dot(u, wu_ref[e], preferred_element_type=F32)
        hid = (_silu(hg) * hu * ge).astype(BF16)
        d = jnp.dot(hid, wd_ref[e], preferred_element_type=F32)
        contrib = d if contrib is None else contrib + d
    return contrib


def _moe_kernel(*refs, sorted_rows, final_norm):
    if sorted_rows:
        tile_ref, group_ref, lo_ref, hi_ref, first_ref, last_ref = refs[:6]
        refs = refs[6:]
    if final_norm:
        hx_ref, gffn_ref, wg_ref, wu_ref, wd_ref, gfin_ref, out_ref, u_ref, acc_ref = refs
    else:
        hx_ref, gffn_ref, wg_ref, wu_ref, wd_ref, out_ref, u_ref, acc_ref = refs
    d = out_ref.shape[1]
    if sorted_rows:
        s = pl.program_id(0)
        grp = group_ref[s]
        is_first = first_ref[s] == 1
        is_last = last_ref[s] == 1
        lo = lo_ref[s]
        hi = hi_ref[s]
    else:
        grp = pl.program_id(1)
        is_first = grp == 0
        is_last = grp == pl.num_programs(1) - 1

    @pl.when(is_first)
    def _():
        u_ref[...] = _rms(hx_ref[:, :d], gffn_ref[...]).astype(BF16)
        acc_ref[...] = jnp.zeros_like(acc_ref)

    def accumulate():
        acc_ref[...] += _group_experts(u_ref[...], hx_ref[:, d:], grp, wg_ref, wu_ref, wd_ref)

    if sorted_rows:
        pl.when(hi > lo)(accumulate)
    else:
        accumulate()

    @pl.when(is_last)
    def _():
        h = hx_ref[:, :d] + acc_ref[...]
        out_ref[...] = _rms(h, gfin_ref[...]) if final_norm else h


def moe(hx, g_ffn, wg, wu, wd, g_final=None, items=None, *, tm):
    n = hx.shape[0]
    d = D_MODEL
    sorted_rows = items is not None
    final_norm = g_final is not None
    if sorted_rows:
        tile_of = lambda s, tile, group, *_: (tile[s], 0)
        group_of = lambda s, tile, group, *_: (group[s], 0, 0)
        const = lambda s, *_: (0, 0)
        grid = (items[0].shape[0],)
    else:
        tile_of = lambda i, g: (i, 0)
        group_of = lambda i, g: (g, 0, 0)
        const = lambda i, g: (0, 0)
        grid = (n // tm, N_GROUPS)
    in_specs = [
        pl.BlockSpec((tm, HX_WIDTH), tile_of),
        pl.BlockSpec((1, d), const),
        pl.BlockSpec((EXPERTS_PER_GROUP, d, D_EXPERT), group_of),
        pl.BlockSpec((EXPERTS_PER_GROUP, d, D_EXPERT), group_of),
        pl.BlockSpec((EXPERTS_PER_GROUP, D_EXPERT, d), group_of),
    ]
    args = [hx, g_ffn, wg, wu, wd]
    if final_norm:
        in_specs.append(pl.BlockSpec((1, d), const))
        args.append(g_final)
    return pl.pallas_call(
        functools.partial(_moe_kernel, sorted_rows=sorted_rows, final_norm=final_norm),
        grid_spec=pltpu.PrefetchScalarGridSpec(
            num_scalar_prefetch=6 if sorted_rows else 0,
            grid=grid,
            in_specs=in_specs,
            out_specs=pl.BlockSpec((tm, d), tile_of),
            scratch_shapes=[pltpu.VMEM((tm, d), BF16), pltpu.VMEM((tm, d), F32)],
        ),
        out_shape=jax.ShapeDtypeStruct((n, d), F32),
        compiler_params=_cparams(("arbitrary",) if sorted_rows else ("parallel", "arbitrary")),
        name="moe",
    )(*(tuple(items) if sorted_rows else ()), *args)


def _permute_kernel(idx_ref, src_ref, dst_ref, sem, *, ch, scatter):
    base = pl.program_id(0) * ch

    def row_copy(r):
        j = idx_ref[base + r]
        if scatter:
            return pltpu.make_async_copy(src_ref.at[pl.ds(r, 1), :], dst_ref.at[pl.ds(j, 1), :], sem)
        return pltpu.make_async_copy(src_ref.at[pl.ds(j, 1), :], dst_ref.at[pl.ds(r, 1), :], sem)

    def issue(r, carry):
        row_copy(r).start()
        return carry

    lax.fori_loop(0, ch, issue, 0, unroll=8)
    rows = pl.ds(base, ch)
    if scatter:
        pltpu.make_async_copy(src_ref, dst_ref.at[rows, :], sem).wait()
    else:
        pltpu.make_async_copy(src_ref.at[rows, :], dst_ref, sem).wait()


def permute_rows(src, idx, *, scatter, ch=2048):
    n, width = src.shape
    ch = min(ch, n)
    tile = pl.BlockSpec((ch, width), lambda i, idx: (i, 0))
    hbm = pl.BlockSpec(memory_space=pl.ANY)
    return pl.pallas_call(
        functools.partial(_permute_kernel, ch=ch, scatter=scatter),
        grid_spec=pltpu.PrefetchScalarGridSpec(
            num_scalar_prefetch=1,
            grid=(n // ch,),
            in_specs=[tile if scatter else hbm],
            out_specs=hbm if scatter else tile,
            scratch_shapes=[pltpu.SemaphoreType.DMA(())],
        ),
        out_shape=jax.ShapeDtypeStruct((n, width), src.dtype),
        compiler_params=_cparams(("arbitrary",)),
        name="permute_rows",
    )(idx, src)


def _rope_cols(y, cosf, sina, sinb):
    blocks = []
    for c in range(y.shape[1] // LANES):
        xb = y[:, c * LANES:(c + 1) * LANES]
        blocks.append(xb * cosf + pltpu.roll(xb, LANES - ROPE_DIM // 2, 1) * sina
                      + pltpu.roll(xb, ROPE_DIM // 2, 1) * sinb)
    return blocks


def _rope_rows(kt, cost, sint):
    half = ROPE_DIM // 2
    parts = []
    for g in range(kt.shape[0] // DIFF_DH):
        b0 = g * DIFF_DH
        x1 = kt[b0:b0 + half]
        x2 = kt[b0 + half:b0 + ROPE_DIM]
        parts += [x1 * cost - x2 * sint, x2 * cost + x1 * sint, kt[b0 + ROPE_DIM:b0 + DIFF_DH]]
    return jnp.concatenate(parts, axis=0)


def _qkv_kernel(*refs, tm, k_feature_major):
    if k_feature_major:
        (h_ref, gn_ref, wk_ref, wv_ref, wq_ref, cos_ref, sa_ref, sb_ref, cost_ref, sint_ref,
         k32_ref, v32_ref, kb_ref, vb_ref, qb_ref) = refs
    else:
        (h_ref, gn_ref, wk_ref, wv_ref, wq_ref, cos_ref, sa_ref, sb_ref,
         k32_ref, v32_ref, kb_ref, vb_ref, qb_ref) = refs
    cosf = cos_ref[...]
    sina = sa_ref[...]
    sinb = sb_ref[...]
    h = h_ref[...]
    inv = lax.rsqrt(jnp.mean(h * h, axis=-1, keepdims=True) + EPS)
    ukv = (h * inv * gn_ref[0:1, :]).astype(BF16)
    umix = (h * inv * gn_ref[1:2, :]).astype(BF16)
    if k_feature_major:
        kt = lax.dot_general(wk_ref[...], ukv, (((1,), (1,)), ((), ())), preferred_element_type=F32)
        kt = _rope_rows(kt, cost_ref[...], sint_ref[...])
        k32_ref[0] = kt
        kb_ref[0, 0] = kt.astype(BF16)
    else:
        k = jnp.dot(ukv, wk_ref[...], preferred_element_type=F32)
        for c, blk in enumerate(_rope_cols(k, cosf, sina, sinb)):
            k32_ref[:, c * LANES:(c + 1) * LANES] = blk
            kb_ref[:, c * LANES:(c + 1) * LANES] = blk.astype(BF16)
    v = jnp.dot(ukv, wv_ref[...], preferred_element_type=F32)
    vb_ref[...] = v.astype(BF16)
    for h in range(DIFF_HEADS):
        v32_ref[pl.ds(h, tm, stride=DIFF_HEADS), :] = v[:, h * DIFF_DV:(h + 1) * DIFF_DV]
    q = jnp.dot(umix, wq_ref[...], preferred_element_type=F32)
    for c, blk in enumerate(_rope_cols(q, cosf, sina, sinb)):
        qb_ref[:, c * LANES:(c + 1) * LANES] = (blk * QK_SCALE_LOG2).astype(BF16)


def qkv_proj(h, gains, wk, wv, wq, cosf, sina, sinb, cost=None, sint=None, *, tm, batch, seq):
    n, d = h.shape
    k_feature_major = cost is not None
    nt = max(seq // tm, 1)
    n_pos_tiles = cosf.shape[0] // tm
    tok = lambda i: (i, 0)
    wspec = pl.BlockSpec((d, d), lambda i: (0, 0))
    tspec = pl.BlockSpec((tm, LANES), lambda i: (i % n_pos_tiles, 0))
    in_specs = [pl.BlockSpec((tm, d), tok), pl.BlockSpec((2, d), lambda i: (0, 0)), wspec, wspec, wspec,
                tspec, tspec, tspec]
    args = [h, gains, wk, wv, wq, cosf, sina, sinb]
    rows_f32 = (pl.BlockSpec((tm, d), tok), jax.ShapeDtypeStruct((n, d), F32))
    rows_bf16 = (pl.BlockSpec((tm, d), tok), jax.ShapeDtypeStruct((n, d), BF16))
    v32 = (pl.BlockSpec((tm * DIFF_HEADS, DIFF_DV), tok), jax.ShapeDtypeStruct((n * DIFF_HEADS, DIFF_DV), F32))
    if k_feature_major:
        half = ROPE_DIM // 2
        rspec = pl.BlockSpec((half, tm), lambda i: (0, i % nt))
        in_specs += [rspec, rspec]
        args += [cost, sint]
        k32 = (pl.BlockSpec((1, d, tm), lambda i: (i // nt, 0, i % nt)),
               jax.ShapeDtypeStruct((batch, d, seq), F32))
        kb = (pl.BlockSpec((1, 1, d, tm), lambda i: (i // nt, i % nt, 0, 0)),
              jax.ShapeDtypeStruct((batch, nt, d, tm), BF16))
    else:
        k32, kb = rows_f32, rows_bf16
    outs = [k32, v32, kb, rows_bf16, rows_bf16]
    return pl.pallas_call(
        functools.partial(_qkv_kernel, tm=tm, k_feature_major=k_feature_major),
        grid=(n // tm,),
        in_specs=in_specs,
        out_specs=[o[0] for o in outs],
        out_shape=[o[1] for o in outs],
        compiler_params=_cparams(("parallel",)),
        name="qkv_proj",
    )(*args)


def _lambda(lam_ref, lam_init):
    lv = lam_ref[...]
    a = jnp.sum(lv[0:1, :] * lv[1:2, :], axis=-1, keepdims=True)
    b = jnp.sum(lv[2:3, :] * lv[3:4, :], axis=-1, keepdims=True)
    return jnp.exp(a) - jnp.exp(b) + lam_init


def _split_q(q):
    qf = q.astype(F32)
    lane = lax.broadcasted_iota(jnp.int32, qf.shape, 1)
    q0 = jnp.where(lane < DIFF_DH, qf, 0.0).astype(BF16)
    q1 = jnp.where(lane >= DIFF_DH, qf, 0.0).astype(BF16)
    return jnp.concatenate([q0, q1], axis=0)


def _online_update(s, v, m_ref, l_ref, acc_ref):
    m_old = m_ref[...]
    if s.shape[1] < LANES:
        m_new = jnp.maximum(m_old, jnp.max(s, axis=-1, keepdims=True))
        alpha = jnp.exp2(m_old - m_new)
        p = jnp.exp2(s - m_new[:, :s.shape[1]])
        lane = lax.broadcasted_iota(jnp.int32, m_old.shape, 1)
        lsum = jnp.where(lane == 0, jnp.sum(p, axis=-1, keepdims=True), 0.0)
        l_ref[...] = alpha * l_ref[...] + lsum
        acc_ref[...] = alpha * acc_ref[...] + jnp.dot(p.astype(BF16), v, preferred_element_type=F32)
        m_ref[...] = m_new
        return
    ncol = s.shape[1] // LANES
    cols = [s[:, c * LANES:(c + 1) * LANES] for c in range(ncol)]
    mx = cols[0]
    for c in range(1, ncol):
        mx = jnp.maximum(mx, cols[c])
    m_new = jnp.maximum(m_old, jnp.max(mx, axis=-1, keepdims=True))
    alpha = jnp.exp2(m_old - m_new)
    ps = [jnp.exp2(col - m_new) for col in cols]
    lsum = ps[0]
    for c in range(1, ncol):
        lsum = lsum + ps[c]
    p = jnp.concatenate([pc.astype(BF16) for pc in ps], axis=1) if ncol > 1 else ps[0].astype(BF16)
    l_ref[...] = alpha * l_ref[...] + lsum
    acc_ref[...] = alpha * acc_ref[...] + jnp.dot(p, v, preferred_element_type=F32)
    m_ref[...] = m_new


def _diff_finish(l_ref, acc_ref, lam, subln, lam_init, tq):
    o = acc_ref[...] / jnp.sum(l_ref[...], axis=-1, keepdims=True)
    a = o[:tq] - lam * o[tq:]
    return _rms(a, subln) * (1.0 - lam_init)


def _attn_prompt_kernel(q_ref, k_ref, v_ref, lam_ref, sub_ref, o_ref, q2_ref, m_ref, l_ref, acc_ref,
                        *, tq, lam_init):
    i = pl.program_id(2)
    q2_ref[...] = _split_q(q_ref[...])
    m_ref[...] = jnp.full_like(m_ref, NEG_BIG)
    l_ref[...] = jnp.zeros_like(l_ref)
    acc_ref[...] = jnp.zeros_like(acc_ref)

    def step(j, nblk, diagonal_last):
        r0 = pl.multiple_of(j * tq, tq)
        v = v_ref[pl.ds(r0, nblk * tq), :]
        q2 = q2_ref[...]
        parts = [jnp.dot(q2, k_ref[0, j + t], preferred_element_type=F32) for t in range(nblk)]
        if diagonal_last:
            d = parts[-1]
            qrow = lax.broadcasted_iota(jnp.int32, d.shape, 0) % tq
            col = lax.broadcasted_iota(jnp.int32, d.shape, 1)
            parts[-1] = jnp.where(col < (qrow // CHUNK + 1) * CHUNK, d, NEG_BIG)
        s = parts[0] if nblk == 1 else jnp.concatenate(parts, axis=1)
        _online_update(s, v, m_ref, l_ref, acc_ref)

    def body(jj, carry):
        step(MAIN_BLOCKS * jj, MAIN_BLOCKS, False)
        return carry

    n_main = i // MAIN_BLOCKS
    lax.fori_loop(0, n_main, body, 0)
    for rem in range(MAIN_BLOCKS):
        pl.when(i % MAIN_BLOCKS == rem)(
            functools.partial(step, n_main * MAIN_BLOCKS, rem + 1, True))

    lam = _lambda(lam_ref, lam_init)
    o_ref[...] = _diff_finish(l_ref, acc_ref, lam, sub_ref[...], lam_init, tq).astype(BF16)


def attn_prompt(qb, kb, vb, lamv, subln, *, batch, seq, tq, lam_init):
    n = batch * seq
    nq = seq // tq
    kblk = tq
    return pl.pallas_call(
        functools.partial(_attn_prompt_kernel, tq=tq, lam_init=lam_init),
        grid=(batch, DIFF_HEADS, nq),
        in_specs=[
            pl.BlockSpec((tq, LANES), lambda b, h, i: (b * nq + i, h)),
            pl.BlockSpec((1, seq // kblk, LANES, kblk), lambda b, h, i: (b, 0, h, 0)),
            pl.BlockSpec((seq, LANES), lambda b, h, i: (b, h)),
            pl.BlockSpec((8, LANES), lambda b, h, i: (0, 0)),
            pl.BlockSpec((1, LANES), lambda b, h, i: (0, 0)),
        ],
        out_specs=pl.BlockSpec((tq, LANES), lambda b, h, i: (b * nq + i, h)),
        out_shape=jax.ShapeDtypeStruct((n, DIFF_HEADS * DIFF_DV), BF16),
        scratch_shapes=[pltpu.VMEM((2 * tq, LANES), BF16), pltpu.VMEM((2 * tq, LANES), F32),
                        pltpu.VMEM((2 * tq, LANES), F32), pltpu.VMEM((2 * tq, LANES), F32)],
        compiler_params=_cparams(("parallel", "parallel", "arbitrary")),
        name="attn_prompt",
    )(qb, kb, vb, lamv, subln)


def _attn_sample_kernel(q_ref, ck_ref, cv_ref, kn_ref, vn_ref, lam_ref, sub_ref, o_ref,
                        q2_ref, m_ref, l_ref, acc_ref, *, tq, past, lam_init):
    j = pl.program_id(1)
    last = pl.num_programs(1) - 1

    @pl.when(j == 0)
    def _():
        for h in range(DIFF_HEADS):
            q2_ref[h] = _split_q(q_ref[:, h * LANES:(h + 1) * LANES])
        m_ref[...] = jnp.full_like(m_ref, NEG_BIG)
        l_ref[...] = jnp.zeros_like(l_ref)
        acc_ref[...] = jnp.zeros_like(acc_ref)

    @pl.when(j < last)
    def _():
        tk = ck_ref.shape[-1]
        scores = [jnp.dot(q2_ref[h], ck_ref[0, h].astype(BF16), preferred_element_type=F32)
                  for h in range(DIFF_HEADS)]
        for h in range(DIFF_HEADS):
            v = cv_ref[pl.ds(h, tk, stride=DIFF_HEADS), :].astype(BF16)
            _online_update(scores[h], v, m_ref.at[h], l_ref.at[h], acc_ref.at[h])

    @pl.when(j == last)
    def _():
        lam = _lambda(lam_ref, lam_init)
        for h in range(DIFF_HEADS):
            k = kn_ref[:, h * LANES:(h + 1) * LANES]
            v = vn_ref[:, h * LANES:(h + 1) * LANES]
            s = lax.dot_general(q2_ref[h], k, (((1,), (1,)), ((), ())), preferred_element_type=F32)
            qpos = past + lax.broadcasted_iota(jnp.int32, s.shape, 0) % tq
            kpos = past + lax.broadcasted_iota(jnp.int32, s.shape, 1)
            s = jnp.where(kpos < (qpos // CHUNK + 1) * CHUNK, s, NEG_BIG)
            _online_update(s, v, m_ref.at[h], l_ref.at[h], acc_ref.at[h])
            o_ref[:, h * LANES:(h + 1) * LANES] = _diff_finish(
                l_ref.at[h], acc_ref.at[h], lam, sub_ref[...], lam_init, tq).astype(BF16)


def attn_sample(qb, cache_k, cache_v, kb, vb, lamv, subln, *, batch, seq, tk, lam_init):
    past = cache_k.shape[-1]
    d = DIFF_HEADS * DIFF_DV
    nk = past // tk
    ck_spec = pl.BlockSpec((1, DIFF_HEADS, LANES, tk), lambda b, j: (b, 0, 0, jnp.minimum(j, nk - 1)))
    cv_spec = pl.BlockSpec((tk * DIFF_HEADS, DIFF_DV), lambda b, j: (b * nk + jnp.minimum(j, nk - 1), 0))
    row_spec = pl.BlockSpec((seq, d), lambda b, j: (b, 0))
    return pl.pallas_call(
        functools.partial(_attn_sample_kernel, tq=seq, past=past, lam_init=lam_init),
        grid=(batch, nk + 1),
        in_specs=[row_spec, ck_spec, cv_spec, row_spec, row_spec,
                  pl.BlockSpec((8, LANES), lambda b, j: (0, 0)),
                  pl.BlockSpec((1, LANES), lambda b, j: (0, 0))],
        out_specs=row_spec,
        out_shape=jax.ShapeDtypeStruct((batch * seq, d), BF16),
        scratch_shapes=[pltpu.VMEM((DIFF_HEADS, 2 * seq, LANES), BF16),
                        pltpu.VMEM((DIFF_HEADS, 2 * seq, LANES), F32),
                        pltpu.VMEM((DIFF_HEADS, 2 * seq, LANES), F32),
                        pltpu.VMEM((DIFF_HEADS, 2 * seq, LANES), F32)],
        compiler_params=_cparams(("parallel", "arbitrary")),
        name="attn_sample",
    )(qb, cache_k, cache_v, kb, vb, lamv, subln)


def _angles(pos, half, theta):
    pos = np.asarray(pos).astype(np.float64)
    inv_freq = np.power(np.float64(theta), -np.arange(half, dtype=np.float64) / half)
    return pos[:, None] * inv_freq[None, :]


def _ret_rope_tables(pos):
    ang = _angles(pos, RET_DK // 2, RET_THETA)
    return np.cos(ang).astype(np.float32), np.sin(ang).astype(np.float32)


def _diff_rope_tables(pos):
    half = ROPE_DIM // 2
    ang = _angles(pos, half, ROPE_THETA)
    cos, sin = np.cos(ang).astype(np.float32), np.sin(ang).astype(np.float32)
    t = cos.shape[0]
    pad = DIFF_DH - ROPE_DIM
    cos64 = np.concatenate([cos, cos, np.ones((t, pad), np.float32)], axis=1)
    sina64 = np.concatenate([-sin, np.zeros((t, DIFF_DH - half), np.float32)], axis=1)
    sinb64 = np.concatenate([np.zeros((t, half), np.float32), sin, np.zeros((t, pad), np.float32)], axis=1)
    rep = LANES // DIFF_DH
    return (np.tile(cos64, (1, rep)), np.tile(sina64, (1, rep)), np.tile(sinb64, (1, rep)),
            np.ascontiguousarray(cos.T), np.ascontiguousarray(sin.T))


def _prep_params(p):
    w = {}
    w['ret_w_in'] = p['ret_w_in'][0].astype(BF16)
    w['ret_w_out'] = p['ret_w_out'][0].astype(BF16)
    w['kv_w_k'] = p['kv_w_k'].astype(BF16)
    w['kv_w_k_t'] = p['kv_w_k'].T.astype(BF16)
    w['kv_w_v'] = p['kv_w_v'].astype(BF16)
    w['diff_w_q'] = p['diff_w_q'][0].astype(BF16)
    w['diff_w_o'] = p['diff_w_o'][0].astype(BF16)
    w['moe_w_gate'] = p['moe_w_gate'].astype(BF16)
    w['moe_w_up'] = p['moe_w_up'].astype(BF16)
    w['moe_w_down'] = p['moe_w_down'].astype(BF16)
    wr, br = [], []
    for layer in range(2):
        route = jnp.transpose(p['moe_w_route'][layer], (1, 0, 2)).reshape(D_MODEL, N_EXPERTS)
        cols = jnp.concatenate([route, p['moe_w_group'][layer]], axis=1)
        full = jnp.pad(cols, ((0, 0), (0, ROUTE_LANES - cols.shape[1])))
        hi = full.astype(BF16)
        wr.append((hi, (full - hi.astype(F32)).astype(BF16)))
        bias = jnp.concatenate([p['moe_b_route'][layer].reshape(-1), p['moe_b_group'][layer]])
        br.append(jnp.pad(bias, (0, ROUTE_LANES - bias.shape[0]))[None, :])
    w['route_w'] = wr
    w['route_b'] = br
    lamv = jnp.concatenate([p['diff_lam_q1'][0][None], p['diff_lam_k1'][0][None],
                            p['diff_lam_q2'][0][None], p['diff_lam_k2'][0][None]], axis=0)
    w['lamv'] = jnp.pad(lamv, ((0, 4), (0, LANES - DIFF_DH)))
    return w


def _group_sort_tables(hx, tm):
    n = hx.shape[0]
    n_tiles = n // tm
    i32 = jnp.int32
    g = hx[:, D_MODEL + GROUP_ID_LANE].astype(i32)
    onehot = (g[:, None] == jnp.arange(N_GROUPS, dtype=i32)[None, :]).astype(i32)
    csum = jnp.cumsum(onehot, axis=0)
    ends = jnp.cumsum(csum[-1])
    starts = ends - csum[-1]
    pos = jnp.sum(onehot * (starts[None, :] + csum - 1), axis=1)
    inner_ends = ends[:N_GROUPS - 1]
    lo = jnp.sort(jnp.concatenate([jnp.arange(n_tiles, dtype=i32) * tm, inner_ends]))
    hi = jnp.concatenate([lo[1:], jnp.full((1,), n, i32)])
    tile = jnp.minimum(lo // tm, n_tiles - 1)
    group = jnp.sum((inner_ends[None, :] <= lo[:, None]).astype(i32), axis=1)
    change = (tile[1:] != tile[:-1]).astype(i32)
    one = jnp.ones((1,), i32)
    return pos, (tile, group, lo, hi, jnp.concatenate([one, change]), jnp.concatenate([change, one]))


def _moe_layer(hx, g_ffn, wg, wu, wd, g_final, *, tm):
    if hx.shape[0] < 8 * tm:
        return moe(hx, g_ffn, wg, wu, wd, g_final, tm=tm)
    pos, items = _group_sort_tables(hx, tm)
    xs = permute_rows(hx, pos, scatter=True)
    ys = moe(xs, g_ffn, wg, wu, wd, g_final, items, tm=tm)
    return permute_rows(ys, pos, scatter=False)


def _trunk(x, pos, r0, past_k, past_v, p, w, *, tm, ret_chunk, ret_tb, attn_tq=None, attn_tk=None):
    batch, seq, d = x.shape
    n = batch * seq
    xf = x.reshape(n, d)
    row = lambda v: v.reshape(1, -1)

    cos, sin = _ret_rope_tables(pos)
    if seq < tm:
        cos, sin = np.tile(cos, (tm // seq, 1)), np.tile(sin, (tm // seq, 1))
    proj = ret_inproj(xf, row(p['norm_mix'][0]), w['ret_w_in'], cos, sin, tm=min(2 * tm, seq, n) if seq >= tm else tm)
    o_gated, r_new = retention(proj, r0, batch=batch, seq=seq, L=ret_chunk, tb=ret_tb)
    hx1 = outproj_route(o_gated, w['ret_w_out'], xf, row(p['norm_ffn'][0]),
                        w['route_w'][0], w['route_b'][0], tm=min(2 * tm, n))
    h2 = _moe_layer(hx1, row(p['norm_ffn'][0]), w['moe_w_gate'][0], w['moe_w_up'][0], w['moe_w_down'][0],
                    None, tm=tm)
    gn = jnp.stack([p['kv_norm'], p['norm_mix'][1]])

    cosf, sina, sinb, cost, sint = _diff_rope_tables(pos)
    if seq < tm:
        rep = (tm // seq, 1)
        cosf, sina, sinb = np.tile(cosf, rep), np.tile(sina, rep), np.tile(sinb, rep)
    lam_init = 0.8 - 0.6 * math.exp(-0.3 * 1)
    subln = row(p['diff_subln'][0])
    if past_k is None:
        assert attn_tq == tm
        k32, v32, kb, vb, qb = qkv_proj(h2, gn, w['kv_w_k_t'], w['kv_w_v'], w['diff_w_q'],
                                        cosf, sina, sinb, cost, sint, tm=tm, batch=batch, seq=seq)
        attn = attn_prompt(qb, kb, vb, w['lamv'], subln, batch=batch, seq=seq, tq=attn_tq,
                           lam_init=lam_init)
        k_out = jnp.transpose(k32.reshape(batch, DIFF_HEADS, 2, DIFF_DH, seq), (0, 4, 1, 2, 3))
    else:
        past = past_k.shape[1]
        k32, v32, kb, vb, qb = qkv_proj(h2, gn, w['kv_w_k'], w['kv_w_v'], w['diff_w_q'],
                                        cosf, sina, sinb, tm=tm, batch=batch, seq=seq)
        ck = jnp.transpose(past_k, (0, 2, 3, 4, 1)).reshape(batch, DIFF_HEADS, 2 * DIFF_DH, past)
        cv = past_v.reshape(batch * past * DIFF_HEADS, DIFF_DV)
        attn = attn_sample(qb, ck, cv, kb, vb, w['lamv'], subln, batch=batch, seq=seq, tk=attn_tk,
                           lam_init=lam_init)
        k_out = k32.reshape(batch, seq, DIFF_HEADS, 2, DIFF_DH)
    hx3 = outproj_route(attn, w['diff_w_o'], h2, row(p['norm_ffn'][1]),
                        w['route_w'][1], w['route_b'][1], tm=min(2 * tm, n))
    y = _moe_layer(hx3, row(p['norm_ffn'][1]), w['moe_w_gate'][1], w['moe_w_up'][1], w['moe_w_down'][1],
                   row(p['norm_final']), tm=tm)

    return (y.reshape(batch, seq, d), r_new[None], k_out, v32.reshape(batch, seq, DIFF_HEADS, DIFF_DV))


def kernel(x_prompt, x_sample, state_ret, cache_k, cache_v, norm_mix, norm_ffn, norm_final, ret_w_in, ret_w_out, kv_norm, kv_w_k, kv_w_v, diff_w_q, diff_lam_q1, diff_lam_k1, diff_lam_q2, diff_lam_k2, diff_subln, diff_w_o, moe_w_group, moe_b_group, moe_w_route, moe_b_route, moe_w_gate, moe_w_up, moe_w_down):
    p = {
        'norm_mix': norm_mix, 'norm_ffn': norm_ffn, 'norm_final': norm_final,
        'ret_w_in': ret_w_in, 'ret_w_out': ret_w_out,
        'kv_norm': kv_norm, 'kv_w_k': kv_w_k, 'kv_w_v': kv_w_v,
        'diff_w_q': diff_w_q, 'diff_lam_q1': diff_lam_q1, 'diff_lam_k1': diff_lam_k1,
        'diff_lam_q2': diff_lam_q2, 'diff_lam_k2': diff_lam_k2, 'diff_subln': diff_subln,
        'diff_w_o': diff_w_o,
        'moe_w_group': moe_w_group, 'moe_b_group': moe_b_group, 'moe_w_route': moe_w_route,
        'moe_b_route': moe_b_route, 'moe_w_gate': moe_w_gate, 'moe_w_up': moe_w_up,
        'moe_w_down': moe_w_down,
    }
    w = _prep_params(p)
    tp = x_prompt.shape[1]
    ts = x_sample.shape[1]
    past = cache_k.shape[1]
    y_p, r_p, k_p, v_p = _trunk(x_prompt, np.arange(tp), None, None, None, p, w,
                                tm=512, ret_chunk=256, ret_tb=512, attn_tq=512)
    y_s, r_s, k_s, v_s = _trunk(x_sample, past + np.arange(ts), state_ret[0], cache_k, cache_v, p, w,
                                tm=x_sample.shape[0] * ts, ret_chunk=ts, ret_tb=ts, attn_tk=1024)
    return (y_p, y_s, r_p, k_p, v_p, r_s, k_s, v_s)
```

```python
import functools
import math

import numpy as np
import jax
import jax.numpy as jnp
from jax import lax
from jax.experimental import pallas as pl
from jax.experimental.pallas import tpu as pltpu

F32 = jnp.float32
BF16 = jnp.bfloat16

D_MODEL = 1024
CHUNK = 64
RET_HEADS = 4
RET_DK = 256
RET_DV = 512
RET_THETA = 10000.0
DIFF_HEADS = 8
DIFF_DH = 64
DIFF_DV = 128
ROPE_THETA = 500000.0
ROPE_DIM = 16
N_GROUPS = 4
EXPERTS_PER_GROUP = 4
N_EXPERTS = 16
D_EXPERT = 256
EPS = 1e-6

LANES = 128
VMEM_LIMIT = 56 * 1024 * 1024
ROUTE_LANES = 128
GROUP_LANE0 = N_EXPERTS
GROUP_ID_LANE = GROUP_LANE0 + N_GROUPS
HX_WIDTH = D_MODEL + ROUTE_LANES
NEG_BIG = -1e30
QK_SCALE_LOG2 = DIFF_DH ** -0.5 * math.log2(math.e)
MAIN_BLOCKS = 4


def _cparams(sem):
    return pltpu.CompilerParams(dimension_semantics=sem, vmem_limit_bytes=VMEM_LIMIT)


def _rms(x, g):
    ms = jnp.mean(x * x, axis=-1, keepdims=True)
    return x * lax.rsqrt(ms + EPS) * g


def _silu(x):
    return x * (1.0 / (1.0 + jnp.exp(-x)))


def _ret_inproj_kernel(x_ref, g_ref, w_ref, cos_ref, sin_ref, o_ref, xn_ref, *, tn):
    j = pl.program_id(1)
    n_q = (RET_HEADS * RET_DK) // tn
    n_qk = 2 * n_q
    n_qkv = n_qk + (RET_HEADS * RET_DV) // tn

    @pl.when(j == 0)
    def _():
        xn_ref[...] = _rms(x_ref[...], g_ref[...]).astype(BF16)

    def proj():
        return jnp.dot(xn_ref[...], w_ref[...], preferred_element_type=F32)

    @pl.when(j < n_qk)
    def _():
        y = proj()
        cos = cos_ref[...]
        sin = sin_ref[...]
        scale = jnp.where(j >= n_q, RET_DK ** -0.5, 1.0).astype(F32)
        half = RET_DK // 2
        for h in range(tn // RET_DK):
            x1 = y[:, h * RET_DK:h * RET_DK + half]
            x2 = y[:, h * RET_DK + half:(h + 1) * RET_DK]
            o_ref[:, h * RET_DK:h * RET_DK + half] = ((x1 * cos - x2 * sin) * scale).astype(BF16)
            o_ref[:, h * RET_DK + half:(h + 1) * RET_DK] = ((x2 * cos + x1 * sin) * scale).astype(BF16)

    @pl.when(jnp.logical_and(j >= n_qk, j < n_qkv))
    def _():
        o_ref[...] = proj().astype(BF16)

    @pl.when(j >= n_qkv)
    def _():
        o_ref[...] = _silu(proj()).astype(BF16)


def ret_inproj(x, g, w, cos, sin, *, tm, tn=1024):
    n, d = x.shape
    m = w.shape[1]
    n_pos_tiles = cos.shape[0] // tm
    return pl.pallas_call(
        functools.partial(_ret_inproj_kernel, tn=tn),
        grid=(n // tm, m // tn),
        in_specs=[
            pl.BlockSpec((tm, d), lambda i, j: (i, 0)),
            pl.BlockSpec((1, d), lambda i, j: (0, 0)),
            pl.BlockSpec((d, tn), lambda i, j: (0, j)),
            pl.BlockSpec((tm, RET_DK // 2), lambda i, j: (i % n_pos_tiles, 0)),
            pl.BlockSpec((tm, RET_DK // 2), lambda i, j: (i % n_pos_tiles, 0)),
        ],
        out_specs=pl.BlockSpec((tm, tn), lambda i, j: (i, j)),
        out_shape=jax.ShapeDtypeStruct((n, m), BF16),
        scratch_shapes=[pltpu.VMEM((tm, d), BF16)],
        compiler_params=_cparams(("parallel", "arbitrary")),
        name="ret_inproj",
    )(x, g, w, cos, sin)


def _retention_kernel(*refs, L, n_chunks, has_r0):
    if has_r0:
        (q_ref, k_ref, v_ref, g_ref, dm_ref, xi_ref, zt_ref, r0_ref, o_ref, rout_ref, r_ref) = refs
    else:
        (q_ref, k_ref, v_ref, g_ref, dm_ref, xi_ref, zt_ref, o_ref, rout_ref, r_ref) = refs
    t = pl.program_id(1)

    @pl.when(t == 0)
    def _():
        if has_r0:
            r_ref[...] = r0_ref[0]
        else:
            r_ref[...] = jnp.zeros_like(r_ref)

    def chunk(c, carry):
        r0 = pl.multiple_of(c * L, L)
        for h in range(RET_HEADS):
            lg = math.log1p(-2.0 ** (-5.0 - h))
            q = q_ref[pl.ds(r0, L), h * RET_DK:(h + 1) * RET_DK]
            k = k_ref[pl.ds(r0, L), h * RET_DK:(h + 1) * RET_DK]
            v = v_ref[pl.ds(r0, L), h * RET_DV:(h + 1) * RET_DV]
            gate = g_ref[pl.ds(r0, L), h * RET_DV:(h + 1) * RET_DV]
            rh = r_ref[h]
            s = lax.dot_general(q, k, (((1,), (1,)), ((), ())), preferred_element_type=F32)
            s = s * dm_ref[h]
            intra = jnp.dot(s.astype(BF16), v, preferred_element_type=F32)
            inter = jnp.dot(q, rh.astype(BF16), preferred_element_type=F32) * xi_ref[h]
            o = intra + inter
            kz = (k.astype(F32) * zt_ref[h]).astype(BF16)
            upd = lax.dot_general(kz, v, (((0,), (0,)), ((), ())), preferred_element_type=F32)
            r_ref[h] = math.exp(lg * L) * rh + upd
            on = o * lax.rsqrt(jnp.mean(o * o, axis=-1, keepdims=True) + EPS)
            o_ref[pl.ds(r0, L), h * RET_DV:(h + 1) * RET_DV] = (on * gate.astype(F32)).astype(BF16)
        return carry

    lax.fori_loop(0, n_chunks, chunk, 0)

    @pl.when(t == pl.num_programs(1) - 1)
    def _():
        rout_ref[0] = r_ref[...]


def _retention_tables(L):
    lg = np.log1p(-np.exp2(-5.0 - np.arange(RET_HEADS, dtype=np.float64)))
    i = np.arange(L, dtype=np.float64)
    diff = i[:, None] - i[None, :]
    dmask = np.where(diff >= 0, np.exp(lg[:, None, None] * np.maximum(diff, 0.0)), 0.0)
    xi = np.exp(lg[:, None] * (i[None, :] + 1.0))
    zeta = np.exp(lg[:, None] * (L - 1.0 - i[None, :]))
    xi = np.broadcast_to(xi[:, :, None], (RET_HEADS, L, RET_DV))
    zeta = np.broadcast_to(zeta[:, :, None], (RET_HEADS, L, RET_DK))
    return (jnp.asarray(dmask, F32), jnp.asarray(xi, F32), jnp.asarray(zeta, F32))


def retention(proj, r0, *, batch, seq, L, tb):
    n = batch * seq
    nt = seq // tb
    dmask, xi, zeta = _retention_tables(L)
    dqk = RET_HEADS * RET_DK
    dv = RET_HEADS * RET_DV
    has_r0 = r0 is not None
    const3 = lambda b, t: (0, 0, 0)
    in_specs = [
        pl.BlockSpec((tb, dqk), lambda b, t: (b * nt + t, 0)),
        pl.BlockSpec((tb, dqk), lambda b, t: (b * nt + t, 1)),
        pl.BlockSpec((tb, dv), lambda b, t: (b * nt + t, 1)),
        pl.BlockSpec((tb, dv), lambda b, t: (b * nt + t, 2)),
        pl.BlockSpec((RET_HEADS, L, L), const3),
        pl.BlockSpec((RET_HEADS, L, RET_DV), const3),
        pl.BlockSpec((RET_HEADS, L, RET_DK), const3),
    ]
    args = [proj, proj, proj, proj, dmask, xi, zeta]
    state_spec = pl.BlockSpec((1, RET_HEADS, RET_DK, RET_DV), lambda b, t: (b, 0, 0, 0))
    if has_r0:
        in_specs.append(state_spec)
        args.append(r0)
    return pl.pallas_call(
        functools.partial(_retention_kernel, L=L, n_chunks=tb // L, has_r0=has_r0),
        grid=(batch, nt),
        in_specs=in_specs,
        out_specs=[pl.BlockSpec((tb, dv), lambda b, t: (b * nt + t, 0)), state_spec],
        out_shape=[jax.ShapeDtypeStruct((n, dv), BF16),
                   jax.ShapeDtypeStruct((batch, RET_HEADS, RET_DK, RET_DV), F32)],
        scratch_shapes=[pltpu.VMEM((RET_HEADS, RET_DK, RET_DV), F32)],
        compiler_params=_cparams(("parallel", "arbitrary")),
        name="retention",
    )(*args)


def _route(u, wr_hi, wr_lo, br):
    u_hi = u.astype(BF16)
    u_lo = (u - u_hi.astype(F32)).astype(BF16)
    logits = (jnp.dot(u_hi, wr_hi, preferred_element_type=F32)
              + jnp.dot(u_lo, wr_hi, preferred_element_type=F32)
              + jnp.dot(u_hi, wr_lo, preferred_element_type=F32)) + br
    lane_i = lax.broadcasted_iota(jnp.int32, logits.shape, 1)
    lane = lane_i.astype(F32)
    lane_grp = (lane_i // EXPERTS_PER_GROUP).astype(F32)
    big = jnp.float32(1 << 20)
    is_grp = jnp.logical_and(lane_i >= GROUP_LANE0, lane_i < GROUP_LANE0 + N_GROUPS)
    gl = jnp.where(is_grp, logits, -jnp.inf)
    gmax = jnp.max(gl, axis=-1, keepdims=True)
    gsum = jnp.sum(jnp.exp(gl - gmax), axis=-1, keepdims=True)
    p_top = 1.0 / gsum
    g_top = jnp.min(jnp.where(gl == gmax, lane, big), axis=-1, keepdims=True) - GROUP_LANE0
    sel = jnp.logical_and(lane_i < N_EXPERTS, lane_grp == g_top)
    el = jnp.where(sel, logits, -jnp.inf)
    emax = jnp.max(el, axis=-1, keepdims=True)
    ee = jnp.exp(el - emax)
    ep = ee / jnp.sum(ee, axis=-1, keepdims=True)
    epm = jnp.where(sel, ep, -1.0)
    m1 = jnp.max(epm, axis=-1, keepdims=True)
    i1 = jnp.min(jnp.where(epm == m1, lane, big), axis=-1, keepdims=True)
    epm2 = jnp.where(lane == i1, -1.0, epm)
    m2 = jnp.max(epm2, axis=-1, keepdims=True)
    i2 = jnp.min(jnp.where(epm2 == m2, lane, big), axis=-1, keepdims=True)
    denom = m1 + m2
    w1 = m1 / denom * p_top
    w2 = m2 / denom * p_top
    gate = jnp.where(lane == i1, w1, jnp.where(lane == i2, w2, 0.0))
    return jnp.where(lane_i == GROUP_ID_LANE, g_top, gate)


def _outproj_kernel(a_ref, w_ref, res_ref, g_ref, wrh_ref, wrl_ref, br_ref, hx_ref, *, rc):
    d = w_ref.shape[1]
    for c in range(a_ref.shape[0] // rc):
        rows = slice(c * rc, (c + 1) * rc)
        h = res_ref[rows, :] + jnp.dot(a_ref[rows, :], w_ref[...], preferred_element_type=F32)
        hx_ref[rows, :d] = h
        hx_ref[rows, d:] = _route(_rms(h, g_ref[...]), wrh_ref[...], wrl_ref[...], br_ref[...])


def outproj_route(a, w, res, g, wr, br, *, tm):
    n, kdim = a.shape
    d = w.shape[1]
    wr_hi, wr_lo = wr
    return pl.pallas_call(
        functools.partial(_outproj_kernel, rc=tm),
        grid=(n // tm,),
        in_specs=[
            pl.BlockSpec((tm, kdim), lambda i: (i, 0)),
            pl.BlockSpec((kdim, d), lambda i: (0, 0)),
            pl.BlockSpec((tm, d), lambda i: (i, 0)),
            pl.BlockSpec((1, d), lambda i: (0, 0)),
            pl.BlockSpec((d, ROUTE_LANES), lambda i: (0, 0)),
            pl.BlockSpec((d, ROUTE_LANES), lambda i: (0, 0)),
            pl.BlockSpec((1, ROUTE_LANES), lambda i: (0, 0)),
        ],
        out_specs=pl.BlockSpec((tm, HX_WIDTH), lambda i: (i, 0)),
        out_shape=jax.ShapeDtypeStruct((n, HX_WIDTH), F32),
        compiler_params=_cparams(("parallel",)),
        name="outproj_route",
    )(a, w, res, g, wr_hi, wr_lo, br)


def _group_experts(u, gate, grp, wg_ref, wu_ref, wd_ref):
    lane = lax.broadcasted_iota(jnp.int32, gate.shape, 1)
    contrib = None
    for e in range(EXPERTS_PER_GROUP):
        ge = jnp.sum(jnp.where(lane == grp * EXPERTS_PER_GROUP + e, gate, 0.0), axis=-1, keepdims=True)
        hg = jnp.dot(u, wg_ref[e], preferred_element_type=F32)
        hu = jnp.dot(u, wu_ref[e], preferred_element_type=F32)
        hid = (_silu(hg) * hu * ge).astype(BF16)
        d = jnp.dot(hid, wd_ref[e], preferred_element_type=F32)
        contrib = d if contrib is None else contrib + d
    return contrib


def _moe_kernel(*refs, sorted_rows, final_norm):
    if sorted_rows:
        tile_ref, group_ref, lo_ref, hi_ref, first_ref, last_ref = refs[:6]
        refs = refs[6:]
    if final_norm:
        hx_ref, gffn_ref, wg_ref, wu_ref, wd_ref, gfin_ref, out_ref, u_ref, acc_ref = refs
    else:
        hx_ref, gffn_ref, wg_ref, wu_ref, wd_ref, out_ref, u_ref, acc_ref = refs
    d = out_ref.shape[1]
    if sorted_rows:
        s = pl.program_id(0)
        grp = group_ref[s]
        is_first = first_ref[s] == 1
        is_last = last_ref[s] == 1
        lo = lo_ref[s]
        hi = hi_ref[s]
    else:
        grp = pl.program_id(1)
        is_first = grp == 0
        is_last = grp == pl.num_programs(1) - 1

    @pl.when(is_first)
    def _():
        u_ref[...] = _rms(hx_ref[:, :d], gffn_ref[...]).astype(BF16)
        acc_ref[...] = jnp.zeros_like(acc_ref)

    def accumulate():
        acc_ref[...] += _group_experts(u_ref[...], hx_ref[:, d:], grp, wg_ref, wu_ref, wd_ref)

    if sorted_rows:
        pl.when(hi > lo)(accumulate)
    else:
        accumulate()

    @pl.when(is_last)
    def _():
        h = hx_ref[:, :d] + acc_ref[...]
        out_ref[...] = _rms(h, gfin_ref[...]) if final_norm else h


def moe(hx, g_ffn, wg, wu, wd, g_final=None, items=None, *, tm):
    n = hx.shape[0]
    d = D_MODEL
    sorted_rows = items is not None
    final_norm = g_final is not None
    if sorted_rows:
        tile_of = lambda s, tile, group, *_: (tile[s], 0)
        group_of = lambda s, tile, group, *_: (group[s], 0, 0)
        const = lambda s, *_: (0, 0)
        grid = (items[0].shape[0],)
    else:
        tile_of = lambda i, g: (i, 0)
        group_of = lambda i, g: (g, 0, 0)
        const = lambda i, g: (0, 0)
        grid = (n // tm, N_GROUPS)
    in_specs = [
        pl.BlockSpec((tm, HX_WIDTH), tile_of),
        pl.BlockSpec((1, d), const),
        pl.BlockSpec((EXPERTS_PER_GROUP, d, D_EXPERT), group_of),
        pl.BlockSpec((EXPERTS_PER_GROUP, d, D_EXPERT), group_of),
        pl.BlockSpec((EXPERTS_PER_GROUP, D_EXPERT, d), group_of),
    ]
    args = [hx, g_ffn, wg, wu, wd]
    if final_norm:
        in_specs.append(pl.BlockSpec((1, d), const))
        args.append(g_final)
    return pl.pallas_call(
        functools.partial(_moe_kernel, sorted_rows=sorted_rows, final_norm=final_norm),
        grid_spec=pltpu.PrefetchScalarGridSpec(
            num_scalar_prefetch=6 if sorted_rows else 0,
            grid=grid,
            in_specs=in_specs,
            out_specs=pl.BlockSpec((tm, d), tile_of),
            scratch_shapes=[pltpu.VMEM((tm, d), BF16), pltpu.VMEM((tm, d), F32)],
        ),
        out_shape=jax.ShapeDtypeStruct((n, d), F32),
        compiler_params=_cparams(("arbitrary",) if sorted_rows else ("parallel", "arbitrary")),
        name="moe",
    )(*(tuple(items) if sorted_rows else ()), *args)


def _permute_kernel(idx_ref, src_ref, dst_ref, sem, *, ch, scatter):
    base = pl.program_id(0) * ch

    def row_copy(r):
        j = idx_ref[base + r]
        if scatter:
            return pltpu.make_async_copy(src_ref.at[pl.ds(r, 1), :], dst_ref.at[pl.ds(j, 1), :], sem)
        return pltpu.make_async_copy(src_ref.at[pl.ds(j, 1), :], dst_ref.at[pl.ds(r, 1), :], sem)

    def issue(r, carry):
        row_copy(r).start()
        return carry

    lax.fori_loop(0, ch, issue, 0, unroll=8)
    rows = pl.ds(base, ch)
    if scatter:
        pltpu.make_async_copy(src_ref, dst_ref.at[rows, :], sem).wait()
    else:
        pltpu.make_async_copy(src_ref.at[rows, :], dst_ref, sem).wait()


def permute_rows(src, idx, *, scatter, ch=2048):
    n, width = src.shape
    ch = min(ch, n)
    tile = pl.BlockSpec((ch, width), lambda i, idx: (i, 0))
    hbm = pl.BlockSpec(memory_space=pl.ANY)
    return pl.pallas_call(
        functools.partial(_permute_kernel, ch=ch, scatter=scatter),
        grid_spec=pltpu.PrefetchScalarGridSpec(
            num_scalar_prefetch=1,
            grid=(n // ch,),
            in_specs=[tile if scatter else hbm],
            out_specs=hbm if scatter else tile,
            scratch_shapes=[pltpu.SemaphoreType.DMA(())],
        ),
        out_shape=jax.ShapeDtypeStruct((n, width), src.dtype),
        compiler_params=_cparams(("arbitrary",)),
        name="permute_rows",
    )(idx, src)


def _rope_cols(y, cosf, sina, sinb):
    blocks = []
    for c in range(y.shape[1] // LANES):
        xb = y[:, c * LANES:(c + 1) * LANES]
        blocks.append(xb * cosf + pltpu.roll(xb, LANES - ROPE_DIM // 2, 1) * sina
                      + pltpu.roll(xb, ROPE_DIM // 2, 1) * sinb)
    return blocks


def _rope_rows(kt, cost, sint):
    half = ROPE_DIM // 2
    parts = []
    for g in range(kt.shape[0] // DIFF_DH):
        b0 = g * DIFF_DH
        x1 = kt[b0:b0 + half]
        x2 = kt[b0 + half:b0 + ROPE_DIM]
        parts += [x1 * cost - x2 * sint, x2 * cost + x1 * sint, kt[b0 + ROPE_DIM:b0 + DIFF_DH]]
    return jnp.concatenate(parts, axis=0)


def _qkv_kernel(*refs, tm, k_feature_major):
    if k_feature_major:
        (h_ref, gn_ref, wk_ref, wv_ref, wq_ref, cos_ref, sa_ref, sb_ref, cost_ref, sint_ref,
         k32_ref, v32_ref, kb_ref, vb_ref, qb_ref) = refs
    else:
        (h_ref, gn_ref, wk_ref, wv_ref, wq_ref, cos_ref, sa_ref, sb_ref,
         k32_ref, v32_ref, kb_ref, vb_ref, qb_ref) = refs
    cosf = cos_ref[...]
    sina = sa_ref[...]
    sinb = sb_ref[...]
    h = h_ref[...]
    inv = lax.rsqrt(jnp.mean(h * h, axis=-1, keepdims=True) + EPS)
    ukv = (h * inv * gn_ref[0:1, :]).astype(BF16)
    umix = (h * inv * gn_ref[1:2, :]).astype(BF16)
    if k_feature_major:
        kt = lax.dot_general(wk_ref[...], ukv, (((1,), (1,)), ((), ())), preferred_element_type=F32)
        kt = _rope_rows(kt, cost_ref[...], sint_ref[...])
        k32_ref[0] = kt
        kb_ref[0, 0] = kt.astype(BF16)
    else:
        k = jnp.dot(ukv, wk_ref[...], preferred_element_type=F32)
        for c, blk in enumerate(_rope_cols(k, cosf, sina, sinb)):
            k32_ref[:, c * LANES:(c + 1) * LANES] = blk
            kb_ref[:, c * LANES:(c + 1) * LANES] = blk.astype(BF16)
    v = jnp.dot(ukv, wv_ref[...], preferred_element_type=F32)
    vb_ref[...] = v.astype(BF16)
    for h in range(DIFF_HEADS):
        v32_ref[pl.ds(h, tm, stride=DIFF_HEADS), :] = v[:, h * DIFF_DV:(h + 1) * DIFF_DV]
    q = jnp.dot(umix, wq_ref[...], preferred_element_type=F32)
    for c, blk in enumerate(_rope_cols(q, cosf, sina, sinb)):
        qb_ref[:, c * LANES:(c + 1) * LANES] = (blk * QK_SCALE_LOG2).astype(BF16)


def qkv_proj(h, gains, wk, wv, wq, cosf, sina, sinb, cost=None, sint=None, *, tm, batch, seq):
    n, d = h.shape
    k_feature_major = cost is not None
    nt = max(seq // tm, 1)
    n_pos_tiles = cosf.shape[0] // tm
    tok = lambda i: (i, 0)
    wspec = pl.BlockSpec((d, d), lambda i: (0, 0))
    tspec = pl.BlockSpec((tm, LANES), lambda i: (i % n_pos_tiles, 0))
    in_specs = [pl.BlockSpec((tm, d), tok), pl.BlockSpec((2, d), lambda i: (0, 0)), wspec, wspec, wspec,
                tspec, tspec, tspec]
    args = [h, gains, wk, wv, wq, cosf, sina, sinb]
    rows_f32 = (pl.BlockSpec((tm, d), tok), jax.ShapeDtypeStruct((n, d), F32))
    rows_bf16 = (pl.BlockSpec((tm, d), tok), jax.ShapeDtypeStruct((n, d), BF16))
    v32 = (pl.BlockSpec((tm * DIFF_HEADS, DIFF_DV), tok), jax.ShapeDtypeStruct((n * DIFF_HEADS, DIFF_DV), F32))
    if k_feature_major:
        half = ROPE_DIM // 2
        rspec = pl.BlockSpec((half, tm), lambda i: (0, i % nt))
        in_specs += [rspec, rspec]
        args += [cost, sint]
        k32 = (pl.BlockSpec((1, d, tm), lambda i: (i // nt, 0, i % nt)),
               jax.ShapeDtypeStruct((batch, d, seq), F32))
        kb = (pl.BlockSpec((1, 1, d, tm), lambda i: (i // nt, i % nt, 0, 0)),
              jax.ShapeDtypeStruct((batch, nt, d, tm), BF16))
    else:
        k32, kb = rows_f32, rows_bf16
    outs = [k32, v32, kb, rows_bf16, rows_bf16]
    return pl.pallas_call(
        functools.partial(_qkv_kernel, tm=tm, k_feature_major=k_feature_major),
        grid=(n // tm,),
        in_specs=in_specs,
        out_specs=[o[0] for o in outs],
        out_shape=[o[1] for o in outs],
        compiler_params=_cparams(("parallel",)),
        name="qkv_proj",
    )(*args)


def _lambda(lam_ref, lam_init):
    lv = lam_ref[...]
    a = jnp.sum(lv[0:1, :] * lv[1:2, :], axis=-1, keepdims=True)
    b = jnp.sum(lv[2:3, :] * lv[3:4, :], axis=-1, keepdims=True)
    return jnp.exp(a) - jnp.exp(b) + lam_init


def _split_q(q):
    qf = q.astype(F32)
    lane = lax.broadcasted_iota(jnp.int32, qf.shape, 1)
    q0 = jnp.where(lane < DIFF_DH, qf, 0.0).astype(BF16)
    q1 = jnp.where(lane >= DIFF_DH, qf, 0.0).astype(BF16)
    return jnp.concatenate([q0, q1], axis=0)


def _online_update(s, v, m_ref, l_ref, acc_ref):
    m_old = m_ref[...]
    if s.shape[1] < LANES:
        m_new = jnp.maximum(m_old, jnp.max(s, axis=-1, keepdims=True))
        alpha = jnp.exp2(m_old - m_new)
        p = jnp.exp2(s - m_new[:, :s.shape[1]])
        lane = lax.broadcasted_iota(jnp.int32, m_old.shape, 1)
        lsum = jnp.where(lane == 0, jnp.sum(p, axis=-1, keepdims=True), 0.0)
        l_ref[...] = alpha * l_ref[...] + lsum
        acc_ref[...] = alpha * acc_ref[...] + jnp.dot(p.astype(BF16), v, preferred_element_type=F32)
        m_ref[...] = m_new
        return
    ncol = s.shape[1] // LANES
    cols = [s[:, c * LANES:(c + 1) * LANES] for c in range(ncol)]
    mx = cols[0]
    for c in range(1, ncol):
        mx = jnp.maximum(mx, cols[c])
    m_new = jnp.maximum(m_old, jnp.max(mx, axis=-1, keepdims=True))
    alpha = jnp.exp2(m_old - m_new)
    ps = [jnp.exp2(col - m_new) for col in cols]
    lsum = ps[0]
    for c in range(1, ncol):
        lsum = lsum + ps[c]
    p = jnp.concatenate([pc.astype(BF16) for pc in ps], axis=1) if ncol > 1 else ps[0].astype(BF16)
    l_ref[...] = alpha * l_ref[...] + lsum
    acc_ref[...] = alpha * acc_ref[...] + jnp.dot(p, v, preferred_element_type=F32)
    m_ref[...] = m_new


def _online_update_staged(parts, vs, m_ref, l_ref, acc_ref, s_ref):
    mx = None
    for t, s in enumerate(parts):
        s_ref[t] = s
        for c in range(s.shape[1] // LANES):
            col = s[:, c * LANES:(c + 1) * LANES]
            mx = col if mx is None else jnp.maximum(mx, col)
    m_old = m_ref[...]
    m_new = jnp.maximum(m_old, jnp.max(mx, axis=-1, keepdims=True))
    alpha = jnp.exp2(m_old - m_new)
    lsum = None
    pv = None
    for t in range(len(parts)):
        s = s_ref[t]
        ps = [jnp.exp2(s[:, c * LANES:(c + 1) * LANES] - m_new) for c in range(s.shape[1] // LANES)]
        for pc in ps:
            lsum = pc if lsum is None else lsum + pc
        d = jnp.dot(jnp.concatenate([pc.astype(BF16) for pc in ps], axis=1), vs[t],
                    preferred_element_type=F32)
        pv = d if pv is None else pv + d
    l_ref[...] = alpha * l_ref[...] + lsum
    acc_ref[...] = alpha * acc_ref[...] + pv
    m_ref[...] = m_new


def _diff_finish(l_ref, acc_ref, lam, subln, lam_init, tq):
    o = acc_ref[...] / jnp.sum(l_ref[...], axis=-1, keepdims=True)
    a = o[:tq] - lam * o[tq:]
    return _rms(a, subln) * (1.0 - lam_init)


def _attn_prompt_kernel(q_ref, k_ref, v_ref, lam_ref, sub_ref, o_ref, q2_ref, m_ref, l_ref, acc_ref,
                        s_ref, *, tq, lam_init):
    i = pl.program_id(2)
    q2_ref[...] = _split_q(q_ref[...])
    m_ref[...] = jnp.full_like(m_ref, NEG_BIG)
    l_ref[...] = jnp.zeros_like(l_ref)
    acc_ref[...] = jnp.zeros_like(acc_ref)

    def step(j, nblk, diagonal_last):
        r0 = pl.multiple_of(j * tq, tq)
        v = v_ref[pl.ds(r0, nblk * tq), :]
        q2 = q2_ref[...]
        parts = [jnp.dot(q2, k_ref[0, j + t], preferred_element_type=F32) for t in range(nblk)]
        if diagonal_last:
            d = parts[-1]
            qrow = lax.broadcasted_iota(jnp.int32, d.shape, 0) % tq
            col = lax.broadcasted_iota(jnp.int32, d.shape, 1)
            parts[-1] = jnp.where(col < (qrow // CHUNK + 1) * CHUNK, d, NEG_BIG)
        if nblk == 1:
            _online_update(parts[0], v, m_ref, l_ref, acc_ref)
        else:
            vs = [v[t * tq:(t + 1) * tq] for t in range(nblk)]
            _online_update_staged(parts, vs, m_ref, l_ref, acc_ref, s_ref)

    def body(jj, carry):
        step(MAIN_BLOCKS * jj, MAIN_BLOCKS, False)
        return carry

    n_main = i // MAIN_BLOCKS
    lax.fori_loop(0, n_main, body, 0)
    for rem in range(MAIN_BLOCKS):
        pl.when(i % MAIN_BLOCKS == rem)(
            functools.partial(step, n_main * MAIN_BLOCKS, rem + 1, True))

    lam = _lambda(lam_ref, lam_init)
    o_ref[...] = _diff_finish(l_ref, acc_ref, lam, sub_ref[...], lam_init, tq).astype(BF16)


def attn_prompt(qb, kb, vb, lamv, subln, *, batch, seq, tq, lam_init):
    n = batch * seq
    nq = seq // tq
    kblk = tq
    return pl.pallas_call(
        functools.partial(_attn_prompt_kernel, tq=tq, lam_init=lam_init),
        grid=(batch, DIFF_HEADS, nq),
        in_specs=[
            pl.BlockSpec((tq, LANES), lambda b, h, i: (b * nq + i, h)),
            pl.BlockSpec((1, seq // kblk, LANES, kblk), lambda b, h, i: (b, 0, h, 0)),
            pl.BlockSpec((seq, LANES), lambda b, h, i: (b, h)),
            pl.BlockSpec((8, LANES), lambda b, h, i: (0, 0)),
            pl.BlockSpec((1, LANES), lambda b, h, i: (0, 0)),
        ],
        out_specs=pl.BlockSpec((tq, LANES), lambda b, h, i: (b * nq + i, h)),
        out_shape=jax.ShapeDtypeStruct((n, DIFF_HEADS * DIFF_DV), BF16),
        scratch_shapes=[pltpu.VMEM((2 * tq, LANES), BF16), pltpu.VMEM((2 * tq, LANES), F32),
                        pltpu.VMEM((2 * tq, LANES), F32), pltpu.VMEM((2 * tq, LANES), F32),
                        pltpu.VMEM((MAIN_BLOCKS, 2 * tq, tq), F32)],
        compiler_params=_cparams(("parallel", "parallel", "arbitrary")),
        name="attn_prompt",
    )(qb, kb, vb, lamv, subln)


def _attn_sample_kernel(q_ref, ck_ref, cv_ref, kn_ref, vn_ref, lam_ref, sub_ref, o_ref,
                        q2_ref, m_ref, l_ref, acc_ref, *, tq, past, lam_init):
    j = pl.program_id(1)
    last = pl.num_programs(1) - 1

    @pl.when(j == 0)
    def _():
        for h in range(DIFF_HEADS):
            q2_ref[h] = _split_q(q_ref[:, h * LANES:(h + 1) * LANES])
        m_ref[...] = jnp.full_like(m_ref, NEG_BIG)
        l_ref[...] = jnp.zeros_like(l_ref)
        acc_ref[...] = jnp.zeros_like(acc_ref)

    @pl.when(j < last)
    def _():
        tk = ck_ref.shape[-1]
        scores = [jnp.dot(q2_ref[h], ck_ref[0, h].astype(BF16), preferred_element_type=F32)
                  for h in range(DIFF_HEADS)]
        for h in range(DIFF_HEADS):
            v = cv_ref[pl.ds(h, tk, stride=DIFF_HEADS), :].astype(BF16)
            _online_update(scores[h], v, m_ref.at[h], l_ref.at[h], acc_ref.at[h])

    @pl.when(j == last)
    def _():
        lam = _lambda(lam_ref, lam_init)
        for h in range(DIFF_HEADS):
            k = kn_ref[:, h * LANES:(h + 1) * LANES]
            v = vn_ref[:, h * LANES:(h + 1) * LANES]
            s = lax.dot_general(q2_ref[h], k, (((1,), (1,)), ((), ())), preferred_element_type=F32)
            qpos = past + lax.broadcasted_iota(jnp.int32, s.shape, 0) % tq
            kpos = past + lax.broadcasted_iota(jnp.int32, s.shape, 1)
            s = jnp.where(kpos < (qpos // CHUNK + 1) * CHUNK, s, NEG_BIG)
            _online_update(s, v, m_ref.at[h], l_ref.at[h], acc_ref.at[h])
            o_ref[:, h * LANES:(h + 1) * LANES] = _diff_finish(
                l_ref.at[h], acc_ref.at[h], lam, sub_ref[...], lam_init, tq).astype(BF16)


def attn_sample(qb, cache_k, cache_v, kb, vb, lamv, subln, *, batch, seq, tk, lam_init):
    past = cache_k.shape[-1]
    d = DIFF_HEADS * DIFF_DV
    nk = past // tk
    ck_spec = pl.BlockSpec((1, DIFF_HEADS, LANES, tk), lambda b, j: (b, 0, 0, jnp.minimum(j, nk - 1)))
    cv_spec = pl.BlockSpec((tk * DIFF_HEADS, DIFF_DV), lambda b, j: (b * nk + jnp.minimum(j, nk - 1), 0))
    row_spec = pl.BlockSpec((seq, d), lambda b, j: (b, 0))
    return pl.pallas_call(
        functools.partial(_attn_sample_kernel, tq=seq, past=past, lam_init=lam_init),
        grid=(batch, nk + 1),
        in_specs=[row_spec, ck_spec, cv_spec, row_spec, row_spec,
                  pl.BlockSpec((8, LANES), lambda b, j: (0, 0)),
                  pl.BlockSpec((1, LANES), lambda b, j: (0, 0))],
        out_specs=row_spec,
        out_shape=jax.ShapeDtypeStruct((batch * seq, d), BF16),
        scratch_shapes=[pltpu.VMEM((DIFF_HEADS, 2 * seq, LANES), BF16),
                        pltpu.VMEM((DIFF_HEADS, 2 * seq, LANES), F32),
                        pltpu.VMEM((DIFF_HEADS, 2 * seq, LANES), F32),
                        pltpu.VMEM((DIFF_HEADS, 2 * seq, LANES), F32)],
        compiler_params=_cparams(("parallel", "arbitrary")),
        name="attn_sample",
    )(qb, cache_k, cache_v, kb, vb, lamv, subln)


def _angles(pos, half, theta):
    pos = np.asarray(pos).astype(np.float64)
    inv_freq = np.power(np.float64(theta), -np.arange(half, dtype=np.float64) / half)
    return pos[:, None] * inv_freq[None, :]


def _ret_rope_tables(pos):
    ang = _angles(pos, RET_DK // 2, RET_THETA)
    return np.cos(ang).astype(np.float32), np.sin(ang).astype(np.float32)


def _diff_rope_tables(pos):
    half = ROPE_DIM // 2
    ang = _angles(pos, half, ROPE_THETA)
    cos, sin = np.cos(ang).astype(np.float32), np.sin(ang).astype(np.float32)
    t = cos.shape[0]
    pad = DIFF_DH - ROPE_DIM
    cos64 = np.concatenate([cos, cos, np.ones((t, pad), np.float32)], axis=1)
    sina64 = np.concatenate([-sin, np.zeros((t, DIFF_DH - half), np.float32)], axis=1)
    sinb64 = np.concatenate([np.zeros((t, half), np.float32), sin, np.zeros((t, pad), np.float32)], axis=1)
    rep = LANES // DIFF_DH
    return (np.tile(cos64, (1, rep)), np.tile(sina64, (1, rep)), np.tile(sinb64, (1, rep)),
            np.ascontiguousarray(cos.T), np.ascontiguousarray(sin.T))


def _prep_params(p):
    w = {}
    w['ret_w_in'] = p['ret_w_in'][0].astype(BF16)
    w['ret_w_out'] = p['ret_w_out'][0].astype(BF16)
    w['kv_w_k'] = p['kv_w_k'].astype(BF16)
    w['kv_w_k_t'] = p['kv_w_k'].T.astype(BF16)
    w['kv_w_v'] = p['kv_w_v'].astype(BF16)
    w['diff_w_q'] = p['diff_w_q'][0].astype(BF16)
    w['diff_w_o'] = p['diff_w_o'][0].astype(BF16)
    w['moe_w_gate'] = p['moe_w_gate'].astype(BF16)
    w['moe_w_up'] = p['moe_w_up'].astype(BF16)
    w['moe_w_down'] = p['moe_w_down'].astype(BF16)
    wr, br = [], []
    for layer in range(2):
        route = jnp.transpose(p['moe_w_route'][layer], (1, 0, 2)).reshape(D_MODEL, N_EXPERTS)
        cols = jnp.concatenate([route, p['moe_w_group'][layer]], axis=1)
        full = jnp.pad(cols, ((0, 0), (0, ROUTE_LANES - cols.shape[1])))
        hi = full.astype(BF16)
        wr.append((hi, (full - hi.astype(F32)).astype(BF16)))
        bias = jnp.concatenate([p['moe_b_route'][layer].reshape(-1), p['moe_b_group'][layer]])
        br.append(jnp.pad(bias, (0, ROUTE_LANES - bias.shape[0]))[None, :])
    w['route_w'] = wr
    w['route_b'] = br
    lamv = jnp.concatenate([p['diff_lam_q1'][0][None], p['diff_lam_k1'][0][None],
                            p['diff_lam_q2'][0][None], p['diff_lam_k2'][0][None]], axis=0)
    w['lamv'] = jnp.pad(lamv, ((0, 4), (0, LANES - DIFF_DH)))
    return w


def _group_sort_tables(hx, tm):
    n = hx.shape[0]
    n_tiles = n // tm
    i32 = jnp.int32
    g = hx[:, D_MODEL + GROUP_ID_LANE].astype(i32)
    onehot = (g[:, None] == jnp.arange(N_GROUPS, dtype=i32)[None, :]).astype(i32)
    csum = jnp.cumsum(onehot, axis=0)
    ends = jnp.cumsum(csum[-1])
    starts = ends - csum[-1]
    pos = jnp.sum(onehot * (starts[None, :] + csum - 1), axis=1)
    inner_ends = ends[:N_GROUPS - 1]
    lo = jnp.sort(jnp.concatenate([jnp.arange(n_tiles, dtype=i32) * tm, inner_ends]))
    hi = jnp.concatenate([lo[1:], jnp.full((1,), n, i32)])
    tile = jnp.minimum(lo // tm, n_tiles - 1)
    group = jnp.sum((inner_ends[None, :] <= lo[:, None]).astype(i32), axis=1)
    change = (tile[1:] != tile[:-1]).astype(i32)
    one = jnp.ones((1,), i32)
    return pos, (tile, group, lo, hi, jnp.concatenate([one, change]), jnp.concatenate([change, one]))


def _moe_layer(hx, g_ffn, wg, wu, wd, g_final, *, tm):
    if hx.shape[0] < 8 * tm:
        return moe(hx, g_ffn, wg, wu, wd, g_final, tm=tm)
    pos, items = _group_sort_tables(hx, tm)
    xs = permute_rows(hx, pos, scatter=True)
    ys = moe(xs, g_ffn, wg, wu, wd, g_final, items, tm=tm)
    return permute_rows(ys, pos, scatter=False)


def _trunk(x, pos, r0, past_k, past_v, p, w, *, tm, ret_chunk, ret_tb, attn_tq=None, attn_tk=None):
    batch, seq, d = x.shape
    n = batch * seq
    xf = x.reshape(n, d)
    row = lambda v: v.reshape(1, -1)

    cos, sin = _ret_rope_tables(pos)
    if seq < tm:
        cos, sin = np.tile(cos, (tm // seq, 1)), np.tile(sin, (tm // seq, 1))
    proj = ret_inproj(xf, row(p['norm_mix'][0]), w['ret_w_in'], cos, sin, tm=min(2 * tm, seq, n) if seq >= tm else tm)
    o_gated, r_new = retention(proj, r0, batch=batch, seq=seq, L=ret_chunk, tb=ret_tb)
    hx1 = outproj_route(o_gated, w['ret_w_out'], xf, row(p['norm_ffn'][0]),
                        w['route_w'][0], w['route_b'][0], tm=min(2 * tm, n))
    h2 = _moe_layer(hx1, row(p['norm_ffn'][0]), w['moe_w_gate'][0], w['moe_w_up'][0], w['moe_w_down'][0],
                    None, tm=tm)
    gn = jnp.stack([p['kv_norm'], p['norm_mix'][1]])

    cosf, sina, sinb, cost, sint = _diff_rope_tables(pos)
    if seq < tm:
        rep = (tm // seq, 1)
        cosf, sina, sinb = np.tile(cosf, rep), np.tile(sina, rep), np.tile(sinb, rep)
    lam_init = 0.8 - 0.6 * math.exp(-0.3 * 1)
    subln = row(p['diff_subln'][0])
    if past_k is None:
        assert attn_tq == tm
        k32, v32, kb, vb, qb = qkv_proj(h2, gn, w['kv_w_k_t'], w['kv_w_v'], w['diff_w_q'],
                                        cosf, sina, sinb, cost, sint, tm=tm, batch=batch, seq=seq)
        attn = attn_prompt(qb, kb, vb, w['lamv'], subln, batch=batch, seq=seq, tq=attn_tq,
                           lam_init=lam_init)
        k_out = jnp.transpose(k32.reshape(batch, DIFF_HEADS, 2, DIFF_DH, seq), (0, 4, 1, 2, 3))
    else:
        past = past_k.shape[1]
        k32, v32, kb, vb, qb = qkv_proj(h2, gn, w['kv_w_k'], w['kv_w_v'], w['diff_w_q'],
                                        cosf, sina, sinb, tm=tm, batch=batch, seq=seq)
        ck = jnp.transpose(past_k, (0, 2, 3, 4, 1)).reshape(batch, DIFF_HEADS, 2 * DIFF_DH, past)
        cv = past_v.reshape(batch * past * DIFF_HEADS, DIFF_DV)
        attn = attn_sample(qb, ck, cv, kb, vb, w['lamv'], subln, batch=batch, seq=seq, tk=attn_tk,
                           lam_init=lam_init)
        k_out = k32.reshape(batch, seq, DIFF_HEADS, 2, DIFF_DH)
    hx3 = outproj_route(attn, w['diff_w_o'], h2, row(p['norm_ffn'][1]),
                        w['route_w'][1], w['route_b'][1], tm=min(2 * tm, n))
    y = _moe_layer(hx3, row(p['norm_ffn'][1]), w['moe_w_gate'][1], w['moe_w_up'][1], w['moe_w_down'][1],
                   row(p['norm_final']), tm=tm)

    return (y.reshape(batch, seq, d), r_new[None], k_out, v32.reshape(batch, seq, DIFF_HEADS, DIFF_DV))


def kernel(x_prompt, x_sample, state_ret, cache_k, cache_v, norm_mix, norm_ffn, norm_final, ret_w_in, ret_w_out, kv_norm, kv_w_k, kv_w_v, diff_w_q, diff_lam_q1, diff_lam_k1, diff_lam_q2, diff_lam_k2, diff_subln, diff_w_o, moe_w_group, moe_b_group, moe_w_route, moe_b_route, moe_w_gate, moe_w_up, moe_w_down):
    p = {
        'norm_mix': norm_mix, 'norm_ffn': norm_ffn, 'norm_final': norm_final,
        'ret_w_in': ret_w_in, 'ret_w_out': ret_w_out,
        'kv_norm': kv_norm, 'kv_w_k': kv_w_k, 'kv_w_v': kv_w_v,
        'diff_w_q': diff_w_q, 'diff_lam_q1': diff_lam_q1, 'diff_lam_k1': diff_lam_k1,
        'diff_lam_q2': diff_lam_q2, 'diff_lam_k2': diff_lam_k2, 'diff_subln': diff_subln,
        'diff_w_o': diff_w_o,
        'moe_w_group': moe_w_group, 'moe_b_group': moe_b_group, 'moe_w_route': moe_w_route,
        'moe_b_route': moe_b_route, 'moe_w_gate': moe_w_gate, 'moe_w_up': moe_w_up,
        'moe_w_down': moe_w_down,
    }
    w = _prep_params(p)
    tp = x_prompt.shape[1]
    ts = x_sample.shape[1]
    past = cache_k.shape[1]
    y_p, r_p, k_p, v_p = _trunk(x_prompt, np.arange(tp), None, None, None, p, w,
                                tm=512, ret_chunk=256, ret_tb=512, attn_tq=512)
    y_s, r_s, k_s, v_s = _trunk(x_sample, past + np.arange(ts), state_ret[0], cache_k, cache_v, p, w,
                                tm=x_sample.shape[0] * ts, ret_chunk=ts, ret_tb=ts, attn_tk=1024)
    return (y_p, y_s, r_p, k_p, v_p, r_s, k_s, v_s)
```
